```python
import jax, jax.numpy as jnp
from jax import lax
import numpy as np

D_MODEL = 2048
BATCH = 1
SEQ = 8192
DEPTH = 1
DEC_BATCH = 128
DEC_SEQ = 8
PAST_LEN = 2048
PAGE_SIZE = 128

FOX_HEADS = 8
FOX_HD = 128
FOX_WIDTH = FOX_HEADS * FOX_HD
POOL_WINDOWS = (2, 4, 8, 16)
POOL_GROUPS = len(POOL_WINDOWS)
POOL_GW = 256
POOL_WIDTH = POOL_GROUPS * POOL_GW
POOL_STATE = max(POOL_WINDOWS) - 1
MEM_TOKENS = 256
MEM_HEADS = 4
MEM_HD = 256
MEM_WIDTH = MEM_HEADS * MEM_HD
N_BRANCH = 3
D_FF = 4 * D_MODEL
Q_BLOCK = 128
ALPHA = (2 * DEPTH) ** 0.25
BETA = (8 * DEPTH) ** -0.25
LN_EPS = 1e-5
FGATE_BIAS_INIT = 3.0
_IN_SIZES = (FOX_WIDTH, FOX_WIDTH, FOX_WIDTH, FOX_HEADS, POOL_WIDTH, MEM_WIDTH, N_BRANCH * D_MODEL)
IN_WIDTH = sum(_IN_SIZES)
IN_SPLITS = tuple(int(s) for s in np.cumsum(_IN_SIZES)[:-1])

kernel_name = 'fox_pool_memory_hybrid_step'


def _layer_norm(x, g, b):
    xf = x.astype(jnp.float32)
    mu = jnp.mean(xf, axis=-1, keepdims=True)
    var = jnp.mean(jnp.square(xf - mu), axis=-1, keepdims=True)
    return ((xf - mu) * lax.rsqrt(var + LN_EPS) * g.astype(jnp.float32) + b.astype(jnp.float32)).astype(x.dtype)


def _project_in(x, w_in, b_f):
    B, T, _ = x.shape
    q, k, v, f, u, qm, gz = jnp.split(x @ w_in, IN_SPLITS, axis=-1)
    q = q.reshape(B, T, FOX_HEADS, FOX_HD)
    k = k.reshape(B, T, FOX_HEADS, FOX_HD)
    v = v.reshape(B, T, FOX_HEADS, FOX_HD)
    logf = jax.nn.log_sigmoid((f + b_f).astype(jnp.float32))
    qm = qm.reshape(B, T, MEM_HEADS, MEM_HD)
    gates = jax.nn.sigmoid(gz.astype(jnp.float32)).astype(x.dtype).reshape(B, T, N_BRANCH, D_MODEL)
    return q, k, v, logf, u, qm, gates


def _fox_attend(q, k, v, c_q, c_k, pos_q, pos_k):
    s = jnp.einsum('bqhd,bkhd->bhqk', q, k, preferred_element_type=jnp.float32) * (FOX_HD ** -0.5)
    bias = jnp.swapaxes(c_q, 1, 2)[:, :, :, None] - jnp.swapaxes(c_k, 1, 2)[:, :, None, :]
    causal = pos_k[None, :] <= pos_q[:, None]
    s = jnp.where(causal, s + bias, -jnp.inf)
    p = jax.nn.softmax(s, axis=-1)
    return jnp.einsum('bhqk,bkhd->bqhd', p.astype(v.dtype), v)


def _fox_prompt(q, k, v, logf):
    B, S, H, D = q.shape
    c = jnp.cumsum(logf, axis=1)
    nb = S // Q_BLOCK
    qb = jnp.swapaxes(q.reshape(B, nb, Q_BLOCK, H, D), 0, 1)
    cb = jnp.swapaxes(c.reshape(B, nb, Q_BLOCK, H), 0, 1)
    pos_k = jnp.arange(S)

    def block(args):
        i, qi, ci = args
        pos_q = i * Q_BLOCK + jnp.arange(Q_BLOCK)
        return _fox_attend(qi, k, v, ci, c, pos_q, pos_k)

    o = lax.map(block, (jnp.arange(nb), qb, cb))
    return jnp.swapaxes(o, 0, 1).reshape(B, S, H * D)


def _fox_sample(q, k, v, logf, cache_k, cache_v, cache_logf, page_table):
    DB, T, H, D = q.shape
    past = page_table.shape[1] * PAGE_SIZE
    pk = cache_k[page_table].reshape(DB, past, H, D).astype(k.dtype)
    pv = cache_v[page_table].reshape(DB, past, H, D).astype(v.dtype)
    pf = cache_logf[page_table].reshape(DB, past, H).astype(jnp.float32)
    k_all = jnp.concatenate([pk, k], axis=1)
    v_all = jnp.concatenate([pv, v], axis=1)
    c_all = jnp.cumsum(jnp.concatenate([pf, logf], axis=1), axis=1)
    pos_k = jnp.arange(past + T)
    pos_q = past + jnp.arange(T)
    o = _fox_attend(q, k_all, v_all, c_all[:, past:], c_all, pos_q, pos_k)
    return o.reshape(DB, T, H * D)


def _pool(u_ext, pos, w_pool, pool_scale):
    B, L, _ = u_ext.shape
    T = L - POOL_STATE
    cs = jnp.cumsum(u_ext.astype(jnp.float32), axis=1)
    cs = jnp.concatenate([jnp.zeros_like(cs[:, :1]), cs], axis=1)
    end = cs[:, POOL_STATE + 1:]
    u_new = u_ext[:, POOL_STATE:].astype(jnp.float32)
    outs = []
    for g, w in enumerate(POOL_WINDOWS):
        sl = slice(g * POOL_GW, (g + 1) * POOL_GW)
        win = end[:, :, sl] - cs[:, POOL_STATE + 1 - w: POOL_STATE + 1 - w + T, sl]
        cnt = jnp.minimum(w, pos + 1).astype(jnp.float32)[None, :, None]
        outs.append(win / cnt - u_new[:, :, sl])
    d = jnp.stack(outs, axis=2)
    y = jnp.einsum('btgc,gcd->btgd', d.astype(w_pool.dtype), w_pool).reshape(B, T, POOL_WIDTH)
    return y * pool_scale


def _mem_kv(mem, w_mem_kv):
    B, M, _ = mem.shape
    mk, mv = jnp.split(mem @ w_mem_kv, 2, axis=-1)
    return mk.reshape(B, M, MEM_HEADS, MEM_HD), mv.reshape(B, M, MEM_HEADS, MEM_HD)


def _mem_attend(qm, mk, mv):
    B, T = qm.shape[:2]
    s = jnp.einsum('bqhd,bmhd->bhqm', qm, mk.astype(qm.dtype), preferred_element_type=jnp.float32) * (MEM_HD ** -0.5)
    p = jax.nn.softmax(s, axis=-1)
    return jnp.einsum('bhqm,bmhd->bqhd', p.astype(qm.dtype), mv.astype(qm.dtype)).reshape(B, T, MEM_WIDTH)


def _finish(x, o_fox, o_pool, o_mem, gates, w_br_fox, w_br_pool, w_br_mem, w_out,
            ln1_g, ln1_b, w_up, w_down, ln2_g, ln2_b):
    merged = (gates[:, :, 0] * (o_fox @ w_br_fox)
              + gates[:, :, 1] * (o_pool @ w_br_pool)
              + gates[:, :, 2] * (o_mem @ w_br_mem))
    h = _layer_norm(ALPHA * x + merged @ w_out, ln1_g, ln1_b)
    ff = jnp.square(jax.nn.relu(h @ w_up)) @ w_down
    return _layer_norm(ALPHA * h + ff, ln2_g, ln2_b)


def setup_inputs(seed: int = 0) -> dict:
    key = jax.random.key(seed)
    ks = jax.random.split(key, 32)
    n_pages = PAST_LEN // PAGE_SIZE
    n_used = DEC_BATCH * n_pages
    n_phys = n_used + max(1, n_used // 4)

    def nrm(k, shape, scale=1.0):
        return jax.random.normal(k, shape, jnp.float32) * scale

    def lin(k, fan_in, fan_out, scale=1.0):
        return nrm(k, (DEPTH, fan_in, fan_out), scale * fan_in ** -0.5)

    page_table = jax.random.permutation(ks[0], n_phys)[:n_used].reshape(DEC_BATCH, n_pages).astype(jnp.int32)
    return {
        'x_prompt': nrm(ks[1], (BATCH, SEQ, D_MODEL)),
        'x_sample': nrm(ks[2], (DEC_BATCH, DEC_SEQ, D_MODEL)),
        'mem_prompt': nrm(ks[3], (BATCH, MEM_TOKENS, D_MODEL)),
        'cache_fox_k': nrm(ks[4], (DEPTH, n_phys, PAGE_SIZE, FOX_HEADS, FOX_HD)),
        'cache_fox_v': nrm(ks[5], (DEPTH, n_phys, PAGE_SIZE, FOX_HEADS, FOX_HD)),
        'cache_fox_logf': jax.nn.log_sigmoid(FGATE_BIAS_INIT + nrm(ks[6], (DEPTH, n_phys, PAGE_SIZE, FOX_HEADS))),
        'state_pool': nrm(ks[7], (DEPTH, DEC_BATCH, POOL_STATE, POOL_WIDTH)),
        'cache_mem_k': nrm(ks[8], (DEPTH, DEC_BATCH, MEM_TOKENS, MEM_HEADS, MEM_HD)),
        'cache_mem_v': nrm(ks[9], (DEPTH, DEC_BATCH, MEM_TOKENS, MEM_HEADS, MEM_HD)),
        'page_table': page_table,
        'w_in': lin(ks[10], D_MODEL, IN_WIDTH),
        'b_fgate': FGATE_BIAS_INIT + nrm(ks[11], (DEPTH, FOX_HEADS), 0.1),
        'w_pool': nrm(ks[12], (DEPTH, POOL_GROUPS, POOL_GW, POOL_GW), POOL_GW ** -0.5),
        'pool_scale': 1.0 + nrm(ks[13], (DEPTH, POOL_WIDTH), 0.1),
        'w_br_fox': lin(ks[14], FOX_WIDTH, D_MODEL),
        'w_br_pool': lin(ks[15], POOL_WIDTH, D_MODEL),
        'w_br_mem': lin(ks[16], MEM_WIDTH, D_MODEL),
        'w_mem_kv': lin(ks[17], D_MODEL, 2 * MEM_WIDTH),
        'w_out': lin(ks[18], D_MODEL, D_MODEL, BETA),
        'ln1_g': 1.0 + nrm(ks[19], (DEPTH, D_MODEL), 0.01),
        'ln1_b': nrm(ks[20], (DEPTH, D_MODEL), 0.01),
        'w_up': lin(ks[21], D_MODEL, D_FF),
        'w_down': lin(ks[22], D_FF, D_MODEL, BETA),
        'ln2_g': 1.0 + nrm(ks[23], (DEPTH, D_MODEL), 0.01),
        'ln2_b': nrm(ks[24], (DEPTH, D_MODEL), 0.01),
    }


def reference(x_prompt, x_sample, mem_prompt, cache_fox_k, cache_fox_v, cache_fox_logf, state_pool,
              cache_mem_k, cache_mem_v, page_table, w_in, b_fgate, w_pool, pool_scale, w_br_fox,
              w_br_pool, w_br_mem, w_mem_kv, w_out, ln1_g, ln1_b, w_up, w_down, ln2_g, ln2_b):
    B, S = x_prompt.shape[:2]
    DB, T = x_sample.shape[:2]
    past = page_table.shape[1] * PAGE_SIZE
    hp, hs = x_prompt, x_sample
    p_k, p_v, p_f, p_pool, p_mk, p_mv = [], [], [], [], [], []
    s_k, s_v, s_f, s_pool = [], [], [], []
    for l in range(DEPTH):
        tail = (w_br_fox[l], w_br_pool[l], w_br_mem[l], w_out[l], ln1_g[l], ln1_b[l],
                w_up[l], w_down[l], ln2_g[l], ln2_b[l])
        q, k, v, logf, u, qm, gates = _project_in(hp, w_in[l], b_fgate[l])
        o_fox = _fox_prompt(q, k, v, logf)
        u_ext = jnp.concatenate([jnp.zeros((B, POOL_STATE, POOL_WIDTH), u.dtype), u], axis=1)
        o_pool = _pool(u_ext, jnp.arange(S), w_pool[l], pool_scale[l])
        mk, mv = _mem_kv(mem_prompt, w_mem_kv[l])
        o_mem = _mem_attend(qm, mk, mv)
        p_k.append(k); p_v.append(v); p_f.append(logf)
        p_pool.append(u_ext[:, -POOL_STATE:]); p_mk.append(mk); p_mv.append(mv)
        hp = _finish(hp, o_fox, o_pool, o_mem, gates, *tail)
        q, k, v, logf, u, qm, gates = _project_in(hs, w_in[l], b_fgate[l])
        o_fox = _fox_sample(q, k, v, logf, cache_fox_k[l], cache_fox_v[l], cache_fox_logf[l], page_table)
        u_ext = jnp.concatenate([state_pool[l].astype(u.dtype), u], axis=1)
        o_pool = _pool(u_ext, past + jnp.arange(T), w_pool[l], pool_scale[l])
        o_mem = _mem_attend(qm, cache_mem_k[l], cache_mem_v[l])
        s_k.append(k); s_v.append(v); s_f.append(logf); s_pool.append(u_ext[:, -POOL_STATE:])
        hs = _finish(hs, o_fox, o_pool, o_mem, gates, *tail)
    return (hp, hs, jnp.stack(p_k), jnp.stack(p_v), jnp.stack(p_f), jnp.stack(p_pool),
            jnp.stack(p_mk), jnp.stack(p_mv), jnp.stack(s_k), jnp.stack(s_v), jnp.stack(s_f), jnp.stack(s_pool))
```

```python
import functools

import numpy as np

import jax
import jax.numpy as jnp
from jax import lax
from jax.experimental import pallas as pl
from jax.experimental.pallas import tpu as pltpu

F32 = jnp.float32
BF16 = jnp.bfloat16

FOX_HEADS = 8
FOX_HD = 128
POOL_WINDOWS = (2, 4, 8, 16)
POOL_GW = 256
POOL_STATE = max(POOL_WINDOWS) - 1
POOL_HALO = 16
MEM_HEADS = 4
MEM_HD = 256
LN_EPS = 1e-5
MASK_VALUE = -1e30

V7X_VMEM_LIMIT_BYTES = 56 * 1024 * 1024


def _cparams(semantics):
    return pltpu.CompilerParams(dimension_semantics=semantics,
                                vmem_limit_bytes=V7X_VMEM_LIMIT_BYTES)


def _dot(a, b):
    return jnp.dot(a, b, preferred_element_type=F32)


def _dot_nt(a, b):
    return lax.dot_general(a, b, (((1,), (1,)), ((), ())), preferred_element_type=F32)


def _split3_dot(x, t):
    hi = x.astype(BF16)
    r1 = x - hi.astype(F32)
    mid = r1.astype(BF16)
    lo = (r1 - mid.astype(F32)).astype(BF16)
    return _dot(hi, t) + _dot(mid, t) + _dot(lo, t)


def _layer_norm(z, g, b):
    mu = jnp.mean(z, axis=-1, keepdims=True)
    zc = z - mu
    var = jnp.mean(zc * zc, axis=-1, keepdims=True)
    return zc * lax.rsqrt(var + LN_EPS) * g + b


def _mm_body(*refs, epilogue, n_in):
    x_ref, w_ref = refs[0], refs[1]
    extras = refs[2:n_in]
    outs = refs[n_in:]
    acc = _dot(x_ref[...], w_ref[...])
    res = epilogue(acc, *(e[...] for e in extras))
    for o, r in zip(outs, res):
        o[...] = r.astype(o.dtype)


def _mm(x, w, out_dtypes, epilogue, *, extras=(), tm=1024, tn=1024, name):
    m, k = x.shape
    n = w.shape[1]
    tm, tn = min(tm, m), min(tn, n)
    assert m % tm == 0 and n % tn == 0
    in_specs = [pl.BlockSpec((tm, k), lambda i, j: (i, 0)),
                pl.BlockSpec((k, tn), lambda i, j: (0, j))]
    in_specs += [pl.BlockSpec(bs, im) for _, bs, im in extras]
    out_specs = [pl.BlockSpec((tm, tn), lambda i, j: (i, j)) for _ in out_dtypes]
    out_shape = [jax.ShapeDtypeStruct((m, n), dt) for dt in out_dtypes]
    return pl.pallas_call(
        functools.partial(_mm_body, epilogue=epilogue, n_in=2 + len(extras)),
        grid=(m // tm, n // tn),
        in_specs=in_specs, out_specs=out_specs, out_shape=out_shape,
        compiler_params=_cparams(("parallel", "arbitrary")),
        name=name,
    )(x, w, *(a for a, _, _ in extras))


def _ep_identity(acc):
    return (acc,)


def _ep_dup(acc):
    return (acc, acc)


def _ep_sigmoid(acc):
    return (jax.nn.sigmoid(acc),)


def _ep_log_sigmoid(acc, bias):
    z = acc + bias
    return (jnp.minimum(z, 0.0) - jnp.log1p(jnp.exp(-jnp.abs(z))),)


def _cumsum_body(x_ref, tri_ref, o_ref, *, chunk):
    n = x_ref.shape[1]
    carry = jnp.zeros((x_ref.shape[0], 1), F32)
    for c in range(n // chunk):
        cs = _split3_dot(x_ref[:, c * chunk:(c + 1) * chunk], tri_ref[...]) + carry
        o_ref[:, c * chunk:(c + 1) * chunk] = cs
        carry = cs[:, chunk - 1:chunk]


def _cumsum_lanes(x, chunk=512):
    rows, n = x.shape
    tri = jnp.asarray(np.triu(np.ones((chunk, chunk), np.float32)), BF16)
    return pl.pallas_call(
        functools.partial(_cumsum_body, chunk=chunk),
        out_shape=jax.ShapeDtypeStruct((rows, n), F32),
        name="logf_cumsum",
    )(x, tri)


def _fox_prompt_body(q_ref, k_ref, v_ref, cq_ref, ck_ref, o_ref, *, tq, tk, scale):
    i = pl.program_id(1)
    q = q_ref[...]
    cq = cq_ref[0]
    row = lax.broadcasted_iota(jnp.int32, (tq, tk), 0)
    col = lax.broadcasted_iota(jnp.int32, (tq, tk), 1)

    def step(j, carry, diag):
        m, l, acc = carry
        ks = pl.multiple_of(j * tk, tk)
        k = k_ref[pl.ds(ks, tk), :]
        v = v_ref[pl.ds(ks, tk), :]
        s = _dot_nt(q, k) * scale + (cq - ck_ref[0, :, pl.ds(ks, tk)])
        if diag is not None:
            s = jnp.where(row >= col + diag * tk, s, MASK_VALUE)
        m_new = jnp.maximum(m, jnp.max(s, axis=-1, keepdims=True))
        alpha = jnp.exp(m - m_new)
        p = jnp.exp(s - m_new)
        l = alpha * l + jnp.sum(p, axis=-1, keepdims=True)
        acc = alpha * acc + _dot(p.astype(BF16), v)
        return m_new, l, acc

    per = tq // tk
    carry = (jnp.full((tq, 1), MASK_VALUE, F32), jnp.zeros((tq, 1), F32),
             jnp.zeros((tq, FOX_HD), F32))
    carry = lax.fori_loop(0, i * per, functools.partial(step, diag=None), carry)
    for d in range(per):
        carry = step(i * per + d, carry, d)
    _, l, acc = carry
    o_ref[...] = (acc / l).astype(o_ref.dtype)


def _fox_prompt(q, k, v, c_col, c_row, *, tq=512, tk=512):
    s = q.shape[0]
    return pl.pallas_call(
        functools.partial(_fox_prompt_body, tq=tq, tk=tk, scale=FOX_HD ** -0.5),
        grid=(FOX_HEADS, s // tq),
        in_specs=[pl.BlockSpec((tq, FOX_HD), lambda h, i: (i, h)),
                  pl.BlockSpec((s, FOX_HD), lambda h, i: (0, h)),
                  pl.BlockSpec((s, FOX_HD), lambda h, i: (0, h)),
                  pl.BlockSpec((1, tq, 1), lambda h, i: (h, i, 0)),
                  pl.BlockSpec((1, 1, s), lambda h, i: (h, 0, 0))],
        out_specs=pl.BlockSpec((tq, FOX_HD), lambda h, i: (i, h)),
        out_shape=jax.ShapeDtypeStruct(q.shape, BF16),
        compiler_params=_cparams(("parallel", "arbitrary")),
        name="fox_prompt_attention",
    )(q, k, v, c_col, c_row)


def _fox_sample_body(pt_ref, q_ref, kn_ref, vn_ref, lfn_ref, msk_ref, t4_ref, t5_ref, *rest,
                     n_pages, scale):
    del pt_ref
    k_pages = rest[:n_pages]
    v_pages = rest[n_pages:2 * n_pages]
    lf_pages = rest[2 * n_pages:3 * n_pages]
    o_ref, s_scr = rest[3 * n_pages], rest[3 * n_pages + 1]
    rows = q_ref.shape[1]
    pk = k_pages[0].shape[1]
    h = FOX_HEADS

    q = q_ref[0]
    r = lfn_ref[0]
    bq = _split3_dot(jnp.broadcast_to(r, (rows, rows)) * msk_ref[...], t4_ref[...])
    cn = _split3_dot(jnp.broadcast_to(r, (8, rows)), t5_ref[...])[0:1]

    row_h = lax.broadcasted_iota(jnp.int32, (rows, 128), 0) & (h - 1)
    lane_h = lax.broadcasted_iota(jnp.int32, (rows, 128), 1) & (h - 1)
    same_head = row_h == lane_h
    mbq = jnp.where(same_head, bq, MASK_VALUE)
    mbq_page = jnp.concatenate([mbq] * (pk // 128), axis=1)

    kn = kn_ref[0].astype(BF16)
    row_t = lax.broadcasted_iota(jnp.int32, (rows, rows), 0) >> 3
    lane_t = lax.broadcasted_iota(jnp.int32, (rows, rows), 1) >> 3
    ok_new = same_head[:, :rows] & (lane_t <= row_t)
    s_new = jnp.where(ok_new, _dot_nt(q, kn) * scale + (bq[:, :rows] - cn), MASK_VALUE)
    m = jnp.max(s_new, axis=-1, keepdims=True)

    later = jnp.zeros((1, 128), F32)
    for p in reversed(range(n_pages)):
        lf = lf_pages[p][0]
        tot = lf[:, pk:]
        d = jnp.concatenate([later + tot] * (pk // 128), axis=1) - lf[:, :pk]
        later = later + tot
        s = _dot_nt(q, k_pages[p][0].astype(BF16)) * scale + (mbq_page + d)
        s_scr[:, p * pk:(p + 1) * pk] = s
        m = jnp.maximum(m, jnp.max(s, axis=-1, keepdims=True))

    p_new = jnp.exp(s_new - m)
    l = jnp.sum(p_new, axis=-1, keepdims=True)
    acc = _dot(p_new.astype(BF16), vn_ref[0].astype(BF16))
    for p in range(n_pages):
        pr = jnp.exp(s_scr[:, p * pk:(p + 1) * pk] - m)
        l = l + jnp.sum(pr, axis=-1, keepdims=True)
        acc = acc + _dot(pr.astype(BF16), v_pages[p][0].astype(BF16))
    o_ref[0] = (acc / l).astype(o_ref.dtype)


def _fox_sample(q, k_new, v_new, lf_new, cache_k, cache_v, lf_cache, page_table):
    b, rows, d = q.shape
    n_pages = page_table.shape[1]
    pk = cache_k.shape[1]
    t = rows // FOX_HEADS
    assert rows == 64 and FOX_HEADS == 8 and pk % 128 == 0

    rr = np.arange(rows)
    tt, hh = rr // FOX_HEADS, rr % FOX_HEADS
    lane = np.arange(128)
    msk = jnp.asarray((tt[None, :] <= tt[:, None]).astype(np.float32))
    t4 = jnp.asarray((hh[:, None] == (lane % FOX_HEADS)[None, :]).astype(np.float32), BF16)
    t5 = jnp.asarray(((hh[:, None] == hh[None, :]) & (tt[:, None] <= tt[None, :])).astype(np.float32), BF16)
    del t

    def page_map(p):
        return lambda i, pt: (pt[i * n_pages + p], 0, 0)

    per_b = lambda i, pt: (i, 0, 0)
    const = lambda i, pt: (0, 0)
    in_specs = [pl.BlockSpec((1, rows, d), per_b),
                pl.BlockSpec((1, rows, d), per_b),
                pl.BlockSpec((1, rows, d), per_b),
                pl.BlockSpec((1, 1, rows), per_b),
                pl.BlockSpec((rows, rows), const),
                pl.BlockSpec((rows, 128), const),
                pl.BlockSpec((rows, rows), const)]
    in_specs += [pl.BlockSpec((1, pk, d), page_map(p)) for p in range(n_pages)]
    in_specs += [pl.BlockSpec((1, pk, d), page_map(p)) for p in range(n_pages)]
    in_specs += [pl.BlockSpec((1, 1, pk + 128), page_map(p)) for p in range(n_pages)]
    grid_spec = pltpu.PrefetchScalarGridSpec(
        num_scalar_prefetch=1, grid=(b,), in_specs=in_specs,
        out_specs=pl.BlockSpec((1, rows, d), per_b),
        scratch_shapes=[pltpu.VMEM((rows, n_pages * pk), F32)])
    return pl.pallas_call(
        functools.partial(_fox_sample_body, n_pages=n_pages, scale=FOX_HD ** -0.5),
        grid_spec=grid_spec,
        out_shape=jax.ShapeDtypeStruct((b, rows, d), BF16),
        compiler_params=_cparams(("arbitrary",)),
        name="fox_sample_attention",
    )(page_table.reshape(-1), q, k_new, v_new, lf_new, msk, t4, t5,
      *([cache_k] * n_pages), *([cache_v] * n_pages), *([lf_cache] * n_pages))


def _page_logf_sums(cache_logf):
    p, page, h = cache_logf.shape
    n = page * h
    pos, head = np.arange(n) // h, np.arange(n) % h
    prefix = (head[:, None] == head[None, :]) & (pos[:, None] <= pos[None, :])
    total = head[:, None] == (np.arange(128) % h)[None, :]
    t = jnp.asarray(np.concatenate([prefix, total], axis=1).astype(np.float32), BF16)
    tm = next(c for c in (512, 256, 128, 64, 32, 16, 8) if p % c == 0)

    def body(x_ref, t_ref, o_ref):
        o_ref[...] = _split3_dot(x_ref[...], t_ref[...])

    out = pl.pallas_call(
        body, grid=(p // tm,),
        in_specs=[pl.BlockSpec((tm, n), lambda i: (i, 0)),
                  pl.BlockSpec((n, n + 128), lambda i: (0, 0))],
        out_specs=pl.BlockSpec((tm, n + 128), lambda i: (i, 0)),
        out_shape=jax.ShapeDtypeStruct((p, n + 128), F32),
        compiler_params=_cparams(("parallel",)),
        name="page_logf_sums",
    )(cache_logf.reshape(p, n), t)
    return out.reshape(p, 1, n + 128)


def _pool_body(u_ref, halo_ref, pos_ref, wp_ref, sc_ref, o_ref, *, zero_first):
    u = u_ref[...]
    halo = halo_ref[...]
    if zero_first:
        halo = jnp.where(pl.program_id(0) == 0, 0.0, halo)
    pos1 = pos_ref[...] + 1.0
    outs = []
    for g, w in enumerate(POOL_WINDOWS):
        sl = slice(g * POOL_GW, (g + 1) * POOL_GW)
        s = jnp.concatenate([halo[:, sl], u[:, sl]], axis=0)
        sh = 1
        while sh < w:
            s = s + pltpu.roll(s, sh, 0)
            sh *= 2
        d = s[POOL_HALO:] / jnp.minimum(float(w), pos1) - u[:, sl]
        outs.append(_dot(d.astype(BF16), wp_ref[g]))
    o_ref[...] = (jnp.concatenate(outs, axis=1) * sc_ref[...]).astype(o_ref.dtype)


def _pool(u, pos, w_pool, pool_scale, *, tm, zero_first):
    assert all(w & (w - 1) == 0 for w in POOL_WINDOWS) and tm % POOL_HALO == 0
    r, width = u.shape
    hb = tm // POOL_HALO
    return pl.pallas_call(
        functools.partial(_pool_body, zero_first=zero_first),
        grid=(r // tm,),
        in_specs=[pl.BlockSpec((tm, width), lambda i: (i, 0)),
                  pl.BlockSpec((POOL_HALO, width), lambda i: (jnp.maximum(i * hb - 1, 0), 0)),
                  pl.BlockSpec((tm, 1), lambda i: (i, 0)),
                  pl.BlockSpec(w_pool.shape, lambda i: (0, 0, 0)),
                  pl.BlockSpec((1, width), lambda i: (0, 0))],
        out_specs=pl.BlockSpec((tm, width), lambda i: (i, 0)),
        out_shape=jax.ShapeDtypeStruct((r, width), BF16),
        compiler_params=_cparams(("arbitrary",)),
        name="pool_mixer",
    )(u, u, pos, w_pool, pool_scale)


def _mem_prompt_body(q_ref, k_ref, v_ref, o_ref, *, scale):
    for h in range(MEM_HEADS):
        sl = slice(h * MEM_HD, (h + 1) * MEM_HD)
        s = _dot_nt(q_ref[:, sl], k_ref[:, sl]) * scale
        p = jnp.exp(s - jnp.max(s, axis=-1, keepdims=True))
        l = jnp.sum(p, axis=-1, keepdims=True)
        o_ref[:, sl] = (_dot(p.astype(BF16), v_ref[:, sl]) / l).astype(o_ref.dtype)


def _mem_prompt(qm, mk, mv, *, tq=1024):
    s, width = qm.shape
    m = mk.shape[0]
    return pl.pallas_call(
        functools.partial(_mem_prompt_body, scale=MEM_HD ** -0.5),
        grid=(s // tq,),
        in_specs=[pl.BlockSpec((tq, width), lambda i: (i, 0)),
                  pl.BlockSpec((m, width), lambda i: (0, 0)),
                  pl.BlockSpec((m, width), lambda i: (0, 0))],
        out_specs=pl.BlockSpec((tq, width), lambda i: (i, 0)),
        out_shape=jax.ShapeDtypeStruct((s, width), BF16),
        compiler_params=_cparams(("parallel",)),
        name="mem_prompt_attention",
    )(qm, mk, mv)


def _mem_sample_body(q_ref, k_ref, v_ref, o_ref, *, bb, t, scale):
    width = q_ref.shape[1]
    rows = MEM_HEADS * t
    row_h = lax.broadcasted_iota(jnp.int32, (rows, width), 0) >> (t.bit_length() - 1)
    lane_h = lax.broadcasted_iota(jnp.int32, (rows, width), 1) >> (MEM_HD.bit_length() - 1)
    own = row_h == lane_h
    q_all = q_ref[...].astype(F32)
    outs = []
    for b in range(bb):
        q = q_all[b * t:(b + 1) * t, :]
        qbd = jnp.where(own, jnp.concatenate([q] * MEM_HEADS, axis=0), 0.0).astype(BF16)
        s = _dot_nt(qbd, k_ref[b].astype(BF16)) * scale
        p = jnp.exp(s - jnp.max(s, axis=-1, keepdims=True))
        l = jnp.sum(p, axis=-1, keepdims=True)
        o = jnp.where(own, _dot(p.astype(BF16), v_ref[b].astype(BF16)) / l, 0.0)
        out = o[0:t]
        for h in range(1, MEM_HEADS):
            out = out + o[h * t:(h + 1) * t]
        outs.append(out)
    o_ref[...] = jnp.concatenate(outs, axis=0).astype(o_ref.dtype)


def _mem_sample(qm, ck, cv, *, t, bb=4):
    n, width = qm.shape
    b, m, _ = ck.shape
    return pl.pallas_call(
        functools.partial(_mem_sample_body, bb=bb, t=t, scale=MEM_HD ** -0.5),
        grid=(b // bb,),
        in_specs=[pl.BlockSpec((bb * t, width), lambda i: (i, 0)),
                  pl.BlockSpec((bb, m, width), lambda i: (i, 0, 0)),
                  pl.BlockSpec((bb, m, width), lambda i: (i, 0, 0))],
        out_specs=pl.BlockSpec((bb * t, width), lambda i: (i, 0)),
        out_shape=jax.ShapeDtypeStruct((n, width), BF16),
        compiler_params=_cparams(("parallel",)),
        name="mem_sample_attention",
    )(qm, ck, cv)


def _merge_body(of_ref, op_ref, om_ref, wf_ref, wp_ref, wm_ref, g0_ref, g1_ref, g2_ref, o_ref):
    merged = (g0_ref[...] * _dot(of_ref[...], wf_ref[...])
              + g1_ref[...] * _dot(op_ref[...], wp_ref[...])
              + g2_ref[...] * _dot(om_ref[...], wm_ref[...]))
    o_ref[...] = merged.astype(o_ref.dtype)


def _merge(o_fox, o_pool, o_mem, w_f, w_p, w_m, gates, *, tm=512, tn=1024):
    m, kb = o_fox.shape
    n = w_f.shape[1]
    tm = min(tm, m)
    nb = n // tn
    act = pl.BlockSpec((tm, kb), lambda i, j: (i, 0))
    wgt = pl.BlockSpec((kb, tn), lambda i, j: (0, j))
    gate = lambda g: pl.BlockSpec((tm, tn), lambda i, j: (i, j + g * nb))
    return pl.pallas_call(
        _merge_body, grid=(m // tm, nb),
        in_specs=[act, act, act, wgt, wgt, wgt, gate(0), gate(1), gate(2)],
        out_specs=pl.BlockSpec((tm, tn), lambda i, j: (i, j)),
        out_shape=jax.ShapeDtypeStruct((m, n), BF16),
        compiler_params=_cparams(("parallel", "arbitrary")),
        name="branch_merge",
    )(o_fox, o_pool, o_mem, w_f, w_p, w_m, gates, gates, gates)


def _ep_residual_ln(acc, x, g, b, *, alpha):
    h = _layer_norm(alpha * x + acc, g, b)
    return (h, h)


def _ffn_body(h_ref, wu_ref, wd_ref, res_ref, g_ref, b_ref, o_ref, acc_ref, *, alpha):
    k = pl.program_id(1)

    @pl.when(k == 0)
    def _():
        acc_ref[...] = jnp.zeros_like(acc_ref)

    a = jnp.square(jnp.maximum(_dot(h_ref[...], wu_ref[...]), 0.0))
    acc_ref[...] += _dot(a.astype(BF16), wd_ref[...])

    @pl.when(k == pl.num_programs(1) - 1)
    def _():
        o_ref[...] = _layer_norm(alpha * res_ref[...] + acc_ref[...], g_ref[...], b_ref[...])


def _ffn(h_bf, h_f32, w_up, w_down, g, b, *, alpha, tm=512, tf=512):
    m, d = h_bf.shape
    dff = w_up.shape[1]
    tm = min(tm, m)
    return pl.pallas_call(
        functools.partial(_ffn_body, alpha=alpha),
        grid=(m // tm, dff // tf),
        in_specs=[pl.BlockSpec((tm, d), lambda i, k: (i, 0)),
                  pl.BlockSpec((d, tf), lambda i, k: (0, k)),
                  pl.BlockSpec((tf, d), lambda i, k: (k, 0)),
                  pl.BlockSpec((tm, d), lambda i, k: (i, 0)),
                  pl.BlockSpec((1, d), lambda i, k: (0, 0)),
                  pl.BlockSpec((1, d), lambda i, k: (0, 0))],
        out_specs=pl.BlockSpec((tm, d), lambda i, k: (i, 0)),
        out_shape=jax.ShapeDtypeStruct((m, d), F32),
        scratch_shapes=[pltpu.VMEM((tm, d), F32)],
        compiler_params=_cparams(("parallel", "arbitrary")),
        name="ffn_ln",
    )(h_bf, w_up, w_down, h_f32, g, b)


def _project_in(x_bf, w, b_f):
    q, = _mm(x_bf, w["q"], [BF16], _ep_identity, name="proj_q")
    k, k_bf = _mm(x_bf, w["k"], [F32, BF16], _ep_dup, name="proj_k")
    v, v_bf = _mm(x_bf, w["v"], [F32, BF16], _ep_dup, name="proj_v")
    u, = _mm(x_bf, w["u"], [F32], _ep_identity, name="proj_u")
    qm, = _mm(x_bf, w["qm"], [BF16], _ep_identity, name="proj_qm")
    gates, = _mm(x_bf, w["gz"], [F32], _ep_sigmoid, name="proj_gates")
    logf, = _mm(x_bf, w["f"], [F32], _ep_log_sigmoid,
                extras=[(b_f, (1, 128), lambda i, j: (0, 0))], tn=128, name="proj_logf")
    return q, k, k_bf, v, v_bf, u, qm, gates, logf[:, :FOX_HEADS]


def _finish(x, o_fox, o_pool, o_mem, gates, w, *, alpha):
    d = x.shape[1]
    merged = _merge(o_fox, o_pool, o_mem, w["br_fox"], w["br_pool"], w["br_mem"], gates)
    row = lambda i, j: (0, 0)
    h, h_bf = _mm(merged, w["out"], [F32, BF16], functools.partial(_ep_residual_ln, alpha=alpha),
                  extras=[(x, (min(512, x.shape[0]), d), lambda i, j: (i, 0)),
                          (w["ln1_g"], (1, d), row), (w["ln1_b"], (1, d), row)],
                  tm=512, tn=d, name="out_proj_ln")
    return _ffn(h_bf, h, w["up"], w["down"], w["ln2_g"], w["ln2_b"], alpha=alpha)


def kernel(x_prompt, x_sample, mem_prompt, cache_fox_k, cache_fox_v, cache_fox_logf, state_pool,
           cache_mem_k, cache_mem_v, page_table, w_in, b_fgate, w_pool, pool_scale, w_br_fox,
           w_br_pool, w_br_mem, w_mem_kv, w_out, ln1_g, ln1_b, w_up, w_down, ln2_g, ln2_b):
    batch, seq, d_model = x_prompt.shape
    dec_batch, dec_seq, _ = x_sample.shape
    depth = w_in.shape[0]
    n_phys, page_size = cache_fox_k.shape[1], cache_fox_k.shape[2]
    n_pages = page_table.shape[1]
    past = n_pages * page_size
    fox_w = FOX_HEADS * FOX_HD
    pool_w = len(POOL_WINDOWS) * POOL_GW
    mem_w = MEM_HEADS * MEM_HD
    mem_tokens = mem_prompt.shape[1]
    assert batch == 1 and dec_seq == FOX_HEADS == 8
    alpha = (2 * depth) ** 0.25

    hp = x_prompt.reshape(seq, d_model)
    hs = x_sample.reshape(dec_batch * dec_seq, d_model)
    mem_bf = mem_prompt.reshape(mem_tokens, d_model).astype(BF16)
    seg = POOL_HALO + dec_seq
    pos_p = jnp.arange(seq, dtype=F32)[:, None]
    pos_s = jnp.tile(jnp.concatenate([jnp.full((POOL_HALO,), 1e9, F32),
                                      past + jnp.arange(dec_seq, dtype=F32)]), dec_batch)[:, None]

    outs = {n: [] for n in ("pk", "pv", "pf", "pp", "pmk", "pmv", "sk", "sv", "sf", "sp")}
    for l in range(depth):
        o = 0
        w = {}
        for name, width in (("q", fox_w), ("k", fox_w), ("v", fox_w), ("f", FOX_HEADS),
                            ("u", pool_w), ("qm", mem_w), ("gz", 3 * d_model)):
            w[name] = w_in[l][:, o:o + width]
            o += width
        w["f"] = jnp.pad(w["f"], ((0, 0), (0, 128 - FOX_HEADS)))
        w = {n: a.astype(BF16) for n, a in w.items()}
        b_f = jnp.pad(b_fgate[l], (0, 128 - FOX_HEADS))[None, :]
        w.update(br_fox=w_br_fox[l].astype(BF16), br_pool=w_br_pool[l].astype(BF16),
                 br_mem=w_br_mem[l].astype(BF16), out=w_out[l].astype(BF16),
                 up=w_up[l].astype(BF16), down=w_down[l].astype(BF16),
                 ln1_g=ln1_g[l][None, :], ln1_b=ln1_b[l][None, :],
                 ln2_g=ln2_g[l][None, :], ln2_b=ln2_b[l][None, :])
        wp_bf = w_pool[l].astype(BF16)
        psc = pool_scale[l][None, :]

        q, k, k_bf, v, v_bf, u, qm, gates, logf = _project_in(hp.astype(BF16), w, b_f)
        c_row = _cumsum_lanes(logf.T)
        o_fox = _fox_prompt(q, k_bf, v_bf, c_row[:, :, None], c_row[:, None, :])
        o_pool = _pool(u, pos_p, wp_bf, psc, tm=1024, zero_first=True)
        mkv, mkv_bf = _mm(mem_bf, w_mem_kv[l].astype(BF16), [F32, BF16], _ep_dup, name="mem_kv")
        o_mem = _mem_prompt(qm, mkv_bf[:, :mem_w], mkv_bf[:, mem_w:])
        outs["pk"].append(k.reshape(batch, seq, FOX_HEADS, FOX_HD))
        outs["pv"].append(v.reshape(batch, seq, FOX_HEADS, FOX_HD))
        outs["pf"].append(logf.reshape(batch, seq, FOX_HEADS))
        outs["pp"].append(u[-POOL_STATE:].reshape(batch, POOL_STATE, pool_w))
        outs["pmk"].append(mkv[:, :mem_w].reshape(batch, mem_tokens, MEM_HEADS, MEM_HD))
        outs["pmv"].append(mkv[:, mem_w:].reshape(batch, mem_tokens, MEM_HEADS, MEM_HD))
        hp = _finish(hp, o_fox, o_pool, o_mem, gates, w, alpha=alpha)

        q, k, _, v, _, u, qm, gates, logf = _project_in(hs.astype(BF16), w, b_f)
        rows = dec_seq * FOX_HEADS
        o_fox = _fox_sample(
            q.reshape(dec_batch, rows, FOX_HD), k.reshape(dec_batch, rows, FOX_HD),
            v.reshape(dec_batch, rows, FOX_HD), logf.reshape(dec_batch, 1, rows),
            cache_fox_k[l].reshape(n_phys, page_size * FOX_HEADS, FOX_HD),
            cache_fox_v[l].reshape(n_phys, page_size * FOX_HEADS, FOX_HD),
            _page_logf_sums(cache_fox_logf[l]), page_table).reshape(dec_batch * dec_seq, fox_w)
        u3 = u.reshape(dec_batch, dec_seq, pool_w)
        state = state_pool[l].astype(F32)
        u_ext = jnp.concatenate([jnp.zeros((dec_batch, POOL_HALO - POOL_STATE, pool_w), F32), state, u3],
                                axis=1).reshape(dec_batch * seg, pool_w)
        o_pool = _pool(u_ext, pos_s, wp_bf, psc, tm=32 * seg, zero_first=False)
        o_pool = o_pool.reshape(dec_batch, seg, pool_w)[:, POOL_HALO:].reshape(dec_batch * dec_seq, pool_w)
        o_mem = _mem_sample(qm, cache_mem_k[l].reshape(dec_batch, mem_tokens, mem_w),
                            cache_mem_v[l].reshape(dec_batch, mem_tokens, mem_w), t=dec_seq)
        outs["sk"].append(k.reshape(dec_batch, dec_seq, FOX_HEADS, FOX_HD))
        outs["sv"].append(v.reshape(dec_batch, dec_seq, FOX_HEADS, FOX_HD))
        outs["sf"].append(logf.reshape(dec_batch, dec_seq, FOX_HEADS))
        outs["sp"].append(jnp.concatenate([state, u3], axis=1)[:, -POOL_STATE:])
        hs = _finish(hs, o_fox, o_pool, o_mem, gates, w, alpha=alpha)

    st = lambda n: jnp.stack(outs[n])
    return (hp.reshape(batch, seq, d_model), hs.reshape(dec_batch, dec_seq, d_model),
            st("pk"), st("pv"), st("pf"), st("pp"), st("pmk"), st("pmv"),
            st("sk"), st("sv"), st("sf"), st("sp"))
```

```python
import functools

import numpy as np

import jax
import jax.numpy as jnp
from jax import lax
from jax.experimental import pallas as pl
from jax.experimental.pallas import tpu as pltpu

F32 = jnp.float32
BF16 = jnp.bfloat16

FOX_HEADS = 8
FOX_HD = 128
POOL_WINDOWS = (2, 4, 8, 16)
POOL_GW = 256
POOL_STATE = max(POOL_WINDOWS) - 1
POOL_HALO = 16
MEM_HEADS = 4
MEM_HD = 256
LN_EPS = 1e-5
MASK_VALUE = -1e30
LOG2E = 1.4426950408889634
FOX_KAUG = 256
FOX_VAUG = 144

V7X_VMEM_LIMIT_BYTES = 56 * 1024 * 1024


def _cparams(semantics):
    return pltpu.CompilerParams(dimension_semantics=semantics,
                                vmem_limit_bytes=V7X_VMEM_LIMIT_BYTES)


def _dot(a, b):
    return jnp.dot(a, b, preferred_element_type=F32)


def _dot_nt(a, b):
    return lax.dot_general(a, b, (((1,), (1,)), ((), ())), preferred_element_type=F32)


def _dot_tn(a, b):
    return lax.dot_general(a, b, (((0,), (0,)), ((), ())), preferred_element_type=F32)


def _split3(x):
    hi = x.astype(BF16)
    r1 = x - hi.astype(F32)
    mid = r1.astype(BF16)
    lo = (r1 - mid.astype(F32)).astype(BF16)
    return hi, mid, lo


def _split3_dot(x, t):
    hi, mid, lo = _split3(x)
    return _dot(hi, t) + _dot(mid, t) + _dot(lo, t)


def _layer_norm(z, g, b):
    mu = jnp.mean(z, axis=-1, keepdims=True)
    zc = z - mu
    var = jnp.mean(zc * zc, axis=-1, keepdims=True)
    return zc * lax.rsqrt(var + LN_EPS) * g + b


def _mm_body(*refs, epilogue, n_in):
    x_ref, w_ref = refs[0], refs[1]
    extras = refs[2:n_in]
    outs = refs[n_in:]
    acc = _dot(x_ref[...], w_ref[...])
    res = epilogue(acc, *(e[...] for e in extras))
    for o, r in zip(outs, res):
        o[...] = r.astype(o.dtype)


def _mm(x, w, outs, epilogue, *, extras=(), tm=1024, tn=1024, col0=0, ncols=None, name):
    m, k = x.shape
    n = w.shape[1] if ncols is None else ncols
    tm, tn = min(tm, m), min(tn, n)
    assert m % tm == 0 and n % tn == 0
    in_specs = [pl.BlockSpec((tm, k), lambda i, j: (i, 0)),
                pl.BlockSpec((k, tn), lambda i, j: (0, j + col0))]
    in_specs += [pl.BlockSpec(bs, im) for _, bs, im in extras]
    out_specs = [pl.BlockSpec((tn, tm), lambda i, j: (j, i)) if t else
                 pl.BlockSpec((tm, tn), lambda i, j: (i, j)) for _, t in outs]
    out_shape = [jax.ShapeDtypeStruct((n, m) if t else (m, n), dt) for dt, t in outs]
    return pl.pallas_call(
        functools.partial(_mm_body, epilogue=epilogue, n_in=2 + len(extras)),
        grid=(m // tm, n // tn),
        in_specs=in_specs, out_specs=out_specs, out_shape=out_shape,
        compiler_params=_cparams(("parallel", "arbitrary")),
        name=name,
    )(x, w, *(a for a, _, _ in extras))


def _ep_identity(acc):
    return (acc,)


def _ep_dup(acc):
    return (acc, acc)


def _ep_scaled_t(acc, *, scale):
    return ((acc * scale).T,)


def _ep_with_t(acc):
    return (acc, acc.T)


def _ep_log_sigmoid(acc, bias):
    z = acc + bias
    return (jnp.minimum(z, 0.0) - jnp.log1p(jnp.exp(-jnp.abs(z))),)


def _cumsum_body(x_ref, tri_ref, o_ref, *, chunk, scale):
    n = x_ref.shape[1]
    carry = jnp.zeros((x_ref.shape[0], 1), F32)
    for c in range(n // chunk):
        cs = _split3_dot(x_ref[:, c * chunk:(c + 1) * chunk], tri_ref[...]) + carry
        o_ref[:, c * chunk:(c + 1) * chunk] = cs * scale
        carry = cs[:, chunk - 1:chunk]


def _cumsum_lanes(x, *, scale, chunk=512):
    rows, n = x.shape
    tri = jnp.asarray(np.triu(np.ones((chunk, chunk), np.float32)), BF16)
    return pl.pallas_call(
        functools.partial(_cumsum_body, chunk=chunk, scale=scale),
        out_shape=jax.ShapeDtypeStruct((rows, n), F32),
        name="logf_cumsum",
    )(x, tri)


def _fox_prep_body(k_ref, qT_ref, vT_ref, ck_ref, cq_ref, ka_ref, qa_ref, va_ref):
    ts = k_ref.shape[0]
    pad = FOX_KAUG - FOX_HD
    lane = lax.broadcasted_iota(jnp.int32, (ts, pad), 1)
    ext = jnp.where(lane < 3, 1.0, 0.0).astype(BF16)
    for n, piece in enumerate(_split3(-ck_ref[0])):
        ext = jnp.where(lane == 3 + n, piece, ext)
    ka_ref[0] = jnp.concatenate([k_ref[...], ext], axis=1)
    sub = lax.broadcasted_iota(jnp.int32, (pad, ts), 0)
    ext = jnp.where((sub >= 3) & (sub < 6), 1.0, 0.0).astype(BF16)
    for n, piece in enumerate(_split3(cq_ref[0])):
        ext = jnp.where(sub == n, piece, ext)
    qa_ref[0] = jnp.concatenate([qT_ref[...], ext], axis=0)
    va_ref[0] = jnp.concatenate([vT_ref[...], jnp.ones((FOX_VAUG - FOX_HD, ts), BF16)], axis=0)


def _fox_prep(k, qT, vT, ck_col, cq_row, *, ts=512):
    s = k.shape[0]
    h = FOX_HEADS
    return pl.pallas_call(
        _fox_prep_body, grid=(h, s // ts),
        in_specs=[pl.BlockSpec((ts, FOX_HD), lambda g, i: (i, g)),
                  pl.BlockSpec((FOX_HD, ts), lambda g, i: (g, i)),
                  pl.BlockSpec((FOX_HD, ts), lambda g, i: (g, i)),
                  pl.BlockSpec((1, ts, 1), lambda g, i: (g, i, 0)),
                  pl.BlockSpec((1, 1, ts), lambda g, i: (g, 0, i))],
        out_specs=[pl.BlockSpec((1, ts, FOX_KAUG), lambda g, i: (g, i, 0)),
                   pl.BlockSpec((1, FOX_KAUG, ts), lambda g, i: (g, 0, i)),
                   pl.BlockSpec((1, FOX_VAUG, ts), lambda g, i: (g, 0, i))],
        out_shape=[jax.ShapeDtypeStruct((h, s, FOX_KAUG), BF16),
                   jax.ShapeDtypeStruct((h, FOX_KAUG, s), BF16),
                   jax.ShapeDtypeStruct((h, FOX_VAUG, s), BF16)],
        compiler_params=_cparams(("parallel", "arbitrary")),
        name="fox_prep",
    )(k, qT, vT, ck_col, cq_row)


def _fox_prompt_body(qa_ref, ka_ref, va_ref, oT_ref, s0, s1, m_scr, acc_scr, *, tq):
    i = pl.program_id(1)
    tk = tq

    def scores(j, dst):
        ks = pl.multiple_of(j * tk, tk)
        dst[...] = _dot(ka_ref[0, pl.ds(ks, tk), :], qa_ref[0])

    def absorb(j, src, diag):
        ks = pl.multiple_of(j * tk, tk)
        sT = src[...]
        if diag:
            key = lax.broadcasted_iota(jnp.int32, (tk, tq), 0)
            qry = lax.broadcasted_iota(jnp.int32, (tk, tq), 1)
            sT = jnp.where(key <= qry, sT, MASK_VALUE)
        m = m_scr[...]
        m_new = jnp.maximum(m, jnp.max(sT, axis=0, keepdims=True))
        p = jnp.exp2(sT - m_new).astype(BF16)
        m_scr[...] = m_new
        acc_scr[...] = jnp.exp2(m - m_new) * acc_scr[...] + _dot(va_ref[0, :, pl.ds(ks, tk)], p)

    def finish(src):
        absorb(i, src, True)
        oT_ref[...] = (acc_scr[:FOX_HD, :] / acc_scr[FOX_HD:FOX_HD + 1, :]).astype(oT_ref.dtype)

    m_scr[...] = jnp.full(m_scr.shape, MASK_VALUE, F32)
    acc_scr[...] = jnp.zeros(acc_scr.shape, F32)
    scores(0, s0)

    def pair(t, carry):
        j = 2 * t
        scores(j + 1, s1)
        absorb(j, s0, False)
        scores(j + 2, s0)
        absorb(j + 1, s1, False)
        return carry

    lax.fori_loop(0, i // 2, pair, 0)

    @pl.when(i % 2 == 0)
    def _():
        finish(s0)

    @pl.when(i % 2 == 1)
    def _():
        scores(i, s1)
        absorb(i - 1, s0, False)
        finish(s1)


def _fox_prompt(q2T, k, vT, c2_rows, *, tq=1024):
    h, s = c2_rows.shape
    cq_last = jnp.repeat(c2_rows.reshape(h, s // tq, tq)[:, :, -1], tq, axis=1)
    ka, qa, va = _fox_prep(k, q2T, vT, c2_rows[:, :, None], cq_last[:, None, :])
    return pl.pallas_call(
        functools.partial(_fox_prompt_body, tq=tq),
        grid=(h, s // tq),
        in_specs=[pl.BlockSpec((1, FOX_KAUG, tq), lambda g, i: (g, 0, i)),
                  pl.BlockSpec((1, s, FOX_KAUG), lambda g, i: (g, 0, 0)),
                  pl.BlockSpec((1, FOX_VAUG, s), lambda g, i: (g, 0, 0))],
        out_specs=pl.BlockSpec((FOX_HD, tq), lambda g, i: (g, i)),
        out_shape=jax.ShapeDtypeStruct((h * FOX_HD, s), BF16),
        scratch_shapes=[pltpu.VMEM((tq, tq), F32), pltpu.VMEM((tq, tq), F32),
                        pltpu.VMEM((1, tq), F32), pltpu.VMEM((FOX_VAUG, tq), F32)],
        compiler_params=_cparams(("parallel", "arbitrary")),
        name="fox_prompt_attention",
    )(qa, ka, va)


def _fox_sample_body(pt_ref, q_ref, kn_ref, vn_ref, lfn_ref, msk_ref, t4_ref, t5_ref, *rest,
                     n_pages, scale):
    del pt_ref
    k_pages = rest[:n_pages]
    v_pages = rest[n_pages:2 * n_pages]
    lf_pages = rest[2 * n_pages:3 * n_pages]
    o_ref, s_scr = rest[3 * n_pages], rest[3 * n_pages + 1]
    rows = q_ref.shape[1]
    pk = k_pages[0].shape[1]
    h = FOX_HEADS

    q = q_ref[0]
    r = lfn_ref[0]
    bq = _split3_dot(jnp.broadcast_to(r, (rows, rows)) * msk_ref[...], t4_ref[...])
    cn = _split3_dot(jnp.broadcast_to(r, (8, rows)), t5_ref[...])[0:1]

    row_h = lax.broadcasted_iota(jnp.int32, (rows, 128), 0) & (h - 1)
    lane_h = lax.broadcasted_iota(jnp.int32, (rows, 128), 1) & (h - 1)
    same_head = row_h == lane_h
    mbq = jnp.where(same_head, bq, MASK_VALUE)
    mbq_page = jnp.concatenate([mbq] * (pk // 128), axis=1)

    kn = kn_ref[0].astype(BF16)
    row_t = lax.broadcasted_iota(jnp.int32, (rows, rows), 0) >> 3
    lane_t = lax.broadcasted_iota(jnp.int32, (rows, rows), 1) >> 3
    ok_new = same_head[:, :rows] & (lane_t <= row_t)
    s_new = jnp.where(ok_new, _dot_nt(q, kn) * scale + (bq[:, :rows] - cn), MASK_VALUE)
    m = jnp.max(s_new, axis=-1, keepdims=True)

    later = jnp.zeros((1, 128), F32)
    for p in reversed(range(n_pages)):
        lf = lf_pages[p][0]
        tot = lf[:, pk:]
        d = jnp.concatenate([later + tot] * (pk // 128), axis=1) - lf[:, :pk]
        later = later + tot
        s = _dot_nt(q, k_pages[p][0].astype(BF16)) * scale + (mbq_page + d)
        s_scr[:, p * pk:(p + 1) * pk] = s
        m = jnp.maximum(m, jnp.max(s, axis=-1, keepdims=True))

    p_new = jnp.exp(s_new - m)
    l = jnp.sum(p_new, axis=-1, keepdims=True)
    acc = _dot(p_new.astype(BF16), vn_ref[0].astype(BF16))
    for p in range(n_pages):
        pr = jnp.exp(s_scr[:, p * pk:(p + 1) * pk] - m)
        l = l + jnp.sum(pr, axis=-1, keepdims=True)
        acc = acc + _dot(pr.astype(BF16), v_pages[p][0].astype(BF16))
    o_ref[0] = (acc / l).astype(o_ref.dtype)


def _fox_sample(q, k_new, v_new, lf_new, cache_k, cache_v, lf_cache, page_table):
    b, rows, d = q.shape
    n_pages = page_table.shape[1]
    pk = cache_k.shape[1]
    assert rows == 64 and FOX_HEADS == 8 and pk % 128 == 0

    rr = np.arange(rows)
    tt, hh = rr // FOX_HEADS, rr % FOX_HEADS
    lane = np.arange(128)
    msk = jnp.asarray((tt[None, :] <= tt[:, None]).astype(np.float32))
    t4 = jnp.asarray((hh[:, None] == (lane % FOX_HEADS)[None, :]).astype(np.float32), BF16)
    t5 = jnp.asarray(((hh[:, None] == hh[None, :]) & (tt[:, None] <= tt[None, :])).astype(np.float32), BF16)

    def page_map(p):
        return lambda i, pt: (pt[i * n_pages + p], 0, 0)

    per_b = lambda i, pt: (i, 0, 0)
    const = lambda i, pt: (0, 0)
    in_specs = [pl.BlockSpec((1, rows, d), per_b),
                pl.BlockSpec((1, rows, d), per_b),
                pl.BlockSpec((1, rows, d), per_b),
                pl.BlockSpec((1, 1, rows), per_b),
                pl.BlockSpec((rows, rows), const),
                pl.BlockSpec((rows, 128), const),
                pl.BlockSpec((rows, rows), const)]
    in_specs += [pl.BlockSpec((1, pk, d), page_map(p)) for p in range(n_pages)]
    in_specs += [pl.BlockSpec((1, pk, d), page_map(p)) for p in range(n_pages)]
    in_specs += [pl.BlockSpec((1, 1, pk + 128), page_map(p)) for p in range(n_pages)]
    grid_spec = pltpu.PrefetchScalarGridSpec(
        num_scalar_prefetch=1, grid=(b,), in_specs=in_specs,
        out_specs=pl.BlockSpec((1, rows, d), per_b),
        scratch_shapes=[pltpu.VMEM((rows, n_pages * pk), F32)])
    return pl.pallas_call(
        functools.partial(_fox_sample_body, n_pages=n_pages, scale=FOX_HD ** -0.5),
        grid_spec=grid_spec,
        out_shape=jax.ShapeDtypeStruct((b, rows, d), BF16),
        compiler_params=_cparams(("arbitrary",)),
        name="fox_sample_attention",
    )(page_table.reshape(-1), q, k_new, v_new, lf_new, msk, t4, t5,
      *([cache_k] * n_pages), *([cache_v] * n_pages), *([lf_cache] * n_pages))


def _page_logf_sums(cache_logf):
    p, page, h = cache_logf.shape
    n = page * h
    pos, head = np.arange(n) // h, np.arange(n) % h
    prefix = (head[:, None] == head[None, :]) & (pos[:, None] <= pos[None, :])
    total = head[:, None] == (np.arange(128) % h)[None, :]
    t = jnp.asarray(np.concatenate([prefix, total], axis=1).astype(np.float32), BF16)
    tm = next(c for c in (512, 256, 128, 64, 32, 16, 8) if p % c == 0)

    def body(x_ref, t_ref, o_ref):
        o_ref[...] = _split3_dot(x_ref[...], t_ref[...])

    out = pl.pallas_call(
        body, grid=(p // tm,),
        in_specs=[pl.BlockSpec((tm, n), lambda i: (i, 0)),
                  pl.BlockSpec((n, n + 128), lambda i: (0, 0))],
        out_specs=pl.BlockSpec((tm, n + 128), lambda i: (i, 0)),
        out_shape=jax.ShapeDtypeStruct((p, n + 128), F32),
        compiler_params=_cparams(("parallel",)),
        name="page_logf_sums",
    )(cache_logf.reshape(p, n), t)
    return out.reshape(p, 1, n + 128)


def _pool_body(u_ref, halo_ref, pos_ref, wp_ref, sc_ref, o_ref, *, zero_first):
    u = u_ref[...]
    halo = halo_ref[...]
    if zero_first:
        halo = jnp.where(pl.program_id(0) == 0, 0.0, halo)
    pos1 = pos_ref[...] + 1.0
    outs = []
    for g, w in enumerate(POOL_WINDOWS):
        sl = slice(g * POOL_GW, (g + 1) * POOL_GW)
        s = jnp.concatenate([halo[:, sl], u[:, sl]], axis=0)
        sh = 1
        while sh < w:
            s = s + pltpu.roll(s, sh, 0)
            sh *= 2
        d = s[POOL_HALO:] / jnp.minimum(float(w), pos1) - u[:, sl]
        outs.append(_dot(d.astype(BF16), wp_ref[g]))
    o_ref[...] = (jnp.concatenate(outs, axis=1) * sc_ref[...]).astype(o_ref.dtype)


def _pool(u, pos, w_pool, pool_scale, *, tm, zero_first):
    assert all(w & (w - 1) == 0 for w in POOL_WINDOWS) and tm % POOL_HALO == 0
    r, width = u.shape
    hb = tm // POOL_HALO
    return pl.pallas_call(
        functools.partial(_pool_body, zero_first=zero_first),
        grid=(r // tm,),
        in_specs=[pl.BlockSpec((tm, width), lambda i: (i, 0)),
                  pl.BlockSpec((POOL_HALO, width), lambda i: (jnp.maximum(i * hb - 1, 0), 0)),
                  pl.BlockSpec((tm, 1), lambda i: (i, 0)),
                  pl.BlockSpec(w_pool.shape, lambda i: (0, 0, 0)),
                  pl.BlockSpec((1, width), lambda i: (0, 0))],
        out_specs=pl.BlockSpec((tm, width), lambda i: (i, 0)),
        out_shape=jax.ShapeDtypeStruct((r, width), BF16),
        compiler_params=_cparams(("arbitrary",)),
        name="pool_mixer",
    )(u, u, pos, w_pool, pool_scale)


def _mem_prompt_body(q_ref, k_ref, v_ref, o_ref, *, scale):
    for h in range(MEM_HEADS):
        sl = slice(h * MEM_HD, (h + 1) * MEM_HD)
        s = _dot_nt(q_ref[:, sl], k_ref[:, sl]) * scale
        p = jnp.exp(s - jnp.max(s, axis=-1, keepdims=True))
        l = jnp.sum(p, axis=-1, keepdims=True)
        o_ref[:, sl] = (_dot(p.astype(BF16), v_ref[:, sl]) / l).astype(o_ref.dtype)


def _mem_prompt(qm, mk, mv, *, tq=1024):
    s, width = qm.shape
    m = mk.shape[0]
    return pl.pallas_call(
        functools.partial(_mem_prompt_body, scale=MEM_HD ** -0.5),
        grid=(s // tq,),
        in_specs=[pl.BlockSpec((tq, width), lambda i: (i, 0)),
                  pl.BlockSpec((m, width), lambda i: (0, 0)),
                  pl.BlockSpec((m, width), lambda i: (0, 0))],
        out_specs=pl.BlockSpec((tq, width), lambda i: (i, 0)),
        out_shape=jax.ShapeDtypeStruct((s, width), BF16),
        compiler_params=_cparams(("parallel",)),
        name="mem_prompt_attention",
    )(qm, mk, mv)


def _mem_sample_body(q_ref, k_ref, v_ref, o_ref, *, bb, scale):
    rows = q_ref.shape[1]
    m, h, d = k_ref.shape[1:]
    row_h = lax.broadcasted_iota(jnp.int32, (rows, m * h), 0) & (h - 1)
    lane_h = lax.broadcasted_iota(jnp.int32, (rows, m * h), 1) & (h - 1)
    own = row_h == lane_h
    for b in range(bb):
        k2 = k_ref[b].reshape(m * h, d).astype(BF16)
        v2 = v_ref[b].reshape(m * h, d).astype(BF16)
        s = jnp.where(own, _dot_nt(q_ref[b], k2) * scale, MASK_VALUE)
        p = jnp.exp(s - jnp.max(s, axis=-1, keepdims=True))
        l = jnp.sum(p, axis=-1, keepdims=True)
        o_ref[b] = (_dot(p.astype(BF16), v2) / l).astype(o_ref.dtype)


def _mem_sample(qm, ck, cv, *, bb=4):
    b, rows, d = qm.shape
    _, m, h, _ = ck.shape
    assert h & (h - 1) == 0 and b % bb == 0
    return pl.pallas_call(
        functools.partial(_mem_sample_body, bb=bb, scale=MEM_HD ** -0.5),
        grid=(b // bb,),
        in_specs=[pl.BlockSpec((bb, rows, d), lambda i: (i, 0, 0)),
                  pl.BlockSpec((bb, m, h, d), lambda i: (i, 0, 0, 0)),
                  pl.BlockSpec((bb, m, h, d), lambda i: (i, 0, 0, 0))],
        out_specs=pl.BlockSpec((bb, rows, d), lambda i: (i, 0, 0)),
        out_shape=jax.ShapeDtypeStruct((b, rows, d), BF16),
        compiler_params=_cparams(("parallel",)),
        name="mem_sample_attention",
    )(qm, ck, cv)


def _merge_body(x_ref, of_ref, op_ref, om_ref, wf_ref, wp_ref, wm_ref, g0_ref, g1_ref, g2_ref,
                o_ref, *, fox_transposed):
    x = x_ref[...]
    fox = (_dot_tn if fox_transposed else _dot)(of_ref[...], wf_ref[...])
    merged = (jax.nn.sigmoid(_dot(x, g0_ref[...])) * fox
              + jax.nn.sigmoid(_dot(x, g1_ref[...])) * _dot(op_ref[...], wp_ref[...])
              + jax.nn.sigmoid(_dot(x, g2_ref[...])) * _dot(om_ref[...], wm_ref[...]))
    o_ref[...] = merged.astype(o_ref.dtype)


def _merge(x, o_fox, o_pool, o_mem, w_f, w_p, w_m, w_gates, gate_col0, *, fox_transposed,
           tm=512, tn=512):
    m, d = x.shape
    kb = o_pool.shape[1]
    n = w_f.shape[1]
    tm = min(tm, m)
    nb = n // tn
    assert gate_col0 % tn == 0
    act = pl.BlockSpec((tm, kb), lambda i, j: (i, 0))
    act_fox = pl.BlockSpec((kb, tm), lambda i, j: (0, i)) if fox_transposed else act
    wgt = pl.BlockSpec((kb, tn), lambda i, j: (0, j))
    gate = lambda g: pl.BlockSpec((d, tn), lambda i, j: (0, gate_col0 // tn + g * nb + j))
    return pl.pallas_call(
        functools.partial(_merge_body, fox_transposed=fox_transposed), grid=(m // tm, nb),
        in_specs=[pl.BlockSpec((tm, d), lambda i, j: (i, 0)), act_fox, act, act, wgt, wgt, wgt,
                  gate(0), gate(1), gate(2)],
        out_specs=pl.BlockSpec((tm, tn), lambda i, j: (i, j)),
        out_shape=jax.ShapeDtypeStruct((m, n), BF16),
        compiler_params=_cparams(("parallel", "arbitrary")),
        name="branch_merge",
    )(x, o_fox, o_pool, o_mem, w_f, w_p, w_m, w_gates, w_gates, w_gates)


def _ep_residual_ln(acc, x, g, b, *, alpha):
    h = _layer_norm(alpha * x + acc, g, b)
    return (h, h)


def _ffn_body(h_ref, wu_ref, wd_ref, res_ref, g_ref, b_ref, o_ref, acc_ref, *, alpha):
    k = pl.program_id(1)

    @pl.when(k == 0)
    def _():
        acc_ref[...] = jnp.zeros_like(acc_ref)

    a = jnp.square(jnp.maximum(_dot(h_ref[...], wu_ref[...]), 0.0))
    acc_ref[...] += _dot(a.astype(BF16), wd_ref[...])

    @pl.when(k == pl.num_programs(1) - 1)
    def _():
        o_ref[...] = _layer_norm(alpha * res_ref[...] + acc_ref[...], g_ref[...], b_ref[...])


def _ffn(h_bf, h_f32, w_up, w_down, g, b, *, alpha, tm=512, tf=512):
    m, d = h_bf.shape
    dff = w_up.shape[1]
    tm = min(tm, m)
    return pl.pallas_call(
        functools.partial(_ffn_body, alpha=alpha),
        grid=(m // tm, dff // tf),
        in_specs=[pl.BlockSpec((tm, d), lambda i, k: (i, 0)),
                  pl.BlockSpec((d, tf), lambda i, k: (0, k)),
                  pl.BlockSpec((tf, d), lambda i, k: (k, 0)),
                  pl.BlockSpec((tm, d), lambda i, k: (i, 0)),
                  pl.BlockSpec((1, d), lambda i, k: (0, 0)),
                  pl.BlockSpec((1, d), lambda i, k: (0, 0))],
        out_specs=pl.BlockSpec((tm, d), lambda i, k: (i, 0)),
        out_shape=jax.ShapeDtypeStruct((m, d), F32),
        scratch_shapes=[pltpu.VMEM((tm, d), F32)],
        compiler_params=_cparams(("parallel", "arbitrary")),
        name="ffn_ln",
    )(h_bf, w_up, w_down, h_f32, g, b)


def _finish(x, x_bf, o_fox, o_pool, o_mem, w, *, alpha, fox_transposed):
    d = x.shape[1]
    merged = _merge(x_bf, o_fox, o_pool, o_mem, w["br_fox"], w["br_pool"], w["br_mem"],
                    w["rest"], w["gate_col0"], fox_transposed=fox_transposed)
    row = lambda i, j: (0, 0)
    h, h_bf = _mm(merged, w["out"], [(F32, False), (BF16, False)],
                  functools.partial(_ep_residual_ln, alpha=alpha),
                  extras=[(x, (min(512, x.shape[0]), d), lambda i, j: (i, 0)),
                          (w["ln1_g"], (1, d), row), (w["ln1_b"], (1, d), row)],
                  tm=512, tn=d, name="out_proj_ln")
    return _ffn(h_bf, h, w["up"], w["down"], w["ln2_g"], w["ln2_b"], alpha=alpha)


def kernel(x_prompt, x_sample, mem_prompt, cache_fox_k, cache_fox_v, cache_fox_logf, state_pool,
           cache_mem_k, cache_mem_v, page_table, w_in, b_fgate, w_pool, pool_scale, w_br_fox,
           w_br_pool, w_br_mem, w_mem_kv, w_out, ln1_g, ln1_b, w_up, w_down, ln2_g, ln2_b):
    batch, seq, d_model = x_prompt.shape
    dec_batch, dec_seq, _ = x_sample.shape
    depth = w_in.shape[0]
    n_phys, page_size = cache_fox_k.shape[1], cache_fox_k.shape[2]
    n_pages = page_table.shape[1]
    past = n_pages * page_size
    fox_w = FOX_HEADS * FOX_HD
    pool_w = len(POOL_WINDOWS) * POOL_GW
    mem_w = MEM_HEADS * MEM_HD
    mem_tokens = mem_prompt.shape[1]
    assert batch == 1 and dec_seq == FOX_HEADS == 8
    assert fox_w == pool_w == mem_w
    alpha = (2 * depth) ** 0.25
    nat, tr = False, True

    hp = x_prompt.reshape(seq, d_model)
    hs = x_sample.reshape(dec_batch * dec_seq, d_model)
    mem_bf = mem_prompt.reshape(mem_tokens, d_model).astype(BF16)
    seg = POOL_HALO + dec_seq
    pos_p = jnp.arange(seq, dtype=F32)[:, None]
    pos_s = jnp.tile(jnp.concatenate([jnp.full((POOL_HALO,), 1e9, F32),
                                      past + jnp.arange(dec_seq, dtype=F32)]), dec_batch)[:, None]

    outs = {n: [] for n in ("pk", "pv", "pf", "pp", "pmk", "pmv", "sk", "sv", "sf", "sp")}
    for l in range(depth):
        f0 = 3 * fox_w
        w_qkv = w_in[l][:, :f0].astype(BF16)
        w_f = jnp.pad(w_in[l][:, f0:f0 + FOX_HEADS], ((0, 0), (0, 128 - FOX_HEADS))).astype(BF16)
        b_f = jnp.pad(b_fgate[l], (0, 128 - FOX_HEADS))[None, :]
        w = dict(rest=w_in[l][:, f0 + FOX_HEADS:].astype(BF16), gate_col0=pool_w + mem_w,
                 br_fox=w_br_fox[l].astype(BF16), br_pool=w_br_pool[l].astype(BF16),
                 br_mem=w_br_mem[l].astype(BF16), out=w_out[l].astype(BF16),
                 up=w_up[l].astype(BF16), down=w_down[l].astype(BF16),
                 ln1_g=ln1_g[l][None, :], ln1_b=ln1_b[l][None, :],
                 ln2_g=ln2_g[l][None, :], ln2_b=ln2_b[l][None, :])
        wp_bf = w_pool[l].astype(BF16)
        psc = pool_scale[l][None, :]
        logf_of = functools.partial(_mm, w=w_f, outs=[(F32, nat)], epilogue=_ep_log_sigmoid,
                                    extras=[(b_f, (1, 128), lambda i, j: (0, 0))], tn=128, name="proj_logf")

        x_bf = hp.astype(BF16)
        q2T, = _mm(x_bf, w_qkv, [(BF16, tr)], functools.partial(_ep_scaled_t, scale=FOX_HD ** -0.5 * LOG2E),
                   col0=0, ncols=fox_w, name="proj_q")
        k, k_bf = _mm(x_bf, w_qkv, [(F32, nat), (BF16, nat)], _ep_dup, col0=1, ncols=fox_w, name="proj_k")
        v, vT = _mm(x_bf, w_qkv, [(F32, nat), (BF16, tr)], _ep_with_t, col0=2, ncols=fox_w, name="proj_v")
        u, = _mm(x_bf, w["rest"], [(F32, nat)], _ep_identity, col0=0, ncols=pool_w, name="proj_u")
        qm, = _mm(x_bf, w["rest"], [(BF16, nat)], _ep_identity, col0=1, ncols=mem_w, name="proj_qm")
        logf = logf_of(x_bf)[0][:, :FOX_HEADS]
        c2_rows = _cumsum_lanes(logf.T, scale=LOG2E)
        o_foxT = _fox_prompt(q2T, k_bf, vT, c2_rows)
        o_pool = _pool(u, pos_p, wp_bf, psc, tm=1024, zero_first=True)
        mkv, mkv_bf = _mm(mem_bf, w_mem_kv[l].astype(BF16), [(F32, nat), (BF16, nat)], _ep_dup, name="mem_kv")
        o_mem = _mem_prompt(qm, mkv_bf[:, :mem_w], mkv_bf[:, mem_w:])
        outs["pk"].append(k.reshape(batch, seq, FOX_HEADS, FOX_HD))
        outs["pv"].append(v.reshape(batch, seq, FOX_HEADS, FOX_HD))
        outs["pf"].append(logf.reshape(batch, seq, FOX_HEADS))
        outs["pp"].append(u[-POOL_STATE:].reshape(batch, POOL_STATE, pool_w))
        outs["pmk"].append(mkv[:, :mem_w].reshape(batch, mem_tokens, MEM_HEADS, MEM_HD))
        outs["pmv"].append(mkv[:, mem_w:].reshape(batch, mem_tokens, MEM_HEADS, MEM_HD))
        hp = _finish(hp, x_bf, o_foxT, o_pool, o_mem, w, alpha=alpha, fox_transposed=True)

        x_bf = hs.astype(BF16)
        q, = _mm(x_bf, w_qkv, [(BF16, nat)], _ep_identity, col0=0, ncols=fox_w, name="proj_q")
        k, = _mm(x_bf, w_qkv, [(F32, nat)], _ep_identity, col0=1, ncols=fox_w, name="proj_k")
        v, = _mm(x_bf, w_qkv, [(F32, nat)], _ep_identity, col0=2, ncols=fox_w, name="proj_v")
        u, = _mm(x_bf, w["rest"], [(F32, nat)], _ep_identity, col0=0, ncols=pool_w, name="proj_u")
        qm, = _mm(x_bf, w["rest"], [(BF16, nat)], _ep_identity, col0=1, ncols=mem_w, name="proj_qm")
        logf = logf_of(x_bf)[0][:, :FOX_HEADS]
        rows = dec_seq * FOX_HEADS
        o_fox = _fox_sample(
            q.reshape(dec_batch, rows, FOX_HD), k.reshape(dec_batch, rows, FOX_HD),
            v.reshape(dec_batch, rows, FOX_HD), logf.reshape(dec_batch, 1, rows),
            cache_fox_k[l].reshape(n_phys, page_size * FOX_HEADS, FOX_HD),
            cache_fox_v[l].reshape(n_phys, page_size * FOX_HEADS, FOX_HD),
            _page_logf_sums(cache_fox_logf[l]), page_table).reshape(dec_batch * dec_seq, fox_w)
        u3 = u.reshape(dec_batch, dec_seq, pool_w)
        state = state_pool[l].astype(F32)
        u_ext = jnp.concatenate([jnp.zeros((dec_batch, POOL_HALO - POOL_STATE, pool_w), F32), state, u3],
                                axis=1).reshape(dec_batch * seg, pool_w)
        o_pool = _pool(u_ext, pos_s, wp_bf, psc, tm=32 * seg, zero_first=False)
        o_pool = o_pool.reshape(dec_batch, seg, pool_w)[:, POOL_HALO:].reshape(dec_batch * dec_seq, pool_w)
        o_mem = _mem_sample(qm.reshape(dec_batch, dec_seq * MEM_HEADS, MEM_HD), cache_mem_k[l],
                            cache_mem_v[l]).reshape(dec_batch * dec_seq, mem_w)
        outs["sk"].append(k.reshape(dec_batch, dec_seq, FOX_HEADS, FOX_HD))
        outs["sv"].append(v.reshape(dec_batch, dec_seq, FOX_HEADS, FOX_HD))
        outs["sf"].append(logf.reshape(dec_batch, dec_seq, FOX_HEADS))
        outs["sp"].append(jnp.concatenate([state, u3], axis=1)[:, -POOL_STATE:])
        hs = _finish(hs, x_bf, o_fox, o_pool, o_mem, w, alpha=alpha, fox_transposed=False)

    st = lambda n: jnp.stack(outs[n])
    return (hp.reshape(batch, seq, d_model), hs.reshape(dec_batch, dec_seq, d_model),
            st("pk"), st("pv"), st("pf"), st("pp"), st("pmk"), st("pmv"),
            st("sk"), st("sv"), st("sf"), st("sp"))
```

```python
import functools

import numpy as np

import jax
import jax.numpy as jnp
from jax import lax
from jax.experimental import pallas as pl
from jax.experimental.pallas import tpu as pltpu

F32 = jnp.float32
BF16 = jnp.bfloat16

FOX_HEADS = 8
FOX_HD = 128
POOL_WINDOWS = (2, 4, 8, 16)
POOL_GW = 256
POOL_STATE = max(POOL_WINDOWS) - 1
POOL_HALO = 16
MEM_HEADS = 4
MEM_HD = 256
LN_EPS = 1e-5
MASK_VALUE = -1e30
LOG2E = 1.4426950408889634
FOX_EXT = 128
FOX_ONES = 16

V7X_VMEM_LIMIT_BYTES = 56 * 1024 * 1024


def _cparams(semantics):
    return pltpu.CompilerParams(dimension_semantics=semantics,
                                vmem_limit_bytes=V7X_VMEM_LIMIT_BYTES)


def _dot(a, b):
    return jnp.dot(a, b, preferred_element_type=F32)


def _dot_nt(a, b):
    return lax.dot_general(a, b, (((1,), (1,)), ((), ())), preferred_element_type=F32)


def _dot_tn(a, b):
    return lax.dot_general(a, b, (((0,), (0,)), ((), ())), preferred_element_type=F32)


def _split3(x):
    hi = x.astype(BF16)
    r1 = x - hi.astype(F32)
    mid = r1.astype(BF16)
    lo = (r1 - mid.astype(F32)).astype(BF16)
    return hi, mid, lo


def _split3_dot(x, t):
    hi, mid, lo = _split3(x)
    return _dot(hi, t) + _dot(mid, t) + _dot(lo, t)


def _layer_norm(z, g, b):
    mu = jnp.mean(z, axis=-1, keepdims=True)
    zc = z - mu
    var = jnp.mean(zc * zc, axis=-1, keepdims=True)
    return zc * lax.rsqrt(var + LN_EPS) * g + b


def _mm_body(*refs, epilogue, n_in, emit_x):
    x_ref, w_ref = refs[0], refs[1]
    extras = refs[2:n_in]
    outs = list(refs[n_in:])
    x = x_ref[...].astype(BF16)
    if emit_x:
        outs.pop(0)[...] = x
    acc = _dot(x, w_ref[...].astype(BF16))
    res = epilogue(acc, *(e[...] for e in extras))
    for o, r in zip(outs, res):
        o[...] = r.astype(o.dtype)


def _mm(x, w, outs, epilogue, *, extras=(), tm=1024, tn=1024, col0=0, ncols=None, emit_x=False, name):
    m, k = x.shape
    n = w.shape[1] if ncols is None else ncols
    tm, tn = min(tm, m), min(tn, n)
    assert m % tm == 0 and n % tn == 0 and (n == tn or not emit_x)
    in_specs = [pl.BlockSpec((tm, k), lambda i, j: (i, 0)),
                pl.BlockSpec((k, tn), lambda i, j: (0, j + col0))]
    in_specs += [pl.BlockSpec(bs, im) for _, bs, im in extras]
    out_specs = [pl.BlockSpec((tn, tm), lambda i, j: (j, i)) if t else
                 pl.BlockSpec((tm, tn), lambda i, j: (i, j)) for _, t in outs]
    out_shape = [jax.ShapeDtypeStruct((n, m) if t else (m, n), dt) for dt, t in outs]
    if emit_x:
        out_specs.insert(0, pl.BlockSpec((tm, k), lambda i, j: (i, 0)))
        out_shape.insert(0, jax.ShapeDtypeStruct((m, k), BF16))
    return pl.pallas_call(
        functools.partial(_mm_body, epilogue=epilogue, n_in=2 + len(extras), emit_x=emit_x),
        grid=(m // tm, n // tn),
        in_specs=in_specs, out_specs=out_specs, out_shape=out_shape,
        compiler_params=_cparams(("parallel", "arbitrary")),
        name=name,
    )(x, w, *(a for a, _, _ in extras))


def _shift_cols_body(a_ref, b_ref, o_ref, *, shift):
    x = jnp.concatenate([a_ref[...], b_ref[...]], axis=1).astype(BF16)
    o_ref[...] = pltpu.roll(x, x.shape[1] - shift, 1)[:, :o_ref.shape[1]]


def _shifted_cols_bf16(w, start, ncols, *, tn=1024):
    k = w.shape[0]
    base, shift = start // tn, start % tn
    assert ncols % tn == 0 and 0 < shift < 128 and tn % 128 == 0
    return pl.pallas_call(
        functools.partial(_shift_cols_body, shift=shift), grid=(ncols // tn,),
        in_specs=[pl.BlockSpec((k, tn), lambda j: (0, base + j)),
                  pl.BlockSpec((k, 128), lambda j: (0, (base + j + 1) * (tn // 128)))],
        out_specs=pl.BlockSpec((k, tn), lambda j: (0, j)),
        out_shape=jax.ShapeDtypeStruct((k, ncols), BF16),
        compiler_params=_cparams(("parallel",)),
        name="rest_weights",
    )(w, w)


def _ep_identity(acc):
    return (acc,)


def _ep_dup(acc):
    return (acc, acc)


def _ep_scaled_t(acc, *, scale):
    return ((acc * scale).T,)


def _ep_with_t(acc):
    return (acc, acc.T)


def _ep_log_sigmoid(acc, bias):
    z = acc + bias
    return (jnp.minimum(z, 0.0) - jnp.log1p(jnp.exp(-jnp.abs(z))),)


def _cumsum_body(x_ref, tri_ref, o_ref, *, chunk, scale):
    n = x_ref.shape[1]
    carry = jnp.zeros((x_ref.shape[0], 1), F32)
    for c in range(n // chunk):
        cs = _split3_dot(x_ref[:, c * chunk:(c + 1) * chunk], tri_ref[...]) + carry
        o_ref[:, c * chunk:(c + 1) * chunk] = cs * scale
        carry = cs[:, chunk - 1:chunk]


def _cumsum_lanes(x, *, scale, chunk=512):
    rows, n = x.shape
    tri = jnp.asarray(np.triu(np.ones((chunk, chunk), np.float32)), BF16)
    return pl.pallas_call(
        functools.partial(_cumsum_body, chunk=chunk, scale=scale),
        out_shape=jax.ShapeDtypeStruct((rows, n), F32),
        name="logf_cumsum",
    )(x, tri)


def _bias_rows(c, ones_first, n):
    sub = lax.broadcasted_iota(jnp.int32, (n, c.shape[1]), 0)
    c0, o0 = (3, 0) if ones_first else (0, 3)
    rows = jnp.where((sub >= o0) & (sub < o0 + 3), 1.0, 0.0).astype(BF16)
    for t, piece in enumerate(_split3(c)):
        rows = jnp.where(sub == c0 + t, piece, rows)
    return rows


def _fox_prompt_body(qT_ref, k_ref, vT_ref, c_ref, oT_ref, kext, s0, s1, m_scr, acc_scr, *, tq):
    i = pl.program_id(1)
    tk = tq

    @pl.when(i == 0)
    def _():
        eye = (lax.broadcasted_iota(jnp.int32, (FOX_ONES, FOX_EXT), 0)
               == lax.broadcasted_iota(jnp.int32, (FOX_ONES, FOX_EXT), 1)).astype(BF16)
        for c in range(k_ref.shape[0] // tk):
            rows = _bias_rows(-c_ref[0, :, c * tk:(c + 1) * tk], True, FOX_ONES)
            kext[c * tk:(c + 1) * tk, :] = _dot_tn(rows, eye).astype(BF16)

    last = pl.multiple_of((i + 1) * tq - 128, 128)
    cq = c_ref[0, :, pl.ds(last, 128)][:, 127:128]
    qa = jnp.concatenate([qT_ref[...], jnp.broadcast_to(_bias_rows(cq, False, FOX_EXT), (FOX_EXT, tq))],
                         axis=0)
    ones = jnp.ones((FOX_ONES, tk), BF16)

    def scores(j, dst):
        ks = pl.multiple_of(j * tk, tk)
        ka = jnp.concatenate([k_ref[pl.ds(ks, tk), :], kext[pl.ds(ks, tk), :]], axis=1)
        dst[...] = _dot(ka, qa)

    def absorb(j, src, diag):
        ks = pl.multiple_of(j * tk, tk)
        sT = src[...]
        if diag:
            key = lax.broadcasted_iota(jnp.int32, (tk, tq), 0)
            qry = lax.broadcasted_iota(jnp.int32, (tk, tq), 1)
            sT = jnp.where(key <= qry, sT, MASK_VALUE)
        m = m_scr[...]
        m_new = jnp.maximum(m, jnp.max(sT, axis=0, keepdims=True))
        p = jnp.exp2(sT - m_new).astype(BF16)
        m_scr[...] = m_new
        va = jnp.concatenate([vT_ref[:, pl.ds(ks, tk)], ones], axis=0)
        acc_scr[...] = jnp.exp2(m - m_new) * acc_scr[...] + _dot(va, p)

    def finish(src):
        absorb(i, src, True)
        oT_ref[...] = (acc_scr[:FOX_HD, :] / acc_scr[FOX_HD:FOX_HD + 1, :]).astype(oT_ref.dtype)

    m_scr[...] = jnp.full(m_scr.shape, MASK_VALUE, F32)
    acc_scr[...] = jnp.zeros(acc_scr.shape, F32)
    scores(0, s0)

    def pair(t, carry):
        j = 2 * t
        scores(j + 1, s1)
        absorb(j, s0, False)
        scores(j + 2, s0)
        absorb(j + 1, s1, False)
        return carry

    lax.fori_loop(0, i // 2, pair, 0)

    @pl.when(i % 2 == 0)
    def _():
        finish(s0)

    @pl.when(i % 2 == 1)
    def _():
        scores(i, s1)
        absorb(i - 1, s0, False)
        finish(s1)


def _fox_prompt(q2T, k, vT, c2_rows, *, tq=1024):
    h, s = c2_rows.shape
    return pl.pallas_call(
        functools.partial(_fox_prompt_body, tq=tq),
        grid=(h, s // tq),
        in_specs=[pl.BlockSpec((FOX_HD, tq), lambda g, i: (g, i)),
                  pl.BlockSpec((s, FOX_HD), lambda g, i: (0, g)),
                  pl.BlockSpec((FOX_HD, s), lambda g, i: (g, 0)),
                  pl.BlockSpec((1, 1, s), lambda g, i: (g, 0, 0))],
        out_specs=pl.BlockSpec((FOX_HD, tq), lambda g, i: (g, i)),
        out_shape=jax.ShapeDtypeStruct((h * FOX_HD, s), BF16),
        scratch_shapes=[pltpu.VMEM((s, FOX_EXT), BF16), pltpu.VMEM((tq, tq), F32), pltpu.VMEM((tq, tq), F32),
                        pltpu.VMEM((1, tq), F32), pltpu.VMEM((FOX_HD + FOX_ONES, tq), F32)],
        compiler_params=_cparams(("parallel", "arbitrary")),
        name="fox_prompt_attention",
    )(q2T, k, vT, c2_rows[:, None, :])


def _fox_sample_body(pt_ref, q_ref, kn_ref, vn_ref, lfn_ref, msk_ref, t4_ref, t5_ref, *rest,
                     n_pages, scale):
    del pt_ref
    k_pages = rest[:n_pages]
    v_pages = rest[n_pages:2 * n_pages]
    lf_pages = rest[2 * n_pages:3 * n_pages]
    o_ref, s_scr = rest[3 * n_pages], rest[3 * n_pages + 1]
    rows = q_ref.shape[1]
    pk = k_pages[0].shape[1]
    h = FOX_HEADS

    q = q_ref[0]
    r = lfn_ref[0]
    bq = _split3_dot(jnp.broadcast_to(r, (rows, rows)) * msk_ref[...], t4_ref[...])
    cn = _split3_dot(jnp.broadcast_to(r, (8, rows)), t5_ref[...])[0:1]

    row_h = lax.broadcasted_iota(jnp.int32, (rows, 128), 0) & (h - 1)
    lane_h = lax.broadcasted_iota(jnp.int32, (rows, 128), 1) & (h - 1)
    same_head = row_h == lane_h
    mbq = jnp.where(same_head, bq, MASK_VALUE)
    mbq_page = jnp.concatenate([mbq] * (pk // 128), axis=1)

    kn = kn_ref[0].astype(BF16)
    row_t = lax.broadcasted_iota(jnp.int32, (rows, rows), 0) >> 3
    lane_t = lax.broadcasted_iota(jnp.int32, (rows, rows), 1) >> 3
    ok_new = same_head[:, :rows] & (lane_t <= row_t)
    s_new = jnp.where(ok_new, _dot_nt(q, kn) * scale + (bq[:, :rows] - cn), MASK_VALUE)
    m = jnp.max(s_new, axis=-1, keepdims=True)

    later = jnp.zeros((1, 128), F32)
    for p in reversed(range(n_pages)):
        lf = lf_pages[p][0]
        tot = lf[:, pk:]
        d = jnp.concatenate([later + tot] * (pk // 128), axis=1) - lf[:, :pk]
        later = later + tot
        s = _dot_nt(q, k_pages[p][0].astype(BF16)) * scale + (mbq_page + d)
        s_scr[:, p * pk:(p + 1) * pk] = s
        m = jnp.maximum(m, jnp.max(s, axis=-1, keepdims=True))

    p_new = jnp.exp(s_new - m)
    l = jnp.sum(p_new, axis=-1, keepdims=True)
    acc = _dot(p_new.astype(BF16), vn_ref[0].astype(BF16))
    for p in range(n_pages):
        pr = jnp.exp(s_scr[:, p * pk:(p + 1) * pk] - m)
        l = l + jnp.sum(pr, axis=-1, keepdims=True)
        acc = acc + _dot(pr.astype(BF16), v_pages[p][0].astype(BF16))
    o_ref[0] = (acc / l).astype(o_ref.dtype)


def _fox_sample(q, k_new, v_new, lf_new, cache_k, cache_v, lf_cache, page_table):
    b, rows, d = q.shape
    n_pages = page_table.shape[1]
    pk = cache_k.shape[1]
    assert rows == 64 and FOX_HEADS == 8 and pk % 128 == 0

    rr = np.arange(rows)
    tt, hh = rr // FOX_HEADS, rr % FOX_HEADS
    lane = np.arange(128)
    msk = jnp.asarray((tt[None, :] <= tt[:, None]).astype(np.float32))
    t4 = jnp.asarray((hh[:, None] == (lane % FOX_HEADS)[None, :]).astype(np.float32), BF16)
    t5 = jnp.asarray(((hh[:, None] == hh[None, :]) & (tt[:, None] <= tt[None, :])).astype(np.float32), BF16)

    def page_map(p):
        return lambda i, pt: (pt[i * n_pages + p], 0, 0)

    per_b = lambda i, pt: (i, 0, 0)
    const = lambda i, pt: (0, 0)
    in_specs = [pl.BlockSpec((1, rows, d), per_b),
                pl.BlockSpec((1, rows, d), per_b),
                pl.BlockSpec((1, rows, d), per_b),
                pl.BlockSpec((1, 1, rows), per_b),
                pl.BlockSpec((rows, rows), const),
                pl.BlockSpec((rows, 128), const),
                pl.BlockSpec((rows, rows), const)]
    in_specs += [pl.BlockSpec((1, pk, d), page_map(p)) for p in range(n_pages)]
    in_specs += [pl.BlockSpec((1, pk, d), page_map(p)) for p in range(n_pages)]
    in_specs += [pl.BlockSpec((1, 1, pk + 128), page_map(p)) for p in range(n_pages)]
    grid_spec = pltpu.PrefetchScalarGridSpec(
        num_scalar_prefetch=1, grid=(b,), in_specs=in_specs,
        out_specs=pl.BlockSpec((1, rows, d), per_b),
        scratch_shapes=[pltpu.VMEM((rows, n_pages * pk), F32)])
    return pl.pallas_call(
        functools.partial(_fox_sample_body, n_pages=n_pages, scale=FOX_HD ** -0.5),
        grid_spec=grid_spec,
        out_shape=jax.ShapeDtypeStruct((b, rows, d), BF16),
        compiler_params=_cparams(("arbitrary",)),
        name="fox_sample_attention",
    )(page_table.reshape(-1), q, k_new, v_new, lf_new, msk, t4, t5,
      *([cache_k] * n_pages), *([cache_v] * n_pages), *([lf_cache] * n_pages))


def _page_logf_sums(cache_logf):
    p, page, h = cache_logf.shape
    n = page * h
    pos, head = np.arange(n) // h, np.arange(n) % h
    prefix = (head[:, None] == head[None, :]) & (pos[:, None] <= pos[None, :])
    total = head[:, None] == (np.arange(128) % h)[None, :]
    t = jnp.asarray(np.concatenate([prefix, total], axis=1).astype(np.float32), BF16)
    tm = next(c for c in (512, 256, 128, 64, 32, 16, 8) if p % c == 0)

    def body(x_ref, t_ref, o_ref):
        o_ref[...] = _split3_dot(x_ref[...], t_ref[...])

    out = pl.pallas_call(
        body, grid=(p // tm,),
        in_specs=[pl.BlockSpec((tm, n), lambda i: (i, 0)),
                  pl.BlockSpec((n, n + 128), lambda i: (0, 0))],
        out_specs=pl.BlockSpec((tm, n + 128), lambda i: (i, 0)),
        out_shape=jax.ShapeDtypeStruct((p, n + 128), F32),
        compiler_params=_cparams(("parallel",)),
        name="page_logf_sums",
    )(cache_logf.reshape(p, n), t)
    return out.reshape(p, 1, n + 128)


def _pool_body(u_ref, halo_ref, pos_ref, wp_ref, sc_ref, o_ref, *, zero_first):
    u = u_ref[...]
    halo = halo_ref[...]
    if zero_first:
        halo = jnp.where(pl.program_id(0) == 0, 0.0, halo)
    pos1 = pos_ref[...] + 1.0
    outs = []
    for g, w in enumerate(POOL_WINDOWS):
        sl = slice(g * POOL_GW, (g + 1) * POOL_GW)
        s = jnp.concatenate([halo[:, sl], u[:, sl]], axis=0)
        sh = 1
        while sh < w:
            s = s + pltpu.roll(s, sh, 0)
            sh *= 2
        d = s[POOL_HALO:] / jnp.minimum(float(w), pos1) - u[:, sl]
        outs.append(_dot(d.astype(BF16), wp_ref[g]))
    o_ref[...] = (jnp.concatenate(outs, axis=1) * sc_ref[...]).astype(o_ref.dtype)


def _pool(u, pos, w_pool, pool_scale, *, tm, zero_first):
    assert all(w & (w - 1) == 0 for w in POOL_WINDOWS) and tm % POOL_HALO == 0
    r, width = u.shape
    hb = tm // POOL_HALO
    return pl.pallas_call(
        functools.partial(_pool_body, zero_first=zero_first),
        grid=(r // tm,),
        in_specs=[pl.BlockSpec((tm, width), lambda i: (i, 0)),
                  pl.BlockSpec((POOL_HALO, width), lambda i: (jnp.maximum(i * hb - 1, 0), 0)),
                  pl.BlockSpec((tm, 1), lambda i: (i, 0)),
                  pl.BlockSpec(w_pool.shape, lambda i: (0, 0, 0)),
                  pl.BlockSpec((1, width), lambda i: (0, 0))],
        out_specs=pl.BlockSpec((tm, width), lambda i: (i, 0)),
        out_shape=jax.ShapeDtypeStruct((r, width), BF16),
        compiler_params=_cparams(("arbitrary",)),
        name="pool_mixer",
    )(u, u, pos, w_pool, pool_scale)


def _mem_prompt_body(q_ref, k_ref, v_ref, o_ref, *, scale):
    for h in range(MEM_HEADS):
        sl = slice(h * MEM_HD, (h + 1) * MEM_HD)
        s = _dot_nt(q_ref[:, sl], k_ref[:, sl]) * scale
        p = jnp.exp(s - jnp.max(s, axis=-1, keepdims=True))
        l = jnp.sum(p, axis=-1, keepdims=True)
        o_ref[:, sl] = (_dot(p.astype(BF16), v_ref[:, sl]) / l).astype(o_ref.dtype)


def _mem_prompt(qm, mk, mv, *, tq=1024):
    s, width = qm.shape
    m = mk.shape[0]
    return pl.pallas_call(
        functools.partial(_mem_prompt_body, scale=MEM_HD ** -0.5),
        grid=(s // tq,),
        in_specs=[pl.BlockSpec((tq, width), lambda i: (i, 0)),
                  pl.BlockSpec((m, width), lambda i: (0, 0)),
                  pl.BlockSpec((m, width), lambda i: (0, 0))],
        out_specs=pl.BlockSpec((tq, width), lambda i: (i, 0)),
        out_shape=jax.ShapeDtypeStruct((s, width), BF16),
        compiler_params=_cparams(("parallel",)),
        name="mem_prompt_attention",
    )(qm, mk, mv)


def _mem_sample_body(q_ref, k_ref, v_ref, o_ref, *, bb, scale):
    rows = q_ref.shape[1]
    m, h, d = k_ref.shape[1:]
    row_h = lax.broadcasted_iota(jnp.int32, (rows, m * h), 0) & (h - 1)
    lane_h = lax.broadcasted_iota(jnp.int32, (rows, m * h), 1) & (h - 1)
    own = row_h == lane_h
    for b in range(bb):
        k2 = k_ref[b].reshape(m * h, d).astype(BF16)
        v2 = v_ref[b].reshape(m * h, d).astype(BF16)
        s = jnp.where(own, _dot_nt(q_ref[b], k2) * scale, MASK_VALUE)
        p = jnp.exp(s - jnp.max(s, axis=-1, keepdims=True))
        l = jnp.sum(p, axis=-1, keepdims=True)
        o_ref[b] = (_dot(p.astype(BF16), v2) / l).astype(o_ref.dtype)


def _mem_sample(qm, ck, cv, *, bb=4):
    b, rows, d = qm.shape
    _, m, h, _ = ck.shape
    assert h & (h - 1) == 0 and b % bb == 0
    return pl.pallas_call(
        functools.partial(_mem_sample_body, bb=bb, scale=MEM_HD ** -0.5),
        grid=(b // bb,),
        in_specs=[pl.BlockSpec((bb, rows, d), lambda i: (i, 0, 0)),
                  pl.BlockSpec((bb, m, h, d), lambda i: (i, 0, 0, 0)),
                  pl.BlockSpec((bb, m, h, d), lambda i: (i, 0, 0, 0))],
        out_specs=pl.BlockSpec((bb, rows, d), lambda i: (i, 0, 0)),
        out_shape=jax.ShapeDtypeStruct((b, rows, d), BF16),
        compiler_params=_cparams(("parallel",)),
        name="mem_sample_attention",
    )(qm, ck, cv)


def _merge_body(x_ref, of_ref, op_ref, om_ref, wf_ref, wp_ref, wm_ref, g0_ref, g1_ref, g2_ref,
                o_ref, *, fox_transposed):
    x = x_ref[...]
    fox = (_dot_tn if fox_transposed else _dot)(of_ref[...], wf_ref[...])
    merged = (jax.nn.sigmoid(_dot(x, g0_ref[...])) * fox
              + jax.nn.sigmoid(_dot(x, g1_ref[...])) * _dot(op_ref[...], wp_ref[...])
              + jax.nn.sigmoid(_dot(x, g2_ref[...])) * _dot(om_ref[...], wm_ref[...]))
    o_ref[...] = merged.astype(o_ref.dtype)


def _merge(x, o_fox, o_pool, o_mem, w_f, w_p, w_m, w_gates, gate_col0, *, fox_transposed,
           tm=512, tn=512):
    m, d = x.shape
    kb = o_pool.shape[1]
    n = w_f.shape[1]
    tm = min(tm, m)
    nb = n // tn
    assert gate_col0 % tn == 0
    act = pl.BlockSpec((tm, kb), lambda i, j: (i, 0))
    act_fox = pl.BlockSpec((kb, tm), lambda i, j: (0, i)) if fox_transposed else act
    wgt = pl.BlockSpec((kb, tn), lambda i, j: (0, j))
    gate = lambda g: pl.BlockSpec((d, tn), lambda i, j: (0, gate_col0 // tn + g * nb + j))
    return pl.pallas_call(
        functools.partial(_merge_body, fox_transposed=fox_transposed), grid=(m // tm, nb),
        in_specs=[pl.BlockSpec((tm, d), lambda i, j: (i, 0)), act_fox, act, act, wgt, wgt, wgt,
                  gate(0), gate(1), gate(2)],
        out_specs=pl.BlockSpec((tm, tn), lambda i, j: (i, j)),
        out_shape=jax.ShapeDtypeStruct((m, n), BF16),
        compiler_params=_cparams(("parallel", "arbitrary")),
        name="branch_merge",
    )(x, o_fox, o_pool, o_mem, w_f, w_p, w_m, w_gates, w_gates, w_gates)


def _ep_residual_ln(acc, x, g, b, *, alpha):
    return (_layer_norm(alpha * x + acc, g, b),)


def _ffn_body(h_ref, wu_ref, wd_ref, g_ref, b_ref, o_ref, hb_ref, *, alpha):
    k = pl.program_id(1)

    @pl.when(k == 0)
    def _():
        hb_ref[...] = h_ref[...].astype(BF16)
        o_ref[...] = jnp.zeros_like(o_ref)

    a = jnp.square(jnp.maximum(_dot(hb_ref[...], wu_ref[...]), 0.0))
    o_ref[...] += _dot(a.astype(BF16), wd_ref[...])

    @pl.when(k == pl.num_programs(1) - 1)
    def _():
        o_ref[...] = _layer_norm(alpha * h_ref[...] + o_ref[...], g_ref[...], b_ref[...])


def _ffn(h, w_up, w_down, g, b, *, alpha, tm=1024, tf=512):
    m, d = h.shape
    dff = w_up.shape[1]
    tm = min(tm, m)
    return pl.pallas_call(
        functools.partial(_ffn_body, alpha=alpha),
        grid=(m // tm, dff // tf),
        in_specs=[pl.BlockSpec((tm, d), lambda i, k: (i, 0)),
                  pl.BlockSpec((d, tf), lambda i, k: (0, k)),
                  pl.BlockSpec((tf, d), lambda i, k: (k, 0)),
                  pl.BlockSpec((1, d), lambda i, k: (0, 0)),
                  pl.BlockSpec((1, d), lambda i, k: (0, 0))],
        out_specs=pl.BlockSpec((tm, d), lambda i, k: (i, 0)),
        out_shape=jax.ShapeDtypeStruct((m, d), F32),
        scratch_shapes=[pltpu.VMEM((tm, d), BF16)],
        compiler_params=_cparams(("parallel", "arbitrary")),
        name="ffn_ln",
    )(h, w_up, w_down, g, b)


def _finish(x, x_bf, o_fox, o_pool, o_mem, w, *, alpha, fox_transposed):
    d = x.shape[1]
    merged = _merge(x_bf, o_fox, o_pool, o_mem, w["br_fox"], w["br_pool"], w["br_mem"],
                    w["rest"], w["gate_col0"], fox_transposed=fox_transposed)
    row = lambda i, j: (0, 0)
    h, = _mm(merged, w["out"], [(F32, False)], functools.partial(_ep_residual_ln, alpha=alpha),
             extras=[(x, (min(512, x.shape[0]), d), lambda i, j: (i, 0)),
                     (w["ln1_g"], (1, d), row), (w["ln1_b"], (1, d), row)],
             tm=512, tn=d, name="out_proj_ln")
    return _ffn(h, w["up"], w["down"], w["ln2_g"], w["ln2_b"], alpha=alpha)


def kernel(x_prompt, x_sample, mem_prompt, cache_fox_k, cache_fox_v, cache_fox_logf, state_pool,
           cache_mem_k, cache_mem_v, page_table, w_in, b_fgate, w_pool, pool_scale, w_br_fox,
           w_br_pool, w_br_mem, w_mem_kv, w_out, ln1_g, ln1_b, w_up, w_down, ln2_g, ln2_b):
    batch, seq, d_model = x_prompt.shape
    dec_batch, dec_seq, _ = x_sample.shape
    depth = w_in.shape[0]
    n_phys, page_size = cache_fox_k.shape[1], cache_fox_k.shape[2]
    n_pages = page_table.shape[1]
    past = n_pages * page_size
    fox_w = FOX_HEADS * FOX_HD
    pool_w = len(POOL_WINDOWS) * POOL_GW
    mem_w = MEM_HEADS * MEM_HD
    mem_tokens = mem_prompt.shape[1]
    assert batch == 1 and dec_seq == FOX_HEADS == 8
    assert fox_w == pool_w == mem_w
    alpha = (2 * depth) ** 0.25
    nat, tr = False, True

    hp = x_prompt.reshape(seq, d_model)
    hs = x_sample.reshape(dec_batch * dec_seq, d_model)
    seg = POOL_HALO + dec_seq
    pos_p = jnp.arange(seq, dtype=F32)[:, None]
    pos_s = jnp.tile(jnp.concatenate([jnp.full((POOL_HALO,), 1e9, F32),
                                      past + jnp.arange(dec_seq, dtype=F32)]), dec_batch)[:, None]

    outs = {n: [] for n in ("pk", "pv", "pf", "pp", "pmk", "pmv", "sk", "sv", "sf", "sp")}
    for l in range(depth):
        f0 = 3 * fox_w
        w_qkv = w_in[l]
        w_f = jnp.pad(w_in[l][:, f0:f0 + FOX_HEADS], ((0, 0), (0, 128 - FOX_HEADS))).astype(BF16)
        b_f = jnp.pad(b_fgate[l], (0, 128 - FOX_HEADS))[None, :]
        w = dict(rest=_shifted_cols_bf16(w_in[l], f0 + FOX_HEADS, pool_w + mem_w + 3 * d_model),
                 gate_col0=pool_w + mem_w,
                 br_fox=w_br_fox[l].astype(BF16), br_pool=w_br_pool[l].astype(BF16),
                 br_mem=w_br_mem[l].astype(BF16), out=w_out[l].astype(BF16),
                 up=w_up[l].astype(BF16), down=w_down[l].astype(BF16),
                 ln1_g=ln1_g[l][None, :], ln1_b=ln1_b[l][None, :],
                 ln2_g=ln2_g[l][None, :], ln2_b=ln2_b[l][None, :])
        wp_bf = w_pool[l].astype(BF16)
        psc = pool_scale[l][None, :]
        logf_of = functools.partial(_mm, w=w_f, outs=[(F32, nat)], epilogue=_ep_log_sigmoid,
                                    extras=[(b_f, (1, 128), lambda i, j: (0, 0))], tn=128, name="proj_logf")

        x_bf, q2T = _mm(hp, w_qkv, [(BF16, tr)], functools.partial(_ep_scaled_t, scale=FOX_HD ** -0.5 * LOG2E),
                        col0=0, ncols=fox_w, tm=512, emit_x=True, name="proj_q")
        k, k_bf = _mm(x_bf, w_qkv, [(F32, nat), (BF16, nat)], _ep_dup, col0=1, ncols=fox_w, name="proj_k")
        v, vT = _mm(x_bf, w_qkv, [(F32, nat), (BF16, tr)], _ep_with_t, col0=2, ncols=fox_w, name="proj_v")
        u, = _mm(x_bf, w["rest"], [(F32, nat)], _ep_identity, col0=0, ncols=pool_w, name="proj_u")
        qm, = _mm(x_bf, w["rest"], [(BF16, nat)], _ep_identity, col0=1, ncols=mem_w, name="proj_qm")
        logf = logf_of(x_bf)[0][:, :FOX_HEADS]
        c2_rows = _cumsum_lanes(logf.T, scale=LOG2E)
        o_foxT = _fox_prompt(q2T, k_bf, vT, c2_rows)
        o_pool = _pool(u, pos_p, wp_bf, psc, tm=1024, zero_first=True)
        mkv, mkv_bf = _mm(mem_prompt.reshape(mem_tokens, d_model), w_mem_kv[l], [(F32, nat), (BF16, nat)],
                          _ep_dup, name="mem_kv")
        o_mem = _mem_prompt(qm, mkv_bf[:, :mem_w], mkv_bf[:, mem_w:])
        outs["pk"].append(k.reshape(batch, seq, FOX_HEADS, FOX_HD))
        outs["pv"].append(v.reshape(batch, seq, FOX_HEADS, FOX_HD))
        outs["pf"].append(logf.reshape(batch, seq, FOX_HEADS))
        outs["pp"].append(u[-POOL_STATE:].reshape(batch, POOL_STATE, pool_w))
        outs["pmk"].append(mkv[:, :mem_w].reshape(batch, mem_tokens, MEM_HEADS, MEM_HD))
        outs["pmv"].append(mkv[:, mem_w:].reshape(batch, mem_tokens, MEM_HEADS, MEM_HD))
        hp = _finish(hp, x_bf, o_foxT, o_pool, o_mem, w, alpha=alpha, fox_transposed=True)

        x_bf, q = _mm(hs, w_qkv, [(BF16, nat)], _ep_identity, col0=0, ncols=fox_w, tm=512, emit_x=True,
                      name="proj_q")
        k, = _mm(x_bf, w_qkv, [(F32, nat)], _ep_identity, col0=1, ncols=fox_w, name="proj_k")
        v, = _mm(x_bf, w_qkv, [(F32, nat)], _ep_identity, col0=2, ncols=fox_w, name="proj_v")
        u, = _mm(x_bf, w["rest"], [(F32, nat)], _ep_identity, col0=0, ncols=pool_w, name="proj_u")
        qm, = _mm(x_bf, w["rest"], [(BF16, nat)], _ep_identity, col0=1, ncols=mem_w, name="proj_qm")
        logf = logf_of(x_bf)[0][:, :FOX_HEADS]
        rows = dec_seq * FOX_HEADS
        o_fox = _fox_sample(
            q.reshape(dec_batch, rows, FOX_HD), k.reshape(dec_batch, rows, FOX_HD),
            v.reshape(dec_batch, rows, FOX_HD), logf.reshape(dec_batch, 1, rows),
            cache_fox_k[l].reshape(n_phys, page_size * FOX_HEADS, FOX_HD),
            cache_fox_v[l].reshape(n_phys, page_size * FOX_HEADS, FOX_HD),
            _page_logf_sums(cache_fox_logf[l]), page_table).reshape(dec_batch * dec_seq, fox_w)
        u3 = u.reshape(dec_batch, dec_seq, pool_w)
        state = state_pool[l].astype(F32)
        u_ext = jnp.concatenate([jnp.zeros((dec_batch, POOL_HALO - POOL_STATE, pool_w), F32), state, u3],
                                axis=1).reshape(dec_batch * seg, pool_w)
        o_pool = _pool(u_ext, pos_s, wp_bf, psc, tm=32 * seg, zero_first=False)
        o_pool = o_pool.reshape(dec_batch, seg, pool_w)[:, POOL_HALO:].reshape(dec_batch * dec_seq, pool_w)
        o_mem = _mem_sample(qm.reshape(dec_batch, dec_seq * MEM_HEADS, MEM_HD), cache_mem_k[l],
                            cache_mem_v[l]).reshape(dec_batch * dec_seq, mem_w)
        outs["sk"].append(k.reshape(dec_batch, dec_seq, FOX_HEADS, FOX_HD))
        outs["sv"].append(v.reshape(dec_batch, dec_seq, FOX_HEADS, FOX_HD))
        outs["sf"].append(logf.reshape(dec_batch, dec_seq, FOX_HEADS))
        outs["sp"].append(jnp.concatenate([state, u3], axis=1)[:, -POOL_STATE:])
        hs = _finish(hs, x_bf, o_fox, o_pool, o_mem, w, alpha=alpha, fox_transposed=False)

    st = lambda n: jnp.stack(outs[n])
    return (hp.reshape(batch, seq, d_model), hs.reshape(dec_batch, dec_seq, d_model),
            st("pk"), st("pv"), st("pf"), st("pp"), st("pmk"), st("pmv"),
            st("sk"), st("sv"), st("sf"), st("sp"))
```

```python
import functools

import numpy as np

import jax
import jax.numpy as jnp
from jax import lax
from jax.experimental import pallas as pl
from jax.experimental.pallas import tpu as pltpu

F32 = jnp.float32
BF16 = jnp.bfloat16

FOX_HEADS = 8
FOX_HD = 128
POOL_WINDOWS = (2, 4, 8, 16)
POOL_GW = 256
POOL_STATE = max(POOL_WINDOWS) - 1
POOL_HALO = 16
MEM_HEADS = 4
MEM_HD = 256
LN_EPS = 1e-5
MASK_VALUE = -1e30
LOG2E = 1.4426950408889634
FOX_EXT = 128
FOX_ONES = 16

V7X_VMEM_LIMIT_BYTES = 56 * 1024 * 1024


def _cparams(semantics):
    return pltpu.CompilerParams(dimension_semantics=semantics,
                                vmem_limit_bytes=V7X_VMEM_LIMIT_BYTES)


def _dot(a, b):
    return jnp.dot(a, b, preferred_element_type=F32)


def _dot_nt(a, b):
    return lax.dot_general(a, b, (((1,), (1,)), ((), ())), preferred_element_type=F32)


def _dot_tn(a, b):
    return lax.dot_general(a, b, (((0,), (0,)), ((), ())), preferred_element_type=F32)


def _split3(x):
    hi = x.astype(BF16)
    r1 = x - hi.astype(F32)
    mid = r1.astype(BF16)
    lo = (r1 - mid.astype(F32)).astype(BF16)
    return hi, mid, lo


def _split3_dot(x, t):
    hi, mid, lo = _split3(x)
    return _dot(hi, t) + _dot(mid, t) + _dot(lo, t)


def _layer_norm(z, g, b):
    mu = jnp.mean(z, axis=-1, keepdims=True)
    zc = z - mu
    var = jnp.mean(zc * zc, axis=-1, keepdims=True)
    return zc * lax.rsqrt(var + LN_EPS) * g + b


def _mm_body(*refs, epilogue, n_in, emit_x, w_rows):
    x_ref, w_ref = refs[0], refs[1]
    extras = refs[2:n_in]
    outs = list(refs[n_in:])
    x = x_ref[...].astype(BF16)
    if emit_x:
        outs.pop(0)[...] = x
    acc = (_dot_nt if w_rows else _dot)(x, w_ref[...].astype(BF16))
    res = epilogue(acc, *(e[...] for e in extras))
    for o, r in zip(outs, res):
        o[...] = r.astype(o.dtype)


def _mm(x, w, outs, epilogue, *, extras=(), tm=1024, tn=1024, row0=None, ncols=None, emit_x=False, name):
    m, k = x.shape
    n = w.shape[1] if ncols is None else ncols
    tm, tn = min(tm, m), min(tn, n)
    assert m % tm == 0 and n % tn == 0 and (n == tn or not emit_x)
    if row0 is None:
        w_spec = pl.BlockSpec((k, tn), lambda i, j: (0, j))
    else:
        assert row0 % 8 == 0 and tn % 8 == 0
        w_spec = pl.BlockSpec((pl.Element(tn), pl.Element(k)),
                              lambda i, j: (pl.multiple_of(row0 + j * tn, 8), 0))
    in_specs = [pl.BlockSpec((tm, k), lambda i, j: (i, 0)), w_spec]
    in_specs += [pl.BlockSpec(bs, im) for _, bs, im in extras]
    out_specs = [pl.BlockSpec((tn, tm), lambda i, j: (j, i)) if t else
                 pl.BlockSpec((tm, tn), lambda i, j: (i, j)) for _, t in outs]
    out_shape = [jax.ShapeDtypeStruct((n, m) if t else (m, n), dt) for dt, t in outs]
    if emit_x:
        out_specs.insert(0, pl.BlockSpec((tm, k), lambda i, j: (i, 0)))
        out_shape.insert(0, jax.ShapeDtypeStruct((m, k), BF16))
    return pl.pallas_call(
        functools.partial(_mm_body, epilogue=epilogue, n_in=2 + len(extras), emit_x=emit_x,
                          w_rows=row0 is not None),
        grid=(m // tm, n // tn),
        in_specs=in_specs, out_specs=out_specs, out_shape=out_shape,
        compiler_params=_cparams(("parallel", "arbitrary")),
        name=name,
    )(x, w, *(a for a, _, _ in extras))


def _ep_identity(acc):
    return (acc,)


def _ep_dup(acc):
    return (acc, acc)


def _ep_scaled_t(acc, *, scale):
    return ((acc * scale).T,)


def _ep_with_t(acc):
    return (acc, acc.T)


def _ep_log_sigmoid(acc, bias):
    z = acc + bias
    return (jnp.minimum(z, 0.0) - jnp.log1p(jnp.exp(-jnp.abs(z))),)


def _cumsum_body(x_ref, tri_ref, o_ref, *, chunk, scale):
    n = x_ref.shape[1]
    carry = jnp.zeros((x_ref.shape[0], 1), F32)
    for c in range(n // chunk):
        cs = _split3_dot(x_ref[:, c * chunk:(c + 1) * chunk], tri_ref[...]) + carry
        o_ref[:, c * chunk:(c + 1) * chunk] = cs * scale
        carry = cs[:, chunk - 1:chunk]


def _cumsum_lanes(x, *, scale, chunk=512):
    rows, n = x.shape
    tri = jnp.asarray(np.triu(np.ones((chunk, chunk), np.float32)), BF16)
    return pl.pallas_call(
        functools.partial(_cumsum_body, chunk=chunk, scale=scale),
        out_shape=jax.ShapeDtypeStruct((rows, n), F32),
        name="logf_cumsum",
    )(x, tri)


def _bias_rows(c, ones_first, n):
    sub = lax.broadcasted_iota(jnp.int32, (n, c.shape[1]), 0)
    c0, o0 = (3, 0) if ones_first else (0, 3)
    rows = jnp.where((sub >= o0) & (sub < o0 + 3), 1.0, 0.0).astype(BF16)
    for t, piece in enumerate(_split3(c)):
        rows = jnp.where(sub == c0 + t, piece, rows)
    return rows


def _fox_prompt_body(qT_ref, k_ref, vT_ref, c_ref, oT_ref, kext, s0, s1, m_scr, acc_scr, *, tq):
    i = pl.program_id(1)
    tk = tq

    @pl.when(i == 0)
    def _():
        eye = (lax.broadcasted_iota(jnp.int32, (FOX_ONES, FOX_EXT), 0)
               == lax.broadcasted_iota(jnp.int32, (FOX_ONES, FOX_EXT), 1)).astype(BF16)
        for c in range(k_ref.shape[0] // tk):
            rows = _bias_rows(-c_ref[0, :, c * tk:(c + 1) * tk], True, FOX_ONES)
            kext[c * tk:(c + 1) * tk, :] = _dot_tn(rows, eye).astype(BF16)

    last = pl.multiple_of((i + 1) * tq - 128, 128)
    cq = c_ref[0, :, pl.ds(last, 128)][:, 127:128]
    qa = jnp.concatenate([qT_ref[...], jnp.broadcast_to(_bias_rows(cq, False, FOX_EXT), (FOX_EXT, tq))],
                         axis=0)
    ones = jnp.ones((FOX_ONES, tk), BF16)

    def scores(j, dst):
        ks = pl.multiple_of(j * tk, tk)
        ka = jnp.concatenate([k_ref[pl.ds(ks, tk), :], kext[pl.ds(ks, tk), :]], axis=1)
        dst[...] = _dot(ka, qa)

    def absorb(j, src, diag):
        ks = pl.multiple_of(j * tk, tk)
        sT = src[...]
        if diag:
            key = lax.broadcasted_iota(jnp.int32, (tk, tq), 0)
            qry = lax.broadcasted_iota(jnp.int32, (tk, tq), 1)
            sT = jnp.where(key <= qry, sT, MASK_VALUE)
        m = m_scr[...]
        m_new = jnp.maximum(m, jnp.max(sT, axis=0, keepdims=True))
        p = jnp.exp2(sT - m_new).astype(BF16)
        m_scr[...] = m_new
        va = jnp.concatenate([vT_ref[:, pl.ds(ks, tk)], ones], axis=0)
        acc_scr[...] = jnp.exp2(m - m_new) * acc_scr[...] + _dot(va, p)

    def finish(src):
        absorb(i, src, True)
        oT_ref[...] = (acc_scr[:FOX_HD, :] / acc_scr[FOX_HD:FOX_HD + 1, :]).astype(oT_ref.dtype)

    m_scr[...] = jnp.full(m_scr.shape, MASK_VALUE, F32)
    acc_scr[...] = jnp.zeros(acc_scr.shape, F32)
    scores(0, s0)

    def pair(t, carry):
        j = 2 * t
        scores(j + 1, s1)
        absorb(j, s0, False)
        scores(j + 2, s0)
        absorb(j + 1, s1, False)
        return carry

    lax.fori_loop(0, i // 2, pair, 0)

    @pl.when(i % 2 == 0)
    def _():
        finish(s0)

    @pl.when(i % 2 == 1)
    def _():
        scores(i, s1)
        absorb(i - 1, s0, False)
        finish(s1)


def _fox_prompt(q2T, k, vT, c2_rows, *, tq=1024):
    h, s = c2_rows.shape
    return pl.pallas_call(
        functools.partial(_fox_prompt_body, tq=tq),
        grid=(h, s // tq),
        in_specs=[pl.BlockSpec((FOX_HD, tq), lambda g, i: (g, i)),
                  pl.BlockSpec((s, FOX_HD), lambda g, i: (0, g)),
                  pl.BlockSpec((FOX_HD, s), lambda g, i: (g, 0)),
                  pl.BlockSpec((1, 1, s), lambda g, i: (g, 0, 0))],
        out_specs=pl.BlockSpec((FOX_HD, tq), lambda g, i: (g, i)),
        out_shape=jax.ShapeDtypeStruct((h * FOX_HD, s), BF16),
        scratch_shapes=[pltpu.VMEM((s, FOX_EXT), BF16), pltpu.VMEM((tq, tq), F32), pltpu.VMEM((tq, tq), F32),
                        pltpu.VMEM((1, tq), F32), pltpu.VMEM((FOX_HD + FOX_ONES, tq), F32)],
        compiler_params=_cparams(("parallel", "arbitrary")),
        name="fox_prompt_attention",
    )(q2T, k, vT, c2_rows[:, None, :])


def _fox_sample_body(pt_ref, q_ref, kn_ref, vn_ref, lfn_ref, msk_ref, t4_ref, t5_ref, *rest,
                     n_pages, scale):
    del pt_ref
    k_pages = rest[:n_pages]
    v_pages = rest[n_pages:2 * n_pages]
    lf_pages = rest[2 * n_pages:3 * n_pages]
    o_ref, s_scr = rest[3 * n_pages], rest[3 * n_pages + 1]
    rows = q_ref.shape[1]
    pk = k_pages[0].shape[1]
    h = FOX_HEADS

    q = q_ref[0]
    r = lfn_ref[0]
    bq = _split3_dot(jnp.broadcast_to(r, (rows, rows)) * msk_ref[...], t4_ref[...])
    cn = _split3_dot(jnp.broadcast_to(r, (8, rows)), t5_ref[...])[0:1]

    row_h = lax.broadcasted_iota(jnp.int32, (rows, 128), 0) & (h - 1)
    lane_h = lax.broadcasted_iota(jnp.int32, (rows, 128), 1) & (h - 1)
    same_head = row_h == lane_h
    mbq = jnp.where(same_head, bq, MASK_VALUE)
    mbq_page = jnp.concatenate([mbq] * (pk // 128), axis=1)

    kn = kn_ref[0].astype(BF16)
    row_t = lax.broadcasted_iota(jnp.int32, (rows, rows), 0) >> 3
    lane_t = lax.broadcasted_iota(jnp.int32, (rows, rows), 1) >> 3
    ok_new = same_head[:, :rows] & (lane_t <= row_t)
    s_new = jnp.where(ok_new, _dot_nt(q, kn) * scale + (bq[:, :rows] - cn), MASK_VALUE)
    m = jnp.max(s_new, axis=-1, keepdims=True)

    later = jnp.zeros((1, 128), F32)
    for p in reversed(range(n_pages)):
        lf = lf_pages[p][0]
        tot = lf[:, pk:]
        d = jnp.concatenate([later + tot] * (pk // 128), axis=1) - lf[:, :pk]
        later = later + tot
        s = _dot_nt(q, k_pages[p][0].astype(BF16)) * scale + (mbq_page + d)
        s_scr[:, p * pk:(p + 1) * pk] = s
        m = jnp.maximum(m, jnp.max(s, axis=-1, keepdims=True))

    p_new = jnp.exp(s_new - m)
    l = jnp.sum(p_new, axis=-1, keepdims=True)
    acc = _dot(p_new.astype(BF16), vn_ref[0].astype(BF16))
    for p in range(n_pages):
        pr = jnp.exp(s_scr[:, p * pk:(p + 1) * pk] - m)
        l = l + jnp.sum(pr, axis=-1, keepdims=True)
        acc = acc + _dot(pr.astype(BF16), v_pages[p][0].astype(BF16))
    o_ref[0] = (acc / l).astype(o_ref.dtype)


def _fox_sample(q, k_new, v_new, lf_new, cache_k, cache_v, lf_cache, page_table):
    b, rows, d = q.shape
    n_pages = page_table.shape[1]
    pk = cache_k.shape[1]
    assert rows == 64 and FOX_HEADS == 8 and pk % 128 == 0

    rr = np.arange(rows)
    tt, hh = rr // FOX_HEADS, rr % FOX_HEADS
    lane = np.arange(128)
    msk = jnp.asarray((tt[None, :] <= tt[:, None]).astype(np.float32))
    t4 = jnp.asarray((hh[:, None] == (lane % FOX_HEADS)[None, :]).astype(np.float32), BF16)
    t5 = jnp.asarray(((hh[:, None] == hh[None, :]) & (tt[:, None] <= tt[None, :])).astype(np.float32), BF16)

    def page_map(p):
        return lambda i, pt: (pt[i * n_pages + p], 0, 0)

    per_b = lambda i, pt: (i, 0, 0)
    const = lambda i, pt: (0, 0)
    in_specs = [pl.BlockSpec((1, rows, d), per_b),
                pl.BlockSpec((1, rows, d), per_b),
                pl.BlockSpec((1, rows, d), per_b),
                pl.BlockSpec((1, 1, rows), per_b),
                pl.BlockSpec((rows, rows), const),
                pl.BlockSpec((rows, 128), const),
                pl.BlockSpec((rows, rows), const)]
    in_specs += [pl.BlockSpec((1, pk, d), page_map(p)) for p in range(n_pages)]
    in_specs += [pl.BlockSpec((1, pk, d), page_map(p)) for p in range(n_pages)]
    in_specs += [pl.BlockSpec((1, 1, pk + 128), page_map(p)) for p in range(n_pages)]
    grid_spec = pltpu.PrefetchScalarGridSpec(
        num_scalar_prefetch=1, grid=(b,), in_specs=in_specs,
        out_specs=pl.BlockSpec((1, rows, d), per_b),
        scratch_shapes=[pltpu.VMEM((rows, n_pages * pk), F32)])
    return pl.pallas_call(
        functools.partial(_fox_sample_body, n_pages=n_pages, scale=FOX_HD ** -0.5),
        grid_spec=grid_spec,
        out_shape=jax.ShapeDtypeStruct((b, rows, d), BF16),
        compiler_params=_cparams(("arbitrary",)),
        name="fox_sample_attention",
    )(page_table.reshape(-1), q, k_new, v_new, lf_new, msk, t4, t5,
      *([cache_k] * n_pages), *([cache_v] * n_pages), *([lf_cache] * n_pages))


def _page_logf_sums(cache_logf):
    p, page, h = cache_logf.shape
    n = page * h
    pos, head = np.arange(n) // h, np.arange(n) % h
    prefix = (head[:, None] == head[None, :]) & (pos[:, None] <= pos[None, :])
    total = head[:, None] == (np.arange(128) % h)[None, :]
    t = jnp.asarray(np.concatenate([prefix, total], axis=1).astype(np.float32), BF16)
    tm = next(c for c in (512, 256, 128, 64, 32, 16, 8) if p % c == 0)

    def body(x_ref, t_ref, o_ref):
        o_ref[...] = _split3_dot(x_ref[...], t_ref[...])

    out = pl.pallas_call(
        body, grid=(p // tm,),
        in_specs=[pl.BlockSpec((tm, n), lambda i: (i, 0)),
                  pl.BlockSpec((n, n + 128), lambda i: (0, 0))],
        out_specs=pl.BlockSpec((tm, n + 128), lambda i: (i, 0)),
        out_shape=jax.ShapeDtypeStruct((p, n + 128), F32),
        compiler_params=_cparams(("parallel",)),
        name="page_logf_sums",
    )(cache_logf.reshape(p, n), t)
    return out.reshape(p, 1, n + 128)


def _pool_body(u_ref, halo_ref, pos_ref, wp_ref, sc_ref, o_ref, *, zero_first):
    u = u_ref[...]
    halo = halo_ref[...]
    if zero_first:
        halo = jnp.where(pl.program_id(0) == 0, 0.0, halo)
    pos1 = pos_ref[...] + 1.0
    outs = []
    for g, w in enumerate(POOL_WINDOWS):
        sl = slice(g * POOL_GW, (g + 1) * POOL_GW)
        s = jnp.concatenate([halo[:, sl], u[:, sl]], axis=0)
        sh = 1
        while sh < w:
            s = s + pltpu.roll(s, sh, 0)
            sh *= 2
        d = s[POOL_HALO:] / jnp.minimum(float(w), pos1) - u[:, sl]
        outs.append(_dot(d.astype(BF16), wp_ref[g]))
    o_ref[...] = (jnp.concatenate(outs, axis=1) * sc_ref[...]).astype(o_ref.dtype)


def _pool(u, pos, w_pool, pool_scale, *, tm, zero_first):
    assert all(w & (w - 1) == 0 for w in POOL_WINDOWS) and tm % POOL_HALO == 0
    r, width = u.shape
    hb = tm // POOL_HALO
    return pl.pallas_call(
        functools.partial(_pool_body, zero_first=zero_first),
        grid=(r // tm,),
        in_specs=[pl.BlockSpec((tm, width), lambda i: (i, 0)),
                  pl.BlockSpec((POOL_HALO, width), lambda i: (jnp.maximum(i * hb - 1, 0), 0)),
                  pl.BlockSpec((tm, 1), lambda i: (i, 0)),
                  pl.BlockSpec(w_pool.shape, lambda i: (0, 0, 0)),
                  pl.BlockSpec((1, width), lambda i: (0, 0))],
        out_specs=pl.BlockSpec((tm, width), lambda i: (i, 0)),
        out_shape=jax.ShapeDtypeStruct((r, width), BF16),
        compiler_params=_cparams(("arbitrary",)),
        name="pool_mixer",
    )(u, u, pos, w_pool, pool_scale)


def _mem_prompt_body(q_ref, k_ref, v_ref, o_ref, *, scale):
    for h in range(MEM_HEADS):
        sl = slice(h * MEM_HD, (h + 1) * MEM_HD)
        s = _dot_nt(q_ref[:, sl], k_ref[:, sl]) * scale
        p = jnp.exp(s - jnp.max(s, axis=-1, keepdims=True))
        l = jnp.sum(p, axis=-1, keepdims=True)
        o_ref[:, sl] = (_dot(p.astype(BF16), v_ref[:, sl]) / l).astype(o_ref.dtype)


def _mem_prompt(qm, mk, mv, *, tq=1024):
    s, width = qm.shape
    m = mk.shape[0]
    return pl.pallas_call(
        functools.partial(_mem_prompt_body, scale=MEM_HD ** -0.5),
        grid=(s // tq,),
        in_specs=[pl.BlockSpec((tq, width), lambda i: (i, 0)),
                  pl.BlockSpec((m, width), lambda i: (0, 0)),
                  pl.BlockSpec((m, width), lambda i: (0, 0))],
        out_specs=pl.BlockSpec((tq, width), lambda i: (i, 0)),
        out_shape=jax.ShapeDtypeStruct((s, width), BF16),
        compiler_params=_cparams(("parallel",)),
        name="mem_prompt_attention",
    )(qm, mk, mv)


def _mem_sample_body(q_ref, k_ref, v_ref, o_ref, *, bb, scale):
    rows = q_ref.shape[1]
    m, h, d = k_ref.shape[1:]
    row_h = lax.broadcasted_iota(jnp.int32, (rows, m * h), 0) & (h - 1)
    lane_h = lax.broadcasted_iota(jnp.int32, (rows, m * h), 1) & (h - 1)
    own = row_h == lane_h
    for b in range(bb):
        k2 = k_ref[b].reshape(m * h, d).astype(BF16)
        v2 = v_ref[b].reshape(m * h, d).astype(BF16)
        s = jnp.where(own, _dot_nt(q_ref[b], k2) * scale, MASK_VALUE)
        p = jnp.exp(s - jnp.max(s, axis=-1, keepdims=True))
        l = jnp.sum(p, axis=-1, keepdims=True)
        o_ref[b] = (_dot(p.astype(BF16), v2) / l).astype(o_ref.dtype)


def _mem_sample(qm, ck, cv, *, bb=4):
    b, rows, d = qm.shape
    _, m, h, _ = ck.shape
    assert h & (h - 1) == 0 and b % bb == 0
    return pl.pallas_call(
        functools.partial(_mem_sample_body, bb=bb, scale=MEM_HD ** -0.5),
        grid=(b // bb,),
        in_specs=[pl.BlockSpec((bb, rows, d), lambda i: (i, 0, 0)),
                  pl.BlockSpec((bb, m, h, d), lambda i: (i, 0, 0, 0)),
                  pl.BlockSpec((bb, m, h, d), lambda i: (i, 0, 0, 0))],
        out_specs=pl.BlockSpec((bb, rows, d), lambda i: (i, 0, 0)),
        out_shape=jax.ShapeDtypeStruct((b, rows, d), BF16),
        compiler_params=_cparams(("parallel",)),
        name="mem_sample_attention",
    )(qm, ck, cv)


def _merge_body(x_ref, of_ref, op_ref, om_ref, wf_ref, wp_ref, wm_ref, g0_ref, g1_ref, g2_ref,
                o_ref, *, fox_transposed):
    x = x_ref[...]
    fox = (_dot_tn if fox_transposed else _dot)(of_ref[...], wf_ref[...])
    merged = (jax.nn.sigmoid(_dot_nt(x, g0_ref[...])) * fox
              + jax.nn.sigmoid(_dot_nt(x, g1_ref[...])) * _dot(op_ref[...], wp_ref[...])
              + jax.nn.sigmoid(_dot_nt(x, g2_ref[...])) * _dot(om_ref[...], wm_ref[...]))
    o_ref[...] = merged.astype(o_ref.dtype)


def _merge(x, o_fox, o_pool, o_mem, w_f, w_p, w_m, w_gates_t, *, fox_transposed, tm=512, tn=512):
    m, d = x.shape
    kb = o_pool.shape[1]
    n = w_f.shape[1]
    tm = min(tm, m)
    nb = n // tn
    act = pl.BlockSpec((tm, kb), lambda i, j: (i, 0))
    act_fox = pl.BlockSpec((kb, tm), lambda i, j: (0, i)) if fox_transposed else act
    wgt = pl.BlockSpec((kb, tn), lambda i, j: (0, j))
    gate = lambda g: pl.BlockSpec((tn, d), lambda i, j: (g * nb + j, 0))
    return pl.pallas_call(
        functools.partial(_merge_body, fox_transposed=fox_transposed), grid=(m // tm, nb),
        in_specs=[pl.BlockSpec((tm, d), lambda i, j: (i, 0)), act_fox, act, act, wgt, wgt, wgt,
                  gate(0), gate(1), gate(2)],
        out_specs=pl.BlockSpec((tm, tn), lambda i, j: (i, j)),
        out_shape=jax.ShapeDtypeStruct((m, n), BF16),
        compiler_params=_cparams(("parallel", "arbitrary")),
        name="branch_merge",
    )(x, o_fox, o_pool, o_mem, w_f, w_p, w_m, w_gates_t, w_gates_t, w_gates_t)


def _ep_residual_ln(acc, x, g, b, *, alpha):
    return (_layer_norm(alpha * x + acc, g, b),)


def _ffn_body(h_ref, wu_ref, wd_ref, g_ref, b_ref, o_ref, hb_ref, *, alpha):
    k = pl.program_id(1)

    @pl.when(k == 0)
    def _():
        hb_ref[...] = h_ref[...].astype(BF16)
        o_ref[...] = jnp.zeros_like(o_ref)

    a = jnp.square(jnp.maximum(_dot(hb_ref[...], wu_ref[...]), 0.0))
    o_ref[...] += _dot(a.astype(BF16), wd_ref[...])

    @pl.when(k == pl.num_programs(1) - 1)
    def _():
        o_ref[...] = _layer_norm(alpha * h_ref[...] + o_ref[...], g_ref[...], b_ref[...])


def _ffn(h, w_up, w_down, g, b, *, alpha, tm=1024, tf=512):
    m, d = h.shape
    dff = w_up.shape[1]
    tm = min(tm, m)
    return pl.pallas_call(
        functools.partial(_ffn_body, alpha=alpha),
        grid=(m // tm, dff // tf),
        in_specs=[pl.BlockSpec((tm, d), lambda i, k: (i, 0)),
                  pl.BlockSpec((d, tf), lambda i, k: (0, k)),
                  pl.BlockSpec((tf, d), lambda i, k: (k, 0)),
                  pl.BlockSpec((1, d), lambda i, k: (0, 0)),
                  pl.BlockSpec((1, d), lambda i, k: (0, 0))],
        out_specs=pl.BlockSpec((tm, d), lambda i, k: (i, 0)),
        out_shape=jax.ShapeDtypeStruct((m, d), F32),
        scratch_shapes=[pltpu.VMEM((tm, d), BF16)],
        compiler_params=_cparams(("parallel", "arbitrary")),
        name="ffn_ln",
    )(h, w_up, w_down, g, b)


def _finish(x, x_bf, o_fox, o_pool, o_mem, w, *, alpha, fox_transposed):
    d = x.shape[1]
    merged = _merge(x_bf, o_fox, o_pool, o_mem, w["br_fox"], w["br_pool"], w["br_mem"],
                    w["gates_t"], fox_transposed=fox_transposed)
    row = lambda i, j: (0, 0)
    h, = _mm(merged, w["out"], [(F32, False)], functools.partial(_ep_residual_ln, alpha=alpha),
             extras=[(x, (min(512, x.shape[0]), d), lambda i, j: (i, 0)),
                     (w["ln1_g"], (1, d), row), (w["ln1_b"], (1, d), row)],
             tm=512, tn=d, name="out_proj_ln")
    return _ffn(h, w["up"], w["down"], w["ln2_g"], w["ln2_b"], alpha=alpha)


def kernel(x_prompt, x_sample, mem_prompt, cache_fox_k, cache_fox_v, cache_fox_logf, state_pool,
           cache_mem_k, cache_mem_v, page_table, w_in, b_fgate, w_pool, pool_scale, w_br_fox,
           w_br_pool, w_br_mem, w_mem_kv, w_out, ln1_g, ln1_b, w_up, w_down, ln2_g, ln2_b):
    batch, seq, d_model = x_prompt.shape
    dec_batch, dec_seq, _ = x_sample.shape
    depth = w_in.shape[0]
    n_phys, page_size = cache_fox_k.shape[1], cache_fox_k.shape[2]
    n_pages = page_table.shape[1]
    past = n_pages * page_size
    fox_w = FOX_HEADS * FOX_HD
    pool_w = len(POOL_WINDOWS) * POOL_GW
    mem_w = MEM_HEADS * MEM_HD
    mem_tokens = mem_prompt.shape[1]
    assert batch == 1 and dec_seq == FOX_HEADS == 8
    assert fox_w == pool_w == mem_w
    alpha = (2 * depth) ** 0.25
    nat, tr = False, True

    hp = x_prompt.reshape(seq, d_model)
    hs = x_sample.reshape(dec_batch * dec_seq, d_model)
    seg = POOL_HALO + dec_seq
    pos_p = jnp.arange(seq, dtype=F32)[:, None]
    pos_s = jnp.tile(jnp.concatenate([jnp.full((POOL_HALO,), 1e9, F32),
                                      past + jnp.arange(dec_seq, dtype=F32)]), dec_batch)[:, None]

    outs = {n: [] for n in ("pk", "pv", "pf", "pp", "pmk", "pmv", "sk", "sv", "sf", "sp")}
    for l in range(depth):
        w_t = jnp.swapaxes(w_in[l], 0, 1)
        r_q, r_k, r_v, r_f = 0, fox_w, 2 * fox_w, 3 * fox_w
        r_u = r_f + FOX_HEADS
        r_qm = r_u + pool_w
        r_g = r_qm + mem_w
        b_f = jnp.pad(b_fgate[l], (0, 128 - FOX_HEADS))[None, :]
        w = dict(gates_t=w_t[r_g:].astype(BF16),
                 br_fox=w_br_fox[l].astype(BF16), br_pool=w_br_pool[l].astype(BF16),
                 br_mem=w_br_mem[l].astype(BF16), out=w_out[l].astype(BF16),
                 up=w_up[l].astype(BF16), down=w_down[l].astype(BF16),
                 ln1_g=ln1_g[l][None, :], ln1_b=ln1_b[l][None, :],
                 ln2_g=ln2_g[l][None, :], ln2_b=ln2_b[l][None, :])
        wp_bf = w_pool[l].astype(BF16)
        psc = pool_scale[l][None, :]
        logf_of = functools.partial(_mm, w=w_t, outs=[(F32, nat)], epilogue=_ep_log_sigmoid,
                                    extras=[(b_f, (1, 128), lambda i, j: (0, 0))], tn=128, row0=r_f,
                                    ncols=128, name="proj_logf")

        x_bf, q2T = _mm(hp, w_t, [(BF16, tr)], functools.partial(_ep_scaled_t, scale=FOX_HD ** -0.5 * LOG2E),
                        row0=r_q, ncols=fox_w, tm=512, emit_x=True, name="proj_q")
        k, k_bf = _mm(x_bf, w_t, [(F32, nat), (BF16, nat)], _ep_dup, row0=r_k, ncols=fox_w, name="proj_k")
        v, vT = _mm(x_bf, w_t, [(F32, nat), (BF16, tr)], _ep_with_t, row0=r_v, ncols=fox_w, name="proj_v")
        u, = _mm(x_bf, w_t, [(F32, nat)], _ep_identity, row0=r_u, ncols=pool_w, name="proj_u")
        qm, = _mm(x_bf, w_t, [(BF16, nat)], _ep_identity, row0=r_qm, ncols=mem_w, name="proj_qm")
        logf = logf_of(x_bf)[0][:, :FOX_HEADS]
        c2_rows = _cumsum_lanes(logf.T, scale=LOG2E)
        o_foxT = _fox_prompt(q2T, k_bf, vT, c2_rows)
        o_pool = _pool(u, pos_p, wp_bf, psc, tm=1024, zero_first=True)
        mkv, mkv_bf = _mm(mem_prompt.reshape(mem_tokens, d_model), w_mem_kv[l], [(F32, nat), (BF16, nat)],
                          _ep_dup, name="mem_kv")
        o_mem = _mem_prompt(qm, mkv_bf[:, :mem_w], mkv_bf[:, mem_w:])
        outs["pk"].append(k.reshape(batch, seq, FOX_HEADS, FOX_HD))
        outs["pv"].append(v.reshape(batch, seq, FOX_HEADS, FOX_HD))
        outs["pf"].append(logf.reshape(batch, seq, FOX_HEADS))
        outs["pp"].append(u[-POOL_STATE:].reshape(batch, POOL_STATE, pool_w))
        outs["pmk"].append(mkv[:, :mem_w].reshape(batch, mem_tokens, MEM_HEADS, MEM_HD))
        outs["pmv"].append(mkv[:, mem_w:].reshape(batch, mem_tokens, MEM_HEADS, MEM_HD))
        hp = _finish(hp, x_bf, o_foxT, o_pool, o_mem, w, alpha=alpha, fox_transposed=True)

        x_bf, q = _mm(hs, w_t, [(BF16, nat)], _ep_identity, row0=r_q, ncols=fox_w, tm=512, emit_x=True,
                      name="proj_q")
        k, = _mm(x_bf, w_t, [(F32, nat)], _ep_identity, row0=r_k, ncols=fox_w, name="proj_k")
        v, = _mm(x_bf, w_t, [(F32, nat)], _ep_identity, row0=r_v, ncols=fox_w, name="proj_v")
        u, = _mm(x_bf, w_t, [(F32, nat)], _ep_identity, row0=r_u, ncols=pool_w, name="proj_u")
        qm, = _mm(x_bf, w_t, [(BF16, nat)], _ep_identity, row0=r_qm, ncols=mem_w, name="proj_qm")
        logf = logf_of(x_bf)[0][:, :FOX_HEADS]
        rows = dec_seq * FOX_HEADS
        o_fox = _fox_sample(
            q.reshape(dec_batch, rows, FOX_HD), k.reshape(dec_batch, rows, FOX_HD),
            v.reshape(dec_batch, rows, FOX_HD), logf.reshape(dec_batch, 1, rows),
            cache_fox_k[l].reshape(n_phys, page_size * FOX_HEADS, FOX_HD),
            cache_fox_v[l].reshape(n_phys, page_size * FOX_HEADS, FOX_HD),
            _page_logf_sums(cache_fox_logf[l]), page_table).reshape(dec_batch * dec_seq, fox_w)
        u3 = u.reshape(dec_batch, dec_seq, pool_w)
        state = state_pool[l].astype(F32)
        u_ext = jnp.concatenate([jnp.zeros((dec_batch, POOL_HALO - POOL_STATE, pool_w), F32), state, u3],
                                axis=1).reshape(dec_batch * seg, pool_w)
        o_pool = _pool(u_ext, pos_s, wp_bf, psc, tm=32 * seg, zero_first=False)
        o_pool = o_pool.reshape(dec_batch, seg, pool_w)[:, POOL_HALO:].reshape(dec_batch * dec_seq, pool_w)
        o_mem = _mem_sample(qm.reshape(dec_batch, dec_seq * MEM_HEADS, MEM_HD), cache_mem_k[l],
                            cache_mem_v[l]).reshape(dec_batch * dec_seq, mem_w)
        outs["sk"].append(k.reshape(dec_batch, dec_seq, FOX_HEADS, FOX_HD))
        outs["sv"].append(v.reshape(dec_batch, dec_seq, FOX_HEADS, FOX_HD))
        outs["sf"].append(logf.reshape(dec_batch, dec_seq, FOX_HEADS))
        outs["sp"].append(jnp.concatenate([state, u3], axis=1)[:, -POOL_STATE:])
        hs = _finish(hs, x_bf, o_fox, o_pool, o_mem, w, alpha=alpha, fox_transposed=False)

    st = lambda n: jnp.stack(outs[n])
    return (hp.reshape(batch, seq, d_model), hs.reshape(dec_batch, dec_seq, d_model),
            st("pk"), st("pv"), st("pf"), st("pp"), st("pmk"), st("pmv"),
            st("sk"), st("sv"), st("sf"), st("sp"))
```

```python
import functools

import numpy as np

import jax
import jax.numpy as jnp
from jax import lax
from jax.experimental import pallas as pl
from jax.experimental.pallas import tpu as pltpu

F32 = jnp.float32
BF16 = jnp.bfloat16

FOX_HEADS = 8
FOX_HD = 128
POOL_WINDOWS = (2, 4, 8, 16)
POOL_GW = 256
POOL_STATE = max(POOL_WINDOWS) - 1
POOL_HALO = 16
MEM_HEADS = 4
MEM_HD = 256
LN_EPS = 1e-5
MASK_VALUE = -1e30
LOG2E = 1.4426950408889634
FOX_EXT = 128
FOX_ONES = 16

V7X_VMEM_LIMIT_BYTES = 56 * 1024 * 1024


def _cparams(semantics):
    return pltpu.CompilerParams(dimension_semantics=semantics,
                                vmem_limit_bytes=V7X_VMEM_LIMIT_BYTES)


def _dot(a, b):
    return jnp.dot(a, b, preferred_element_type=F32)


def _dot_nt(a, b):
    return lax.dot_general(a, b, (((1,), (1,)), ((), ())), preferred_element_type=F32)


def _dot_tn(a, b):
    return lax.dot_general(a, b, (((0,), (0,)), ((), ())), preferred_element_type=F32)


def _split3(x):
    hi = x.astype(BF16)
    r1 = x - hi.astype(F32)
    mid = r1.astype(BF16)
    lo = (r1 - mid.astype(F32)).astype(BF16)
    return hi, mid, lo


def _split3_dot(x, t):
    hi, mid, lo = _split3(x)
    return _dot(hi, t) + _dot(mid, t) + _dot(lo, t)


def _layer_norm(z, g, b):
    mu = jnp.mean(z, axis=-1, keepdims=True)
    zc = z - mu
    var = jnp.mean(zc * zc, axis=-1, keepdims=True)
    return zc * lax.rsqrt(var + LN_EPS) * g + b


def _mm_body(*refs, epilogue, n_in, emit_x, w_rows):
    x_ref, w_ref = refs[0], refs[1]
    extras = refs[2:n_in]
    outs = list(refs[n_in:])
    x = x_ref[...].astype(BF16)
    if emit_x:
        outs.pop(0)[...] = x
    acc = (_dot_nt if w_rows else _dot)(x, w_ref[...].astype(BF16))
    res = epilogue(acc, *(e[...] for e in extras))
    for o, r in zip(outs, res):
        o[...] = r.astype(o.dtype)


def _mm(x, w, outs, epilogue, *, extras=(), tm=1024, tn=1024, row0=None, ncols=None, emit_x=False, name):
    m, k = x.shape
    n = w.shape[1] if ncols is None else ncols
    tm, tn = min(tm, m), min(tn, n)
    assert m % tm == 0 and n % tn == 0 and (n == tn or not emit_x)
    if row0 is None:
        w_spec = pl.BlockSpec((k, tn), lambda i, j: (0, j))
    else:
        assert row0 % 8 == 0 and tn % 8 == 0
        w_spec = pl.BlockSpec((pl.Element(tn), pl.Element(k)),
                              lambda i, j: (pl.multiple_of(row0 + j * tn, 8), 0))
    in_specs = [pl.BlockSpec((tm, k), lambda i, j: (i, 0)), w_spec]
    in_specs += [pl.BlockSpec(bs, im) for _, bs, im in extras]
    out_specs = [pl.BlockSpec((tn, tm), lambda i, j: (j, i)) if t else
                 pl.BlockSpec((tm, tn), lambda i, j: (i, j)) for _, t in outs]
    out_shape = [jax.ShapeDtypeStruct((n, m) if t else (m, n), dt) for dt, t in outs]
    if emit_x:
        out_specs.insert(0, pl.BlockSpec((tm, k), lambda i, j: (i, 0)))
        out_shape.insert(0, jax.ShapeDtypeStruct((m, k), BF16))
    return pl.pallas_call(
        functools.partial(_mm_body, epilogue=epilogue, n_in=2 + len(extras), emit_x=emit_x,
                          w_rows=row0 is not None),
        grid=(m // tm, n // tn),
        in_specs=in_specs, out_specs=out_specs, out_shape=out_shape,
        compiler_params=_cparams(("parallel", "arbitrary")),
        name=name,
    )(x, w, *(a for a, _, _ in extras))


def _ep_identity(acc):
    return (acc,)


def _ep_dup(acc):
    return (acc, acc)


def _ep_scaled(acc, *, scale):
    return (acc * scale,)


def _ep_scaled_t(acc, *, scale):
    return ((acc * scale).T,)


def _ep_with_t(acc):
    return (acc, acc.T)


def _ep_log_sigmoid(acc, bias):
    z = acc + bias
    return (jnp.minimum(z, 0.0) - jnp.log1p(jnp.exp(-jnp.abs(z))),)


def _cumsum_body(x_ref, tri_ref, o_ref, *, chunk, scale):
    n = x_ref.shape[1]
    carry = jnp.zeros((x_ref.shape[0], 1), F32)
    for c in range(n // chunk):
        cs = _split3_dot(x_ref[:, c * chunk:(c + 1) * chunk], tri_ref[...]) + carry
        o_ref[:, c * chunk:(c + 1) * chunk] = cs * scale
        carry = cs[:, chunk - 1:chunk]


def _cumsum_lanes(x, *, scale, chunk=512):
    rows, n = x.shape
    tri = jnp.asarray(np.triu(np.ones((chunk, chunk), np.float32)), BF16)
    return pl.pallas_call(
        functools.partial(_cumsum_body, chunk=chunk, scale=scale),
        out_shape=jax.ShapeDtypeStruct((rows, n), F32),
        name="logf_cumsum",
    )(x, tri)


def _bias_rows(c, ones_first, n):
    sub = lax.broadcasted_iota(jnp.int32, (n, c.shape[1]), 0)
    c0, o0 = (3, 0) if ones_first else (0, 3)
    rows = jnp.where((sub >= o0) & (sub < o0 + 3), 1.0, 0.0).astype(BF16)
    for t, piece in enumerate(_split3(c)):
        rows = jnp.where(sub == c0 + t, piece, rows)
    return rows


def _fox_prompt_body(qT_ref, k_ref, vT_ref, c_ref, oT_ref, kext, s0, s1, m_scr, acc_scr, *, tq):
    i = pl.program_id(1)
    tk = tq

    @pl.when(i == 0)
    def _():
        eye = (lax.broadcasted_iota(jnp.int32, (FOX_ONES, FOX_EXT), 0)
               == lax.broadcasted_iota(jnp.int32, (FOX_ONES, FOX_EXT), 1)).astype(BF16)
        for c in range(k_ref.shape[0] // tk):
            rows = _bias_rows(-c_ref[0, :, c * tk:(c + 1) * tk], True, FOX_ONES)
            kext[c * tk:(c + 1) * tk, :] = _dot_tn(rows, eye).astype(BF16)

    last = pl.multiple_of((i + 1) * tq - 128, 128)
    cq = c_ref[0, :, pl.ds(last, 128)][:, 127:128]
    qa = jnp.concatenate([qT_ref[...], jnp.broadcast_to(_bias_rows(cq, False, FOX_EXT), (FOX_EXT, tq))],
                         axis=0)
    ones = jnp.ones((FOX_ONES, tk), BF16)

    def scores(j, dst):
        ks = pl.multiple_of(j * tk, tk)
        ka = jnp.concatenate([k_ref[pl.ds(ks, tk), :], kext[pl.ds(ks, tk), :]], axis=1)
        dst[...] = _dot(ka, qa)

    def absorb(j, src, diag):
        ks = pl.multiple_of(j * tk, tk)
        sT = src[...]
        if diag:
            key = lax.broadcasted_iota(jnp.int32, (tk, tq), 0)
            qry = lax.broadcasted_iota(jnp.int32, (tk, tq), 1)
            sT = jnp.where(key <= qry, sT, MASK_VALUE)
        m = m_scr[...]
        m_new = jnp.maximum(m, jnp.max(sT, axis=0, keepdims=True))
        p = jnp.exp2(sT - m_new).astype(BF16)
        m_scr[...] = m_new
        va = jnp.concatenate([vT_ref[:, pl.ds(ks, tk)], ones], axis=0)
        acc_scr[...] = jnp.exp2(m - m_new) * acc_scr[...] + _dot(va, p)

    def finish(src):
        absorb(i, src, True)
        oT_ref[...] = (acc_scr[:FOX_HD, :] / acc_scr[FOX_HD:FOX_HD + 1, :]).astype(oT_ref.dtype)

    m_scr[...] = jnp.full(m_scr.shape, MASK_VALUE, F32)
    acc_scr[...] = jnp.zeros(acc_scr.shape, F32)
    scores(0, s0)

    def pair(t, carry):
        j = 2 * t
        scores(j + 1, s1)
        absorb(j, s0, False)
        scores(j + 2, s0)
        absorb(j + 1, s1, False)
        return carry

    lax.fori_loop(0, i // 2, pair, 0)

    @pl.when(i % 2 == 0)
    def _():
        finish(s0)

    @pl.when(i % 2 == 1)
    def _():
        scores(i, s1)
        absorb(i - 1, s0, False)
        finish(s1)


def _fox_prompt(q2T, k, vT, c2_rows, *, tq=1024):
    h, s = c2_rows.shape
    return pl.pallas_call(
        functools.partial(_fox_prompt_body, tq=tq),
        grid=(h, s // tq),
        in_specs=[pl.BlockSpec((FOX_HD, tq), lambda g, i: (g, i)),
                  pl.BlockSpec((s, FOX_HD), lambda g, i: (0, g)),
                  pl.BlockSpec((FOX_HD, s), lambda g, i: (g, 0)),
                  pl.BlockSpec((1, 1, s), lambda g, i: (g, 0, 0))],
        out_specs=pl.BlockSpec((FOX_HD, tq), lambda g, i: (g, i)),
        out_shape=jax.ShapeDtypeStruct((h * FOX_HD, s), BF16),
        scratch_shapes=[pltpu.VMEM((s, FOX_EXT), BF16), pltpu.VMEM((tq, tq), F32), pltpu.VMEM((tq, tq), F32),
                        pltpu.VMEM((1, tq), F32), pltpu.VMEM((FOX_HD + FOX_ONES, tq), F32)],
        compiler_params=_cparams(("parallel", "arbitrary")),
        name="fox_prompt_attention",
    )(q2T, k, vT, c2_rows[:, None, :])


def _ffn_fox_body(pt_ref, h_ref, wu_ref, wd_ref, g_ref, b_ref,
                  q_ref, kn_ref, vn_ref, lfn_ref, msk_ref, t4_ref, t5_ref, *rest, n_half, alpha):
    del pt_ref
    k_pages = rest[:n_half]
    v_pages = rest[n_half:2 * n_half]
    lf_pages = rest[2 * n_half:3 * n_half]
    y_ref, o_ref, hb_ref, s_scr, mbq_scr, m_scr, l_scr, acc_scr, later_scr = rest[3 * n_half:]
    kc = pl.program_id(1)
    rows = q_ref.shape[1]
    pk = k_pages[0].shape[1]
    nh = FOX_HEADS
    q = q_ref[0]
    first = kc % 2 == 0

    @pl.when(kc == 0)
    def _():
        hb_ref[...] = h_ref[...].astype(BF16)
        y_ref[...] = jnp.zeros_like(y_ref)

    @pl.when(first)
    def _():
        r = lfn_ref[0] * LOG2E
        bq = _split3_dot(jnp.broadcast_to(r, (rows, rows)) * msk_ref[...], t4_ref[...])
        cn = _split3_dot(jnp.broadcast_to(r, (8, rows)), t5_ref[...])[0:1]
        row_h = lax.broadcasted_iota(jnp.int32, (rows, 128), 0) & (nh - 1)
        lane_h = lax.broadcasted_iota(jnp.int32, (rows, 128), 1) & (nh - 1)
        same_head = row_h == lane_h
        mbq = jnp.where(same_head, bq, MASK_VALUE)
        mbq_scr[...] = jnp.concatenate([mbq] * (pk // 128), axis=1)
        row_t = lax.broadcasted_iota(jnp.int32, (rows, rows), 0) >> 3
        lane_t = lax.broadcasted_iota(jnp.int32, (rows, rows), 1) >> 3
        ok_new = same_head[:, :rows] & (lane_t <= row_t)
        s_new = jnp.where(ok_new, _dot_nt(q, kn_ref[0].astype(BF16)) + (bq[:, :rows] - cn), MASK_VALUE)
        m0 = jnp.max(s_new, axis=-1, keepdims=True)
        p_new = jnp.exp2(s_new - m0)
        m_scr[...] = m0
        l_scr[...] = jnp.sum(p_new, axis=-1, keepdims=True)
        acc_scr[...] = _dot(p_new.astype(BF16), vn_ref[0].astype(BF16))
        later_scr[...] = jnp.zeros_like(later_scr)

    later = later_scr[...]
    m_old = m_scr[...]
    m_new = m_old
    for p in reversed(range(n_half)):
        lf = lf_pages[p][0]
        tot = lf[:, pk:]
        d = jnp.concatenate([later + tot] * (pk // 128), axis=1) - lf[:, :pk]
        later = later + tot
        s = _dot_nt(q, k_pages[p][0].astype(BF16)) + (mbq_scr[...] + d)
        s_scr[:, p * pk:(p + 1) * pk] = s
        m_new = jnp.maximum(m_new, jnp.max(s, axis=-1, keepdims=True))
    later_scr[...] = later

    a = jnp.square(jnp.maximum(_dot(hb_ref[...], wu_ref[...]), 0.0))
    y_ref[...] += _dot(a.astype(BF16), wd_ref[...])

    scale_old = jnp.exp2(m_old - m_new)
    l = scale_old * l_scr[...]
    acc = scale_old * acc_scr[...]
    for p in range(n_half):
        pr = jnp.exp2(s_scr[:, p * pk:(p + 1) * pk] - m_new)
        l = l + jnp.sum(pr, axis=-1, keepdims=True)
        acc = acc + _dot(pr.astype(BF16), v_pages[p][0].astype(BF16))
    m_scr[...] = m_new
    l_scr[...] = l
    acc_scr[...] = acc

    @pl.when(jnp.logical_not(first))
    def _():
        o_ref[0] = (acc / l).astype(o_ref.dtype)

    @pl.when(kc == pl.num_programs(1) - 1)
    def _():
        y_ref[...] = _layer_norm(alpha * h_ref[...] + y_ref[...], g_ref[...], b_ref[...])


def _ffn_fox(h, w_up, w_down, g, b, q, k_new, v_new, lf_new, cache_k, cache_v, lf_cache, page_table,
             *, alpha, tm=512, tf=512):
    m, d = h.shape
    dff = w_up.shape[1]
    nb, rows, hd = q.shape
    n_pages = page_table.shape[1]
    n_half = n_pages // 2
    pk = cache_k.shape[1]
    ni, nk = m // tm, dff // tf
    per_i = nk // 2
    assert rows == 64 and FOX_HEADS == 8 and pk % 128 == 0 and n_pages % 2 == 0
    assert m % tm == 0 and dff % tf == 0 and nk % 2 == 0 and ni * per_i == nb

    rr = np.arange(rows)
    tt, hh = rr // FOX_HEADS, rr % FOX_HEADS
    lane = np.arange(128)
    msk = jnp.asarray((tt[None, :] <= tt[:, None]).astype(np.float32))
    t4 = jnp.asarray((hh[:, None] == (lane % FOX_HEADS)[None, :]).astype(np.float32), BF16)
    t5 = jnp.asarray(((hh[:, None] == hh[None, :]) & (tt[:, None] <= tt[None, :])).astype(np.float32), BF16)

    batch = lambda i, k: i * per_i + k // 2

    def page_map(p):
        return lambda i, k, pt: (pt[batch(i, k) * n_pages + (1 - k % 2) * n_half + p], 0, 0)

    per_b = lambda i, k, pt: (batch(i, k), 0, 0)
    const = lambda i, k, pt: (0, 0)
    in_specs = [pl.BlockSpec((tm, d), lambda i, k, pt: (i, 0)),
                pl.BlockSpec((d, tf), lambda i, k, pt: (0, k)),
                pl.BlockSpec((tf, d), lambda i, k, pt: (k, 0)),
                pl.BlockSpec((1, d), const),
                pl.BlockSpec((1, d), const),
                pl.BlockSpec((1, rows, hd), per_b),
                pl.BlockSpec((1, rows, hd), per_b),
                pl.BlockSpec((1, rows, hd), per_b),
                pl.BlockSpec((1, 1, rows), per_b),
                pl.BlockSpec((rows, rows), const),
                pl.BlockSpec((rows, 128), const),
                pl.BlockSpec((rows, rows), const)]
    in_specs += [pl.BlockSpec((1, pk, hd), page_map(p)) for p in range(n_half)]
    in_specs += [pl.BlockSpec((1, pk, hd), page_map(p)) for p in range(n_half)]
    in_specs += [pl.BlockSpec((1, 1, pk + 128), page_map(p)) for p in range(n_half)]
    grid_spec = pltpu.PrefetchScalarGridSpec(
        num_scalar_prefetch=1, grid=(ni, nk), in_specs=in_specs,
        out_specs=[pl.BlockSpec((tm, d), lambda i, k, pt: (i, 0)),
                   pl.BlockSpec((1, rows, hd), per_b)],
        scratch_shapes=[pltpu.VMEM((tm, d), BF16), pltpu.VMEM((rows, n_half * pk), F32),
                        pltpu.VMEM((rows, pk), F32), pltpu.VMEM((rows, 1), F32), pltpu.VMEM((rows, 1), F32),
                        pltpu.VMEM((rows, hd), F32), pltpu.VMEM((1, 128), F32)])
    return pl.pallas_call(
        functools.partial(_ffn_fox_body, n_half=n_half, alpha=alpha),
        grid_spec=grid_spec,
        out_shape=[jax.ShapeDtypeStruct((m, d), F32), jax.ShapeDtypeStruct((nb, rows, hd), BF16)],
        compiler_params=_cparams(("arbitrary", "arbitrary")),
        name="ffn_ln_fox_sample",
    )(page_table.reshape(-1), h, w_up, w_down, g, b, q, k_new, v_new, lf_new, msk, t4, t5,
      *([cache_k] * n_half), *([cache_v] * n_half), *([lf_cache] * n_half))


def _page_logf_sums(cache_logf):
    p, page, h = cache_logf.shape
    n = page * h
    pos, head = np.arange(n) // h, np.arange(n) % h
    prefix = (head[:, None] == head[None, :]) & (pos[:, None] <= pos[None, :])
    total = head[:, None] == (np.arange(128) % h)[None, :]
    t = jnp.asarray(np.concatenate([prefix, total], axis=1).astype(np.float32), BF16)
    tm = next(c for c in (512, 256, 128, 64, 32, 16, 8) if p % c == 0)

    def body(x_ref, t_ref, o_ref):
        o_ref[...] = _split3_dot(x_ref[...], t_ref[...]) * LOG2E

    out = pl.pallas_call(
        body, grid=(p // tm,),
        in_specs=[pl.BlockSpec((tm, n), lambda i: (i, 0)),
                  pl.BlockSpec((n, n + 128), lambda i: (0, 0))],
        out_specs=pl.BlockSpec((tm, n + 128), lambda i: (i, 0)),
        out_shape=jax.ShapeDtypeStruct((p, n + 128), F32),
        compiler_params=_cparams(("parallel",)),
        name="page_logf_sums",
    )(cache_logf.reshape(p, n), t)
    return out.reshape(p, 1, n + 128)


def _pool_body(u_ref, halo_ref, pos_ref, wp_ref, sc_ref, o_ref, *, zero_first):
    u = u_ref[...]
    halo = halo_ref[...]
    if zero_first:
        halo = jnp.where(pl.program_id(0) == 0, 0.0, halo)
    pos1 = pos_ref[...] + 1.0
    outs = []
    for g, w in enumerate(POOL_WINDOWS):
        sl = slice(g * POOL_GW, (g + 1) * POOL_GW)
        s = jnp.concatenate([halo[:, sl], u[:, sl]], axis=0)
        sh = 1
        while sh < w:
            s = s + pltpu.roll(s, sh, 0)
            sh *= 2
        d = s[POOL_HALO:] / jnp.minimum(float(w), pos1) - u[:, sl]
        outs.append(_dot(d.astype(BF16), wp_ref[g]))
    o_ref[...] = (jnp.concatenate(outs, axis=1) * sc_ref[...]).astype(o_ref.dtype)


def _pool(u, pos, w_pool, pool_scale, *, tm, zero_first):
    assert all(w & (w - 1) == 0 for w in POOL_WINDOWS) and tm % POOL_HALO == 0
    r, width = u.shape
    hb = tm // POOL_HALO
    return pl.pallas_call(
        functools.partial(_pool_body, zero_first=zero_first),
        grid=(r // tm,),
        in_specs=[pl.BlockSpec((tm, width), lambda i: (i, 0)),
                  pl.BlockSpec((POOL_HALO, width), lambda i: (jnp.maximum(i * hb - 1, 0), 0)),
                  pl.BlockSpec((tm, 1), lambda i: (i, 0)),
                  pl.BlockSpec(w_pool.shape, lambda i: (0, 0, 0)),
                  pl.BlockSpec((1, width), lambda i: (0, 0))],
        out_specs=pl.BlockSpec((tm, width), lambda i: (i, 0)),
        out_shape=jax.ShapeDtypeStruct((r, width), BF16),
        compiler_params=_cparams(("arbitrary",)),
        name="pool_mixer",
    )(u, u, pos, w_pool, pool_scale)


def _mem_prompt_body(q_ref, k_ref, v_ref, o_ref, *, scale):
    for h in range(MEM_HEADS):
        sl = slice(h * MEM_HD, (h + 1) * MEM_HD)
        s = _dot_nt(q_ref[:, sl], k_ref[:, sl]) * scale
        p = jnp.exp(s - jnp.max(s, axis=-1, keepdims=True))
        l = jnp.sum(p, axis=-1, keepdims=True)
        o_ref[:, sl] = (_dot(p.astype(BF16), v_ref[:, sl]) / l).astype(o_ref.dtype)


def _mem_prompt(qm, mk, mv, *, tq=1024):
    s, width = qm.shape
    m = mk.shape[0]
    return pl.pallas_call(
        functools.partial(_mem_prompt_body, scale=MEM_HD ** -0.5),
        grid=(s // tq,),
        in_specs=[pl.BlockSpec((tq, width), lambda i: (i, 0)),
                  pl.BlockSpec((m, width), lambda i: (0, 0)),
                  pl.BlockSpec((m, width), lambda i: (0, 0))],
        out_specs=pl.BlockSpec((tq, width), lambda i: (i, 0)),
        out_shape=jax.ShapeDtypeStruct((s, width), BF16),
        compiler_params=_cparams(("parallel",)),
        name="mem_prompt_attention",
    )(qm, mk, mv)


def _mem_sample_body(q_ref, k_ref, v_ref, o_ref, *, bb, scale):
    rows = q_ref.shape[1]
    m, h, d = k_ref.shape[1:]
    row_h = lax.broadcasted_iota(jnp.int32, (rows, m * h), 0) & (h - 1)
    lane_h = lax.broadcasted_iota(jnp.int32, (rows, m * h), 1) & (h - 1)
    own = row_h == lane_h
    for b in range(bb):
        k2 = k_ref[b].reshape(m * h, d).astype(BF16)
        v2 = v_ref[b].reshape(m * h, d).astype(BF16)
        s = jnp.where(own, _dot_nt(q_ref[b], k2) * scale, MASK_VALUE)
        p = jnp.exp(s - jnp.max(s, axis=-1, keepdims=True))
        l = jnp.sum(p, axis=-1, keepdims=True)
        o_ref[b] = (_dot(p.astype(BF16), v2) / l).astype(o_ref.dtype)


def _mem_sample(qm, ck, cv, *, bb=4):
    b, rows, d = qm.shape
    _, m, h, _ = ck.shape
    assert h & (h - 1) == 0 and b % bb == 0
    return pl.pallas_call(
        functools.partial(_mem_sample_body, bb=bb, scale=MEM_HD ** -0.5),
        grid=(b // bb,),
        in_specs=[pl.BlockSpec((bb, rows, d), lambda i: (i, 0, 0)),
                  pl.BlockSpec((bb, m, h, d), lambda i: (i, 0, 0, 0)),
                  pl.BlockSpec((bb, m, h, d), lambda i: (i, 0, 0, 0))],
        out_specs=pl.BlockSpec((bb, rows, d), lambda i: (i, 0, 0)),
        out_shape=jax.ShapeDtypeStruct((b, rows, d), BF16),
        compiler_params=_cparams(("parallel",)),
        name="mem_sample_attention",
    )(qm, ck, cv)


def _merge_body(x_ref, of_ref, op_ref, om_ref, wf_ref, wp_ref, wm_ref, g0_ref, g1_ref, g2_ref,
                o_ref, *, fox_transposed):
    x = x_ref[...]
    fox = (_dot_tn if fox_transposed else _dot)(of_ref[...], wf_ref[...])
    merged = (jax.nn.sigmoid(_dot_nt(x, g0_ref[...])) * fox
              + jax.nn.sigmoid(_dot_nt(x, g1_ref[...])) * _dot(op_ref[...], wp_ref[...])
              + jax.nn.sigmoid(_dot_nt(x, g2_ref[...])) * _dot(om_ref[...], wm_ref[...]))
    o_ref[...] = merged.astype(o_ref.dtype)


def _merge(x, o_fox, o_pool, o_mem, w_f, w_p, w_m, w_gates_t, *, fox_transposed, tm=512, tn=512):
    m, d = x.shape
    kb = o_pool.shape[1]
    n = w_f.shape[1]
    tm = min(tm, m)
    nb = n // tn
    act = pl.BlockSpec((tm, kb), lambda i, j: (i, 0))
    act_fox = pl.BlockSpec((kb, tm), lambda i, j: (0, i)) if fox_transposed else act
    wgt = pl.BlockSpec((kb, tn), lambda i, j: (0, j))
    gate = lambda g: pl.BlockSpec((tn, d), lambda i, j: (g * nb + j, 0))
    return pl.pallas_call(
        functools.partial(_merge_body, fox_transposed=fox_transposed), grid=(m // tm, nb),
        in_specs=[pl.BlockSpec((tm, d), lambda i, j: (i, 0)), act_fox, act, act, wgt, wgt, wgt,
                  gate(0), gate(1), gate(2)],
        out_specs=pl.BlockSpec((tm, tn), lambda i, j: (i, j)),
        out_shape=jax.ShapeDtypeStruct((m, n), BF16),
        compiler_params=_cparams(("parallel", "arbitrary")),
        name="branch_merge",
    )(x, o_fox, o_pool, o_mem, w_f, w_p, w_m, w_gates_t, w_gates_t, w_gates_t)


def _ep_residual_ln(acc, x, g, b, *, alpha):
    return (_layer_norm(alpha * x + acc, g, b),)


def _ffn_body(h_ref, wu_ref, wd_ref, g_ref, b_ref, o_ref, hb_ref, *, alpha):
    k = pl.program_id(1)

    @pl.when(k == 0)
    def _():
        hb_ref[...] = h_ref[...].astype(BF16)
        o_ref[...] = jnp.zeros_like(o_ref)

    a = jnp.square(jnp.maximum(_dot(hb_ref[...], wu_ref[...]), 0.0))
    o_ref[...] += _dot(a.astype(BF16), wd_ref[...])

    @pl.when(k == pl.num_programs(1) - 1)
    def _():
        o_ref[...] = _layer_norm(alpha * h_ref[...] + o_ref[...], g_ref[...], b_ref[...])


def _ffn(h, w_up, w_down, g, b, *, alpha, tm=1024, tf=512):
    m, d = h.shape
    dff = w_up.shape[1]
    tm = min(tm, m)
    return pl.pallas_call(
        functools.partial(_ffn_body, alpha=alpha),
        grid=(m // tm, dff // tf),
        in_specs=[pl.BlockSpec((tm, d), lambda i, k: (i, 0)),
                  pl.BlockSpec((d, tf), lambda i, k: (0, k)),
                  pl.BlockSpec((tf, d), lambda i, k: (k, 0)),
                  pl.BlockSpec((1, d), lambda i, k: (0, 0)),
                  pl.BlockSpec((1, d), lambda i, k: (0, 0))],
        out_specs=pl.BlockSpec((tm, d), lambda i, k: (i, 0)),
        out_shape=jax.ShapeDtypeStruct((m, d), F32),
        scratch_shapes=[pltpu.VMEM((tm, d), BF16)],
        compiler_params=_cparams(("parallel", "arbitrary")),
        name="ffn_ln",
    )(h, w_up, w_down, g, b)


def _mix(x, x_bf, o_fox, o_pool, o_mem, w, *, alpha, fox_transposed):
    d = x.shape[1]
    merged = _merge(x_bf, o_fox, o_pool, o_mem, w["br_fox"], w["br_pool"], w["br_mem"],
                    w["gates_t"], fox_transposed=fox_transposed)
    row = lambda i, j: (0, 0)
    h, = _mm(merged, w["out"], [(F32, False)], functools.partial(_ep_residual_ln, alpha=alpha),
             extras=[(x, (min(512, x.shape[0]), d), lambda i, j: (i, 0)),
                     (w["ln1_g"], (1, d), row), (w["ln1_b"], (1, d), row)],
             tm=512, tn=d, name="out_proj_ln")
    return h


def kernel(x_prompt, x_sample, mem_prompt, cache_fox_k, cache_fox_v, cache_fox_logf, state_pool,
           cache_mem_k, cache_mem_v, page_table, w_in, b_fgate, w_pool, pool_scale, w_br_fox,
           w_br_pool, w_br_mem, w_mem_kv, w_out, ln1_g, ln1_b, w_up, w_down, ln2_g, ln2_b):
    batch, seq, d_model = x_prompt.shape
    dec_batch, dec_seq, _ = x_sample.shape
    depth = w_in.shape[0]
    n_phys, page_size = cache_fox_k.shape[1], cache_fox_k.shape[2]
    n_pages = page_table.shape[1]
    past = n_pages * page_size
    fox_w = FOX_HEADS * FOX_HD
    pool_w = len(POOL_WINDOWS) * POOL_GW
    mem_w = MEM_HEADS * MEM_HD
    mem_tokens = mem_prompt.shape[1]
    assert batch == 1 and dec_seq == FOX_HEADS == 8
    assert fox_w == pool_w == mem_w
    alpha = (2 * depth) ** 0.25
    nat, tr = False, True

    hp = x_prompt.reshape(seq, d_model)
    hs = x_sample.reshape(dec_batch * dec_seq, d_model)
    seg = POOL_HALO + dec_seq
    pos_p = jnp.arange(seq, dtype=F32)[:, None]
    pos_s = jnp.tile(jnp.concatenate([jnp.full((POOL_HALO,), 1e9, F32),
                                      past + jnp.arange(dec_seq, dtype=F32)]), dec_batch)[:, None]

    outs = {n: [] for n in ("pk", "pv", "pf", "pp", "pmk", "pmv", "sk", "sv", "sf", "sp")}
    for l in range(depth):
        w_t = jnp.swapaxes(w_in[l], 0, 1)
        r_q, r_k, r_v, r_f = 0, fox_w, 2 * fox_w, 3 * fox_w
        r_u = r_f + FOX_HEADS
        r_qm = r_u + pool_w
        r_g = r_qm + mem_w
        b_f = jnp.pad(b_fgate[l], (0, 128 - FOX_HEADS))[None, :]
        w = dict(gates_t=w_t[r_g:].astype(BF16),
                 br_fox=w_br_fox[l].astype(BF16), br_pool=w_br_pool[l].astype(BF16),
                 br_mem=w_br_mem[l].astype(BF16), out=w_out[l].astype(BF16),
                 up=w_up[l].astype(BF16), down=w_down[l].astype(BF16),
                 ln1_g=ln1_g[l][None, :], ln1_b=ln1_b[l][None, :],
                 ln2_g=ln2_g[l][None, :], ln2_b=ln2_b[l][None, :])
        wp_bf = w_pool[l].astype(BF16)
        psc = pool_scale[l][None, :]
        logf_of = functools.partial(_mm, w=w_t, outs=[(F32, nat)], epilogue=_ep_log_sigmoid,
                                    extras=[(b_f, (1, 128), lambda i, j: (0, 0))], tn=128, row0=r_f,
                                    ncols=128, name="proj_logf")

        x_bf, q2T = _mm(hp, w_t, [(BF16, tr)], functools.partial(_ep_scaled_t, scale=FOX_HD ** -0.5 * LOG2E),
                        row0=r_q, ncols=fox_w, tm=512, emit_x=True, name="proj_q")
        k, k_bf = _mm(x_bf, w_t, [(F32, nat), (BF16, nat)], _ep_dup, row0=r_k, ncols=fox_w, name="proj_k")
        v, vT = _mm(x_bf, w_t, [(F32, nat), (BF16, tr)], _ep_with_t, row0=r_v, ncols=fox_w, name="proj_v")
        u, = _mm(x_bf, w_t, [(F32, nat)], _ep_identity, row0=r_u, ncols=pool_w, name="proj_u")
        qm, = _mm(x_bf, w_t, [(BF16, nat)], _ep_identity, row0=r_qm, ncols=mem_w, name="proj_qm")
        logf = logf_of(x_bf)[0][:, :FOX_HEADS]
        c2_rows = _cumsum_lanes(logf.T, scale=LOG2E)
        o_foxT = _fox_prompt(q2T, k_bf, vT, c2_rows)
        o_pool = _pool(u, pos_p, wp_bf, psc, tm=1024, zero_first=True)
        mkv, mkv_bf = _mm(mem_prompt.reshape(mem_tokens, d_model), w_mem_kv[l], [(F32, nat), (BF16, nat)],
                          _ep_dup, name="mem_kv")
        o_mem = _mem_prompt(qm, mkv_bf[:, :mem_w], mkv_bf[:, mem_w:])
        outs["pk"].append(k.reshape(batch, seq, FOX_HEADS, FOX_HD))
        outs["pv"].append(v.reshape(batch, seq, FOX_HEADS, FOX_HD))
        outs["pf"].append(logf.reshape(batch, seq, FOX_HEADS))
        outs["pp"].append(u[-POOL_STATE:].reshape(batch, POOL_STATE, pool_w))
        outs["pmk"].append(mkv[:, :mem_w].reshape(batch, mem_tokens, MEM_HEADS, MEM_HD))
        outs["pmv"].append(mkv[:, mem_w:].reshape(batch, mem_tokens, MEM_HEADS, MEM_HD))
        h_p = _mix(hp, x_bf, o_foxT, o_pool, o_mem, w, alpha=alpha, fox_transposed=True)

        x_bf, q = _mm(hs, w_t, [(BF16, nat)], functools.partial(_ep_scaled, scale=FOX_HD ** -0.5 * LOG2E),
                      row0=r_q, ncols=fox_w, tm=512, emit_x=True, name="proj_q")
        k, = _mm(x_bf, w_t, [(F32, nat)], _ep_identity, row0=r_k, ncols=fox_w, name="proj_k")
        v, = _mm(x_bf, w_t, [(F32, nat)], _ep_identity, row0=r_v, ncols=fox_w, name="proj_v")
        u, = _mm(x_bf, w_t, [(F32, nat)], _ep_identity, row0=r_u, ncols=pool_w, name="proj_u")
        qm, = _mm(x_bf, w_t, [(BF16, nat)], _ep_identity, row0=r_qm, ncols=mem_w, name="proj_qm")
        logf = logf_of(x_bf)[0][:, :FOX_HEADS]
        rows = dec_seq * FOX_HEADS
        hp, o_fox = _ffn_fox(
            h_p, w["up"], w["down"], w["ln2_g"], w["ln2_b"],
            q.reshape(dec_batch, rows, FOX_HD), k.reshape(dec_batch, rows, FOX_HD),
            v.reshape(dec_batch, rows, FOX_HD), logf.reshape(dec_batch, 1, rows),
            cache_fox_k[l].reshape(n_phys, page_size * FOX_HEADS, FOX_HD),
            cache_fox_v[l].reshape(n_phys, page_size * FOX_HEADS, FOX_HD),
            _page_logf_sums(cache_fox_logf[l]), page_table, alpha=alpha)
        o_fox = o_fox.reshape(dec_batch * dec_seq, fox_w)
        u3 = u.reshape(dec_batch, dec_seq, pool_w)
        state = state_pool[l].astype(F32)
        u_ext = jnp.concatenate([jnp.zeros((dec_batch, POOL_HALO - POOL_STATE, pool_w), F32), state, u3],
                                axis=1).reshape(dec_batch * seg, pool_w)
        o_pool = _pool(u_ext, pos_s, wp_bf, psc, tm=32 * seg, zero_first=False)
        o_pool = o_pool.reshape(dec_batch, seg, pool_w)[:, POOL_HALO:].reshape(dec_batch * dec_seq, pool_w)
        o_mem = _mem_sample(qm.reshape(dec_batch, dec_seq * MEM_HEADS, MEM_HD), cache_mem_k[l],
                            cache_mem_v[l]).reshape(dec_batch * dec_seq, mem_w)
        outs["sk"].append(k.reshape(dec_batch, dec_seq, FOX_HEADS, FOX_HD))
        outs["sv"].append(v.reshape(dec_batch, dec_seq, FOX_HEADS, FOX_HD))
        outs["sf"].append(logf.reshape(dec_batch, dec_seq, FOX_HEADS))
        outs["sp"].append(jnp.concatenate([state, u3], axis=1)[:, -POOL_STATE:])
        h_s = _mix(hs, x_bf, o_fox, o_pool, o_mem, w, alpha=alpha, fox_transposed=False)
        hs = _ffn(h_s, w["up"], w["down"], w["ln2_g"], w["ln2_b"], alpha=alpha)

    st = lambda n: jnp.stack(outs[n])
    return (hp.reshape(batch, seq, d_model), hs.reshape(dec_batch, dec_seq, d_model),
            st("pk"), st("pv"), st("pf"), st("pp"), st("pmk"), st("pmv"),
            st("sk"), st("sv"), st("sf"), st("sp"))
```

```python
import functools

import numpy as np

import jax
import jax.numpy as jnp
from jax import lax
from jax.experimental import pallas as pl
from jax.experimental.pallas import tpu as pltpu

F32 = jnp.float32
BF16 = jnp.bfloat16

FOX_HEADS = 8
FOX_HD = 128
POOL_WINDOWS = (2, 4, 8, 16)
POOL_GW = 256
POOL_STATE = max(POOL_WINDOWS) - 1
POOL_HALO = 16
MEM_HEADS = 4
MEM_HD = 256
LN_EPS = 1e-5
MASK_VALUE = -1e30
LOG2E = 1.4426950408889634
FOX_EXT = 128
FOX_ONES = 16

V7X_VMEM_LIMIT_BYTES = 56 * 1024 * 1024


def _cparams(semantics):
    return pltpu.CompilerParams(dimension_semantics=semantics,
                                vmem_limit_bytes=V7X_VMEM_LIMIT_BYTES)


def _dot(a, b):
    return jnp.dot(a, b, preferred_element_type=F32)


def _dot_nt(a, b):
    return lax.dot_general(a, b, (((1,), (1,)), ((), ())), preferred_element_type=F32)


def _dot_tn(a, b):
    return lax.dot_general(a, b, (((0,), (0,)), ((), ())), preferred_element_type=F32)


def _split3(x):
    hi = x.astype(BF16)
    r1 = x - hi.astype(F32)
    mid = r1.astype(BF16)
    lo = (r1 - mid.astype(F32)).astype(BF16)
    return hi, mid, lo


def _split3_dot(x, t):
    hi, mid, lo = _split3(x)
    return _dot(hi, t) + _dot(mid, t) + _dot(lo, t)


def _layer_norm(z, g, b):
    mu = jnp.mean(z, axis=-1, keepdims=True)
    zc = z - mu
    var = jnp.mean(zc * zc, axis=-1, keepdims=True)
    return zc * lax.rsqrt(var + LN_EPS) * g + b


def _mm_body(*refs, epilogue, n_in, emit_x, w_rows):
    x_ref, w_ref = refs[0], refs[1]
    extras = refs[2:n_in]
    outs = list(refs[n_in:])
    x = x_ref[...].astype(BF16)
    if emit_x:
        outs.pop(0)[...] = x
    acc = (_dot_nt if w_rows else _dot)(x, w_ref[...].astype(BF16))
    res = epilogue(acc, *(e[...] for e in extras))
    for o, r in zip(outs, res):
        o[...] = r.astype(o.dtype)


def _mm(x, w, outs, epilogue, *, extras=(), tm=1024, tn=1024, row0=None, ncols=None, emit_x=False, name):
    m, k = x.shape
    n = w.shape[1] if ncols is None else ncols
    tm, tn = min(tm, m), min(tn, n)
    assert m % tm == 0 and n % tn == 0 and (n == tn or not emit_x)
    if row0 is None:
        w_spec = pl.BlockSpec((k, tn), lambda i, j: (0, j))
    else:
        assert row0 % 8 == 0 and tn % 8 == 0
        w_spec = pl.BlockSpec((pl.Element(tn), pl.Element(k)),
                              lambda i, j: (pl.multiple_of(row0 + j * tn, 8), 0))
    in_specs = [pl.BlockSpec((tm, k), lambda i, j: (i, 0)), w_spec]
    in_specs += [pl.BlockSpec(bs, im) for _, bs, im in extras]
    out_specs = [pl.BlockSpec((tn, tm), lambda i, j: (j, i)) if t else
                 pl.BlockSpec((tm, tn), lambda i, j: (i, j)) for _, t in outs]
    out_shape = [jax.ShapeDtypeStruct((n, m) if t else (m, n), dt) for dt, t in outs]
    if emit_x:
        out_specs.insert(0, pl.BlockSpec((tm, k), lambda i, j: (i, 0)))
        out_shape.insert(0, jax.ShapeDtypeStruct((m, k), BF16))
    return pl.pallas_call(
        functools.partial(_mm_body, epilogue=epilogue, n_in=2 + len(extras), emit_x=emit_x,
                          w_rows=row0 is not None),
        grid=(m // tm, n // tn),
        in_specs=in_specs, out_specs=out_specs, out_shape=out_shape,
        compiler_params=_cparams(("parallel", "arbitrary")),
        name=name,
    )(x, w, *(a for a, _, _ in extras))


def _ep_identity(acc):
    return (acc,)


def _ep_dup(acc):
    return (acc, acc)


def _ep_scaled(acc, *, scale):
    return (acc * scale,)


def _ep_scaled_t(acc, *, scale):
    return ((acc * scale).T,)


def _ep_with_t(acc):
    return (acc, acc.T)


def _ep_log_sigmoid(acc, bias):
    z = acc + bias
    return (jnp.minimum(z, 0.0) - jnp.log1p(jnp.exp(-jnp.abs(z))),)


def _cumsum_body(x_ref, tri_ref, o_ref, *, chunk, scale):
    n = x_ref.shape[1]
    carry = jnp.zeros((x_ref.shape[0], 1), F32)
    for c in range(n // chunk):
        cs = _split3_dot(x_ref[:, c * chunk:(c + 1) * chunk], tri_ref[...]) + carry
        o_ref[:, c * chunk:(c + 1) * chunk] = cs * scale
        carry = cs[:, chunk - 1:chunk]


def _cumsum_lanes(x, *, scale, chunk=512):
    rows, n = x.shape
    tri = jnp.asarray(np.triu(np.ones((chunk, chunk), np.float32)), BF16)
    return pl.pallas_call(
        functools.partial(_cumsum_body, chunk=chunk, scale=scale),
        out_shape=jax.ShapeDtypeStruct((rows, n), F32),
        name="logf_cumsum",
    )(x, tri)


def _bias_rows(c, ones_first, n):
    sub = lax.broadcasted_iota(jnp.int32, (n, c.shape[1]), 0)
    c0, o0 = (3, 0) if ones_first else (0, 3)
    rows = jnp.where((sub >= o0) & (sub < o0 + 3), 1.0, 0.0).astype(BF16)
    for t, piece in enumerate(_split3(c)):
        rows = jnp.where(sub == c0 + t, piece, rows)
    return rows


def _fox_prompt_body(qT_ref, k_ref, vT_ref, c_ref, oT_ref, kext, s0, s1, m_scr, acc_scr, *, tq):
    i = pl.program_id(1)
    tk = tq

    @pl.when(i == 0)
    def _():
        eye = (lax.broadcasted_iota(jnp.int32, (FOX_ONES, FOX_EXT), 0)
               == lax.broadcasted_iota(jnp.int32, (FOX_ONES, FOX_EXT), 1)).astype(BF16)
        for c in range(k_ref.shape[0] // tk):
            rows = _bias_rows(-c_ref[0, :, c * tk:(c + 1) * tk], True, FOX_ONES)
            kext[c * tk:(c + 1) * tk, :] = _dot_tn(rows, eye).astype(BF16)

    last = pl.multiple_of((i + 1) * tq - 128, 128)
    cq = c_ref[0, :, pl.ds(last, 128)][:, 127:128]
    qa = jnp.concatenate([qT_ref[...], jnp.broadcast_to(_bias_rows(cq, False, FOX_EXT), (FOX_EXT, tq))],
                         axis=0)
    ones = jnp.ones((FOX_ONES, tk), BF16)

    def scores(j, dst):
        ks = pl.multiple_of(j * tk, tk)
        ka = jnp.concatenate([k_ref[pl.ds(ks, tk), :], kext[pl.ds(ks, tk), :]], axis=1)
        dst[...] = _dot(ka, qa)

    def absorb(j, src, diag):
        ks = pl.multiple_of(j * tk, tk)
        sT = src[...]
        if diag:
            key = lax.broadcasted_iota(jnp.int32, (tk, tq), 0)
            qry = lax.broadcasted_iota(jnp.int32, (tk, tq), 1)
            sT = jnp.where(key <= qry, sT, MASK_VALUE)
        m = m_scr[...]
        m_new = jnp.maximum(m, jnp.max(sT, axis=0, keepdims=True))
        p = jnp.exp2(sT - m_new).astype(BF16)
        m_scr[...] = m_new
        va = jnp.concatenate([vT_ref[:, pl.ds(ks, tk)], ones], axis=0)
        acc_scr[...] = jnp.exp2(m - m_new) * acc_scr[...] + _dot(va, p)

    def finish(src):
        absorb(i, src, True)
        oT_ref[...] = (acc_scr[:FOX_HD, :] / acc_scr[FOX_HD:FOX_HD + 1, :]).astype(oT_ref.dtype)

    m_scr[...] = jnp.full(m_scr.shape, MASK_VALUE, F32)
    acc_scr[...] = jnp.zeros(acc_scr.shape, F32)
    scores(0, s0)

    def pair(t, carry):
        j = 2 * t
        scores(j + 1, s1)
        absorb(j, s0, False)
        scores(j + 2, s0)
        absorb(j + 1, s1, False)
        return carry

    lax.fori_loop(0, i // 2, pair, 0)

    @pl.when(i % 2 == 0)
    def _():
        finish(s0)

    @pl.when(i % 2 == 1)
    def _():
        scores(i, s1)
        absorb(i - 1, s0, False)
        finish(s1)


def _fox_prompt(q2T, k, vT, c2_rows, *, tq=1024):
    h, s = c2_rows.shape
    return pl.pallas_call(
        functools.partial(_fox_prompt_body, tq=tq),
        grid=(h, s // tq),
        in_specs=[pl.BlockSpec((FOX_HD, tq), lambda g, i: (g, i)),
                  pl.BlockSpec((s, FOX_HD), lambda g, i: (0, g)),
                  pl.BlockSpec((FOX_HD, s), lambda g, i: (g, 0)),
                  pl.BlockSpec((1, 1, s), lambda g, i: (g, 0, 0))],
        out_specs=pl.BlockSpec((FOX_HD, tq), lambda g, i: (g, i)),
        out_shape=jax.ShapeDtypeStruct((h * FOX_HD, s), BF16),
        scratch_shapes=[pltpu.VMEM((s, FOX_EXT), BF16), pltpu.VMEM((tq, tq), F32), pltpu.VMEM((tq, tq), F32),
                        pltpu.VMEM((1, tq), F32), pltpu.VMEM((FOX_HD + FOX_ONES, tq), F32)],
        compiler_params=_cparams(("parallel", "arbitrary")),
        name="fox_prompt_attention",
    )(q2T, k, vT, c2_rows[:, None, :])


def _ffn_fox_body(pt_ref, h_ref, wu_ref, wd_ref, g_ref, b_ref,
                  q_ref, kn_ref, vn_ref, lfn_ref, msk_ref, t4_ref, t5_ref, *rest, n_half, alpha):
    k_pages = rest[:n_half]
    v_pages = rest[n_half:2 * n_half]
    lf_pages = rest[2 * n_half:3 * n_half]
    y_ref, o_ref, hb_ref, s_scr, mbq_scr, m_scr, l_scr, acc_scr, later_scr = rest[3 * n_half:]
    kc = pl.program_id(1)
    pt0 = ((pl.program_id(0) * (pl.num_programs(1) // 2) + kc // 2) * 2 + (1 - kc % 2)) * n_half
    rows = q_ref.shape[1]
    pk = k_pages[0].shape[1]
    nh = FOX_HEADS
    q = q_ref[0]
    first = kc % 2 == 0

    @pl.when(kc == 0)
    def _():
        hb_ref[...] = h_ref[...].astype(BF16)
        y_ref[...] = jnp.zeros_like(y_ref)

    @pl.when(first)
    def _():
        r = lfn_ref[0] * LOG2E
        bq = _split3_dot(jnp.broadcast_to(r, (rows, rows)) * msk_ref[...], t4_ref[...])
        cn = _split3_dot(jnp.broadcast_to(r, (8, rows)), t5_ref[...])[0:1]
        row_h = lax.broadcasted_iota(jnp.int32, (rows, 128), 0) & (nh - 1)
        lane_h = lax.broadcasted_iota(jnp.int32, (rows, 128), 1) & (nh - 1)
        same_head = row_h == lane_h
        mbq = jnp.where(same_head, bq, MASK_VALUE)
        mbq_scr[...] = jnp.concatenate([mbq] * (pk // 128), axis=1)
        row_t = lax.broadcasted_iota(jnp.int32, (rows, rows), 0) >> 3
        lane_t = lax.broadcasted_iota(jnp.int32, (rows, rows), 1) >> 3
        ok_new = same_head[:, :rows] & (lane_t <= row_t)
        s_new = jnp.where(ok_new, _dot_nt(q, kn_ref[0].astype(BF16)) + (bq[:, :rows] - cn), MASK_VALUE)
        m0 = jnp.max(s_new, axis=-1, keepdims=True)
        p_new = jnp.exp2(s_new - m0)
        m_scr[...] = m0
        l_scr[...] = jnp.sum(p_new, axis=-1, keepdims=True)
        acc_scr[...] = _dot(p_new.astype(BF16), vn_ref[0].astype(BF16))
        later_scr[...] = jnp.zeros_like(later_scr)

    later = later_scr[...]
    m_old = m_scr[...]
    m_new = m_old
    for p in reversed(range(n_half)):
        lf = lf_pages[p][pl.ds(pt_ref[pt0 + p] & 7, 1), :]
        tot = lf[:, pk:]
        d = jnp.concatenate([later + tot] * (pk // 128), axis=1) - lf[:, :pk]
        later = later + tot
        s = _dot_nt(q, k_pages[p][0].astype(BF16)) + (mbq_scr[...] + d)
        s_scr[:, p * pk:(p + 1) * pk] = s
        m_new = jnp.maximum(m_new, jnp.max(s, axis=-1, keepdims=True))
    later_scr[...] = later

    a = jnp.square(jnp.maximum(_dot_nt(hb_ref[...], wu_ref[...]), 0.0))
    y_ref[...] += _dot(a.astype(BF16), wd_ref[...])

    scale_old = jnp.exp2(m_old - m_new)
    l = scale_old * l_scr[...]
    acc = scale_old * acc_scr[...]
    for p in range(n_half):
        pr = jnp.exp2(s_scr[:, p * pk:(p + 1) * pk] - m_new)
        l = l + jnp.sum(pr, axis=-1, keepdims=True)
        acc = acc + _dot(pr.astype(BF16), v_pages[p][0].astype(BF16))
    m_scr[...] = m_new
    l_scr[...] = l
    acc_scr[...] = acc

    @pl.when(jnp.logical_not(first))
    def _():
        o_ref[0] = (acc / l).astype(o_ref.dtype)

    @pl.when(kc == pl.num_programs(1) - 1)
    def _():
        y_ref[...] = _layer_norm(alpha * h_ref[...] + y_ref[...], g_ref[...], b_ref[...])


def _ffn_fox(h, w_up_t, w_down, g, b, q, k_new, v_new, lf_new, cache_k, cache_v, lf_cache, page_table,
             *, alpha, tm=512, tf=512):
    m, d = h.shape
    dff = w_up_t.shape[0]
    nb, rows, hd = q.shape
    n_pages = page_table.shape[1]
    n_half = n_pages // 2
    pk = cache_k.shape[1]
    ni, nk = m // tm, dff // tf
    per_i = nk // 2
    assert rows == 64 and FOX_HEADS == 8 and pk % 128 == 0 and n_pages % 2 == 0
    assert m % tm == 0 and dff % tf == 0 and nk % 2 == 0 and ni * per_i == nb
    assert lf_cache.shape[0] % 8 == 0

    rr = np.arange(rows)
    tt, hh = rr // FOX_HEADS, rr % FOX_HEADS
    lane = np.arange(128)
    msk = jnp.asarray((tt[None, :] <= tt[:, None]).astype(np.float32))
    t4 = jnp.asarray((hh[:, None] == (lane % FOX_HEADS)[None, :]).astype(np.float32), BF16)
    t5 = jnp.asarray(((hh[:, None] == hh[None, :]) & (tt[:, None] <= tt[None, :])).astype(np.float32), BF16)

    batch = lambda i, k: i * per_i + k // 2

    def page(i, k, pt, p):
        return pt[batch(i, k) * n_pages + (1 - k % 2) * n_half + p]

    def page_map(p):
        return lambda i, k, pt: (page(i, k, pt, p), 0, 0)

    def page_sums_map(p):
        return lambda i, k, pt: (page(i, k, pt, p) // 8, 0)

    per_b = lambda i, k, pt: (batch(i, k), 0, 0)
    const = lambda i, k, pt: (0, 0)
    in_specs = [pl.BlockSpec((tm, d), lambda i, k, pt: (i, 0)),
                pl.BlockSpec((tf, d), lambda i, k, pt: (k, 0)),
                pl.BlockSpec((tf, d), lambda i, k, pt: (k, 0)),
                pl.BlockSpec((1, d), const),
                pl.BlockSpec((1, d), const),
                pl.BlockSpec((1, rows, hd), per_b),
                pl.BlockSpec((1, rows, hd), per_b),
                pl.BlockSpec((1, rows, hd), per_b),
                pl.BlockSpec((1, 1, rows), per_b),
                pl.BlockSpec((rows, rows), const),
                pl.BlockSpec((rows, 128), const),
                pl.BlockSpec((rows, rows), const)]
    in_specs += [pl.BlockSpec((1, pk, hd), page_map(p)) for p in range(n_half)]
    in_specs += [pl.BlockSpec((1, pk, hd), page_map(p)) for p in range(n_half)]
    in_specs += [pl.BlockSpec((8, pk + 128), page_sums_map(p)) for p in range(n_half)]
    grid_spec = pltpu.PrefetchScalarGridSpec(
        num_scalar_prefetch=1, grid=(ni, nk), in_specs=in_specs,
        out_specs=[pl.BlockSpec((tm, d), lambda i, k, pt: (i, 0)),
                   pl.BlockSpec((1, rows, hd), per_b)],
        scratch_shapes=[pltpu.VMEM((tm, d), BF16), pltpu.VMEM((rows, n_half * pk), F32),
                        pltpu.VMEM((rows, pk), F32), pltpu.VMEM((rows, 1), F32), pltpu.VMEM((rows, 1), F32),
                        pltpu.VMEM((rows, hd), F32), pltpu.VMEM((1, 128), F32)])
    return pl.pallas_call(
        functools.partial(_ffn_fox_body, n_half=n_half, alpha=alpha),
        grid_spec=grid_spec,
        out_shape=[jax.ShapeDtypeStruct((m, d), F32), jax.ShapeDtypeStruct((nb, rows, hd), BF16)],
        compiler_params=_cparams(("arbitrary", "arbitrary")),
        name="ffn_ln_fox_sample",
    )(page_table.reshape(-1), h, w_up_t, w_down, g, b, q, k_new, v_new, lf_new, msk, t4, t5,
      *([cache_k] * n_half), *([cache_v] * n_half), *([lf_cache] * n_half))


def _page_logf_sums(cache_logf):
    p, page, h = cache_logf.shape
    n = page * h
    pos, head = np.arange(n) // h, np.arange(n) % h
    prefix = (np.arange(h)[:, None, None] == head[None, None, :]) & (np.arange(page)[None, :, None] <= pos[None, None, :])
    total = np.broadcast_to(np.arange(h)[:, None, None] == (np.arange(128) % h)[None, None, :], (h, page, 128))
    t = jnp.asarray(np.concatenate([prefix, total], axis=2).astype(np.float32), BF16)
    tm = next(c for c in (512, 256, 128, 64, 32, 16, 8) if p % c == 0)

    def body(x_ref, t_ref, o_ref):
        acc = _split3_dot(x_ref[:, 0, :], t_ref[0])
        for g in range(1, h):
            acc = acc + _split3_dot(x_ref[:, g, :], t_ref[g])
        o_ref[...] = acc * LOG2E

    return pl.pallas_call(
        body, grid=(p // tm,),
        in_specs=[pl.BlockSpec((tm, h, page), lambda i: (i, 0, 0)),
                  pl.BlockSpec((h, page, n + 128), lambda i: (0, 0, 0))],
        out_specs=pl.BlockSpec((tm, n + 128), lambda i: (i, 0)),
        out_shape=jax.ShapeDtypeStruct((p, n + 128), F32),
        compiler_params=_cparams(("parallel",)),
        name="page_logf_sums",
    )(jnp.swapaxes(cache_logf, 1, 2), t)


def _pool_body(u_ref, halo_ref, pos_ref, wp_ref, sc_ref, o_ref, *, zero_first):
    u = u_ref[...]
    halo = halo_ref[...]
    if zero_first:
        halo = jnp.where(pl.program_id(0) == 0, 0.0, halo)
    pos1 = pos_ref[...] + 1.0
    outs = []
    for g, w in enumerate(POOL_WINDOWS):
        sl = slice(g * POOL_GW, (g + 1) * POOL_GW)
        s = jnp.concatenate([halo[:, sl], u[:, sl]], axis=0)
        sh = 1
        while sh < w:
            s = s + pltpu.roll(s, sh, 0)
            sh *= 2
        d = s[POOL_HALO:] / jnp.minimum(float(w), pos1) - u[:, sl]
        outs.append(_dot(d.astype(BF16), wp_ref[g]))
    o_ref[...] = (jnp.concatenate(outs, axis=1) * sc_ref[...]).astype(o_ref.dtype)


def _pool(u, pos, w_pool, pool_scale, *, tm, zero_first):
    assert all(w & (w - 1) == 0 for w in POOL_WINDOWS) and tm % POOL_HALO == 0
    r, width = u.shape
    hb = tm // POOL_HALO
    return pl.pallas_call(
        functools.partial(_pool_body, zero_first=zero_first),
        grid=(r // tm,),
        in_specs=[pl.BlockSpec((tm, width), lambda i: (i, 0)),
                  pl.BlockSpec((POOL_HALO, width), lambda i: (jnp.maximum(i * hb - 1, 0), 0)),
                  pl.BlockSpec((tm, 1), lambda i: (i, 0)),
                  pl.BlockSpec(w_pool.shape, lambda i: (0, 0, 0)),
                  pl.BlockSpec((1, width), lambda i: (0, 0))],
        out_specs=pl.BlockSpec((tm, width), lambda i: (i, 0)),
        out_shape=jax.ShapeDtypeStruct((r, width), BF16),
        compiler_params=_cparams(("arbitrary",)),
        name="pool_mixer",
    )(u, u, pos, w_pool, pool_scale)


def _mem_prompt_body(q_ref, k_ref, v_ref, o_ref, *, scale):
    for h in range(MEM_HEADS):
        sl = slice(h * MEM_HD, (h + 1) * MEM_HD)
        s = _dot_nt(q_ref[:, sl], k_ref[:, sl]) * scale
        p = jnp.exp(s - jnp.max(s, axis=-1, keepdims=True))
        l = jnp.sum(p, axis=-1, keepdims=True)
        o_ref[:, sl] = (_dot(p.astype(BF16), v_ref[:, sl]) / l).astype(o_ref.dtype)


def _mem_prompt(qm, mk, mv, *, tq=1024):
    s, width = qm.shape
    m = mk.shape[0]
    return pl.pallas_call(
        functools.partial(_mem_prompt_body, scale=MEM_HD ** -0.5),
        grid=(s // tq,),
        in_specs=[pl.BlockSpec((tq, width), lambda i: (i, 0)),
                  pl.BlockSpec((m, width), lambda i: (0, 0)),
                  pl.BlockSpec((m, width), lambda i: (0, 0))],
        out_specs=pl.BlockSpec((tq, width), lambda i: (i, 0)),
        out_shape=jax.ShapeDtypeStruct((s, width), BF16),
        compiler_params=_cparams(("parallel",)),
        name="mem_prompt_attention",
    )(qm, mk, mv)


def _mem_sample_body(q_ref, k_ref, v_ref, o_ref, *, bb, scale):
    rows = q_ref.shape[1]
    m, h, d = k_ref.shape[1:]
    row_h = lax.broadcasted_iota(jnp.int32, (rows, m * h), 0) & (h - 1)
    lane_h = lax.broadcasted_iota(jnp.int32, (rows, m * h), 1) & (h - 1)
    own = row_h == lane_h
    for b in range(bb):
        k2 = k_ref[b].reshape(m * h, d).astype(BF16)
        v2 = v_ref[b].reshape(m * h, d).astype(BF16)
        s = jnp.where(own, _dot_nt(q_ref[b], k2) * scale, MASK_VALUE)
        p = jnp.exp(s - jnp.max(s, axis=-1, keepdims=True))
        l = jnp.sum(p, axis=-1, keepdims=True)
        o_ref[b] = (_dot(p.astype(BF16), v2) / l).astype(o_ref.dtype)


def _mem_sample(qm, ck, cv, *, bb=4):
    b, rows, d = qm.shape
    _, m, h, _ = ck.shape
    assert h & (h - 1) == 0 and b % bb == 0
    return pl.pallas_call(
        functools.partial(_mem_sample_body, bb=bb, scale=MEM_HD ** -0.5),
        grid=(b // bb,),
        in_specs=[pl.BlockSpec((bb, rows, d), lambda i: (i, 0, 0)),
                  pl.BlockSpec((bb, m, h, d), lambda i: (i, 0, 0, 0)),
                  pl.BlockSpec((bb, m, h, d), lambda i: (i, 0, 0, 0))],
        out_specs=pl.BlockSpec((bb, rows, d), lambda i: (i, 0, 0)),
        out_shape=jax.ShapeDtypeStruct((b, rows, d), BF16),
        compiler_params=_cparams(("parallel",)),
        name="mem_sample_attention",
    )(qm, ck, cv)


def _merge_body(x_ref, of_ref, op_ref, om_ref, wf_ref, wp_ref, wm_ref, g0_ref, g1_ref, g2_ref,
                o_ref, *, fox_transposed):
    x = x_ref[...]
    fox = (_dot_tn if fox_transposed else _dot)(of_ref[...], wf_ref[...])
    merged = (jax.nn.sigmoid(_dot_nt(x, g0_ref[...])) * fox
              + jax.nn.sigmoid(_dot_nt(x, g1_ref[...])) * _dot(op_ref[...], wp_ref[...])
              + jax.nn.sigmoid(_dot_nt(x, g2_ref[...])) * _dot(om_ref[...], wm_ref[...]))
    o_ref[...] = merged.astype(o_ref.dtype)


def _merge(x, o_fox, o_pool, o_mem, w_f, w_p, w_m, w_gates_t, *, fox_transposed, tm=512, tn=512):
    m, d = x.shape
    kb = o_pool.shape[1]
    n = w_f.shape[1]
    tm = min(tm, m)
    nb = n // tn
    act = pl.BlockSpec((tm, kb), lambda i, j: (i, 0))
    act_fox = pl.BlockSpec((kb, tm), lambda i, j: (0, i)) if fox_transposed else act
    wgt = pl.BlockSpec((kb, tn), lambda i, j: (0, j))
    gate = lambda g: pl.BlockSpec((tn, d), lambda i, j: (g * nb + j, 0))
    return pl.pallas_call(
        functools.partial(_merge_body, fox_transposed=fox_transposed), grid=(m // tm, nb),
        in_specs=[pl.BlockSpec((tm, d), lambda i, j: (i, 0)), act_fox, act, act, wgt, wgt, wgt,
                  gate(0), gate(1), gate(2)],
        out_specs=pl.BlockSpec((tm, tn), lambda i, j: (i, j)),
        out_shape=jax.ShapeDtypeStruct((m, n), BF16),
        compiler_params=_cparams(("parallel", "arbitrary")),
        name="branch_merge",
    )(x, o_fox, o_pool, o_mem, w_f, w_p, w_m, w_gates_t, w_gates_t, w_gates_t)


def _ep_residual_ln(acc, x, g, b, *, alpha):
    return (_layer_norm(alpha * x + acc, g, b),)


def _ffn_body(h_ref, wu_ref, wd_ref, g_ref, b_ref, o_ref, hb_ref, *, alpha):
    k = pl.program_id(1)

    @pl.when(k == 0)
    def _():
        hb_ref[...] = h_ref[...].astype(BF16)
        o_ref[...] = jnp.zeros_like(o_ref)

    a = jnp.square(jnp.maximum(_dot_nt(hb_ref[...], wu_ref[...]), 0.0))
    o_ref[...] += _dot(a.astype(BF16), wd_ref[...])

    @pl.when(k == pl.num_programs(1) - 1)
    def _():
        o_ref[...] = _layer_norm(alpha * h_ref[...] + o_ref[...], g_ref[...], b_ref[...])


def _ffn(h, w_up_t, w_down, g, b, *, alpha, tm=1024, tf=512):
    m, d = h.shape
    dff = w_up_t.shape[0]
    tm = min(tm, m)
    return pl.pallas_call(
        functools.partial(_ffn_body, alpha=alpha),
        grid=(m // tm, dff // tf),
        in_specs=[pl.BlockSpec((tm, d), lambda i, k: (i, 0)),
                  pl.BlockSpec((tf, d), lambda i, k: (k, 0)),
                  pl.BlockSpec((tf, d), lambda i, k: (k, 0)),
                  pl.BlockSpec((1, d), lambda i, k: (0, 0)),
                  pl.BlockSpec((1, d), lambda i, k: (0, 0))],
        out_specs=pl.BlockSpec((tm, d), lambda i, k: (i, 0)),
        out_shape=jax.ShapeDtypeStruct((m, d), F32),
        scratch_shapes=[pltpu.VMEM((tm, d), BF16)],
        compiler_params=_cparams(("parallel", "arbitrary")),
        name="ffn_ln",
    )(h, w_up_t, w_down, g, b)


def _mix(x, x_bf, o_fox, o_pool, o_mem, w, *, alpha, fox_transposed):
    d = x.shape[1]
    merged = _merge(x_bf, o_fox, o_pool, o_mem, w["br_fox"], w["br_pool"], w["br_mem"],
                    w["gates_t"], fox_transposed=fox_transposed)
    row = lambda i, j: (0, 0)
    h, = _mm(merged, w["out"], [(F32, False)], functools.partial(_ep_residual_ln, alpha=alpha),
             extras=[(x, (min(512, x.shape[0]), d), lambda i, j: (i, 0)),
                     (w["ln1_g"], (1, d), row), (w["ln1_b"], (1, d), row)],
             tm=512, tn=d, name="out_proj_ln")
    return h


def kernel(x_prompt, x_sample, mem_prompt, cache_fox_k, cache_fox_v, cache_fox_logf, state_pool,
           cache_mem_k, cache_mem_v, page_table, w_in, b_fgate, w_pool, pool_scale, w_br_fox,
           w_br_pool, w_br_mem, w_mem_kv, w_out, ln1_g, ln1_b, w_up, w_down, ln2_g, ln2_b):
    batch, seq, d_model = x_prompt.shape
    dec_batch, dec_seq, _ = x_sample.shape
    depth = w_in.shape[0]
    n_phys, page_size = cache_fox_k.shape[1], cache_fox_k.shape[2]
    n_pages = page_table.shape[1]
    past = n_pages * page_size
    fox_w = FOX_HEADS * FOX_HD
    pool_w = len(POOL_WINDOWS) * POOL_GW
    mem_w = MEM_HEADS * MEM_HD
    mem_tokens = mem_prompt.shape[1]
    assert batch == 1 and dec_seq == FOX_HEADS == 8
    assert fox_w == pool_w == mem_w
    alpha = (2 * depth) ** 0.25
    nat, tr = False, True

    hp = x_prompt.reshape(seq, d_model)
    hs = x_sample.reshape(dec_batch * dec_seq, d_model)
    seg = POOL_HALO + dec_seq
    pos_p = jnp.arange(seq, dtype=F32)[:, None]
    pos_s = jnp.tile(jnp.concatenate([jnp.full((POOL_HALO,), 1e9, F32),
                                      past + jnp.arange(dec_seq, dtype=F32)]), dec_batch)[:, None]

    outs = {n: [] for n in ("pk", "pv", "pf", "pp", "pmk", "pmv", "sk", "sv", "sf", "sp")}
    for l in range(depth):
        w_t = jnp.swapaxes(w_in[l], 0, 1)
        r_q, r_k, r_v, r_f = 0, fox_w, 2 * fox_w, 3 * fox_w
        r_u = r_f + FOX_HEADS
        r_qm = r_u + pool_w
        r_g = r_qm + mem_w
        b_f = jnp.pad(b_fgate[l], (0, 128 - FOX_HEADS))[None, :]
        w = dict(gates_t=w_t[r_g:].astype(BF16),
                 br_fox=w_br_fox[l].astype(BF16), br_pool=w_br_pool[l].astype(BF16),
                 br_mem=w_br_mem[l].astype(BF16), out=w_out[l].astype(BF16),
                 up_t=w_up[l].T.astype(BF16), down=w_down[l].astype(BF16),
                 ln1_g=ln1_g[l][None, :], ln1_b=ln1_b[l][None, :],
                 ln2_g=ln2_g[l][None, :], ln2_b=ln2_b[l][None, :])
        wp_bf = w_pool[l].astype(BF16)
        psc = pool_scale[l][None, :]
        logf_of = functools.partial(_mm, w=w_t, outs=[(F32, nat)], epilogue=_ep_log_sigmoid,
                                    extras=[(b_f, (1, 128), lambda i, j: (0, 0))], tn=128, row0=r_f,
                                    ncols=128, name="proj_logf")

        x_bf, q2T = _mm(hp, w_t, [(BF16, tr)], functools.partial(_ep_scaled_t, scale=FOX_HD ** -0.5 * LOG2E),
                        row0=r_q, ncols=fox_w, tm=512, emit_x=True, name="proj_q")
        k, k_bf = _mm(x_bf, w_t, [(F32, nat), (BF16, nat)], _ep_dup, row0=r_k, ncols=fox_w, name="proj_k")
        v, vT = _mm(x_bf, w_t, [(F32, nat), (BF16, tr)], _ep_with_t, row0=r_v, ncols=fox_w, name="proj_v")
        u, = _mm(x_bf, w_t, [(F32, nat)], _ep_identity, row0=r_u, ncols=pool_w, name="proj_u")
        qm, = _mm(x_bf, w_t, [(BF16, nat)], _ep_identity, row0=r_qm, ncols=mem_w, name="proj_qm")
        logf = logf_of(x_bf)[0][:, :FOX_HEADS]
        c2_rows = _cumsum_lanes(logf.T, scale=LOG2E)
        o_foxT = _fox_prompt(q2T, k_bf, vT, c2_rows)
        o_pool = _pool(u, pos_p, wp_bf, psc, tm=1024, zero_first=True)
        mkv, mkv_bf = _mm(mem_prompt.reshape(mem_tokens, d_model), w_mem_kv[l], [(F32, nat), (BF16, nat)],
                          _ep_dup, name="mem_kv")
        o_mem = _mem_prompt(qm, mkv_bf[:, :mem_w], mkv_bf[:, mem_w:])
        outs["pk"].append(k.reshape(batch, seq, FOX_HEADS, FOX_HD))
        outs["pv"].append(v.reshape(batch, seq, FOX_HEADS, FOX_HD))
        outs["pf"].append(logf.reshape(batch, seq, FOX_HEADS))
        outs["pp"].append(u[-POOL_STATE:].reshape(batch, POOL_STATE, pool_w))
        outs["pmk"].append(mkv[:, :mem_w].reshape(batch, mem_tokens, MEM_HEADS, MEM_HD))
        outs["pmv"].append(mkv[:, mem_w:].reshape(batch, mem_tokens, MEM_HEADS, MEM_HD))
        h_p = _mix(hp, x_bf, o_foxT, o_pool, o_mem, w, alpha=alpha, fox_transposed=True)

        x_bf, q = _mm(hs, w_t, [(BF16, nat)], functools.partial(_ep_scaled, scale=FOX_HD ** -0.5 * LOG2E),
                      row0=r_q, ncols=fox_w, tm=512, emit_x=True, name="proj_q")
        k, = _mm(x_bf, w_t, [(F32, nat)], _ep_identity, row0=r_k, ncols=fox_w, name="proj_k")
        v, = _mm(x_bf, w_t, [(F32, nat)], _ep_identity, row0=r_v, ncols=fox_w, name="proj_v")
        u, = _mm(x_bf, w_t, [(F32, nat)], _ep_identity, row0=r_u, ncols=pool_w, name="proj_u")
        qm, = _mm(x_bf, w_t, [(BF16, nat)], _ep_identity, row0=r_qm, ncols=mem_w, name="proj_qm")
        logf = logf_of(x_bf)[0][:, :FOX_HEADS]
        rows = dec_seq * FOX_HEADS
        hp, o_fox = _ffn_fox(
            h_p, w["up_t"], w["down"], w["ln2_g"], w["ln2_b"],
            q.reshape(dec_batch, rows, FOX_HD), k.reshape(dec_batch, rows, FOX_HD),
            v.reshape(dec_batch, rows, FOX_HD), logf.reshape(dec_batch, 1, rows),
            cache_fox_k[l].reshape(n_phys, page_size * FOX_HEADS, FOX_HD),
            cache_fox_v[l].reshape(n_phys, page_size * FOX_HEADS, FOX_HD),
            _page_logf_sums(cache_fox_logf[l]), page_table, alpha=alpha)
        o_fox = o_fox.reshape(dec_batch * dec_seq, fox_w)
        u3 = u.reshape(dec_batch, dec_seq, pool_w)
        state = state_pool[l].astype(F32)
        u_ext = jnp.concatenate([jnp.zeros((dec_batch, POOL_HALO - POOL_STATE, pool_w), F32), state, u3],
                                axis=1).reshape(dec_batch * seg, pool_w)
        o_pool = _pool(u_ext, pos_s, wp_bf, psc, tm=32 * seg, zero_first=False)
        o_pool = o_pool.reshape(dec_batch, seg, pool_w)[:, POOL_HALO:].reshape(dec_batch * dec_seq, pool_w)
        o_mem = _mem_sample(qm.reshape(dec_batch, dec_seq * MEM_HEADS, MEM_HD), cache_mem_k[l],
                            cache_mem_v[l]).reshape(dec_batch * dec_seq, mem_w)
        outs["sk"].append(k.reshape(dec_batch, dec_seq, FOX_HEADS, FOX_HD))
        outs["sv"].append(v.reshape(dec_batch, dec_seq, FOX_HEADS, FOX_HD))
        outs["sf"].append(logf.reshape(dec_batch, dec_seq, FOX_HEADS))
        outs["sp"].append(jnp.concatenate([state, u3], axis=1)[:, -POOL_STATE:])
        h_s = _mix(hs, x_bf, o_fox, o_pool, o_mem, w, alpha=alpha, fox_transposed=False)
        hs = _ffn(h_s, w["up_t"], w["down"], w["ln2_g"], w["ln2_b"], alpha=alpha)

    st = lambda n: jnp.stack(outs[n])
    return (hp.reshape(batch, seq, d_model), hs.reshape(dec_batch, dec_seq, d_model),
            st("pk"), st("pv"), st("pf"), st("pp"), st("pmk"), st("pmv"),
            st("sk"), st("sv"), st("sf"), st("sp"))
```

```python
import functools

import numpy as np

import jax
import jax.numpy as jnp
from jax import lax
from jax.experimental import pallas as pl
from jax.experimental.pallas import tpu as pltpu

F32 = jnp.float32
BF16 = jnp.bfloat16

FOX_HEADS = 8
FOX_HD = 128
POOL_WINDOWS = (2, 4, 8, 16)
POOL_GW = 256
POOL_STATE = max(POOL_WINDOWS) - 1
POOL_HALO = 16
MEM_HEADS = 4
MEM_HD = 256
LN_EPS = 1e-5
MASK_VALUE = -1e30
LOG2E = 1.4426950408889634
FOX_EXT = 128
FOX_ONES = 16

V7X_VMEM_LIMIT_BYTES = 56 * 1024 * 1024


def _cparams(semantics):
    return pltpu.CompilerParams(dimension_semantics=semantics,
                                vmem_limit_bytes=V7X_VMEM_LIMIT_BYTES)


def _dot(a, b):
    return jnp.dot(a, b, preferred_element_type=F32)


def _dot_nt(a, b):
    return lax.dot_general(a, b, (((1,), (1,)), ((), ())), preferred_element_type=F32)


def _dot_tn(a, b):
    return lax.dot_general(a, b, (((0,), (0,)), ((), ())), preferred_element_type=F32)


def _split3(x):
    hi = x.astype(BF16)
    r1 = x - hi.astype(F32)
    mid = r1.astype(BF16)
    lo = (r1 - mid.astype(F32)).astype(BF16)
    return hi, mid, lo


def _split3_dot(x, t):
    hi, mid, lo = _split3(x)
    return _dot(hi, t) + _dot(mid, t) + _dot(lo, t)


def _layer_norm(z, g, b):
    mu = jnp.mean(z, axis=-1, keepdims=True)
    zc = z - mu
    var = jnp.mean(zc * zc, axis=-1, keepdims=True)
    return zc * lax.rsqrt(var + LN_EPS) * g + b


def _mm_body(*refs, epilogue, n_in, emit_x, w_rows):
    x_ref, w_ref = refs[0], refs[1]
    extras = refs[2:n_in]
    outs = list(refs[n_in:])
    x = x_ref[...].astype(BF16)
    if emit_x:
        outs.pop(0)[...] = x
    acc = (_dot_nt if w_rows else _dot)(x, w_ref[...].astype(BF16))
    res = epilogue(acc, *(e[...] for e in extras))
    for o, r in zip(outs, res):
        o[...] = r.astype(o.dtype)


def _mm(x, w, outs, epilogue, *, extras=(), tm=1024, tn=1024, row0=None, ncols=None, emit_x=False, name):
    m, k = x.shape
    n = w.shape[1] if ncols is None else ncols
    tm, tn = min(tm, m), min(tn, n)
    assert m % tm == 0 and n % tn == 0 and (n == tn or not emit_x)
    if row0 is None:
        w_spec = pl.BlockSpec((k, tn), lambda i, j: (0, j))
    else:
        assert row0 % 8 == 0 and tn % 8 == 0
        w_spec = pl.BlockSpec((pl.Element(tn), pl.Element(k)),
                              lambda i, j: (pl.multiple_of(row0 + j * tn, 8), 0))
    in_specs = [pl.BlockSpec((tm, k), lambda i, j: (i, 0)), w_spec]
    in_specs += [pl.BlockSpec(bs, im) for _, bs, im in extras]
    out_specs = [pl.BlockSpec((tn, tm), lambda i, j: (j, i)) if t else
                 pl.BlockSpec((tm, tn), lambda i, j: (i, j)) for _, t in outs]
    out_shape = [jax.ShapeDtypeStruct((n, m) if t else (m, n), dt) for dt, t in outs]
    if emit_x:
        out_specs.insert(0, pl.BlockSpec((tm, k), lambda i, j: (i, 0)))
        out_shape.insert(0, jax.ShapeDtypeStruct((m, k), BF16))
    return pl.pallas_call(
        functools.partial(_mm_body, epilogue=epilogue, n_in=2 + len(extras), emit_x=emit_x,
                          w_rows=row0 is not None),
        grid=(m // tm, n // tn),
        in_specs=in_specs, out_specs=out_specs, out_shape=out_shape,
        compiler_params=_cparams(("parallel", "arbitrary")),
        name=name,
    )(x, w, *(a for a, _, _ in extras))


def _ep_identity(acc):
    return (acc,)


def _ep_dup(acc):
    return (acc, acc)


def _ep_scaled(acc, *, scale):
    return (acc * scale,)


def _ep_scaled_t(acc, *, scale):
    return ((acc * scale).T,)


def _ep_with_t(acc):
    return (acc, acc.T)


def _ep_log_sigmoid(acc, bias):
    z = acc + bias
    return (jnp.minimum(z, 0.0) - jnp.log1p(jnp.exp(-jnp.abs(z))),)


def _cumsum_body(x_ref, tri_ref, o_ref, *, chunk, scale):
    n = x_ref.shape[1]
    carry = jnp.zeros((x_ref.shape[0], 1), F32)
    for c in range(n // chunk):
        cs = _split3_dot(x_ref[:, c * chunk:(c + 1) * chunk], tri_ref[...]) + carry
        o_ref[:, c * chunk:(c + 1) * chunk] = cs * scale
        carry = cs[:, chunk - 1:chunk]


def _cumsum_lanes(x, *, scale, chunk=512):
    rows, n = x.shape
    tri = jnp.asarray(np.triu(np.ones((chunk, chunk), np.float32)), BF16)
    return pl.pallas_call(
        functools.partial(_cumsum_body, chunk=chunk, scale=scale),
        out_shape=jax.ShapeDtypeStruct((rows, n), F32),
        name="logf_cumsum",
    )(x, tri)


def _bias_rows(c, ones_first, n):
    sub = lax.broadcasted_iota(jnp.int32, (n, c.shape[1]), 0)
    c0, o0 = (3, 0) if ones_first else (0, 3)
    rows = jnp.where((sub >= o0) & (sub < o0 + 3), 1.0, 0.0).astype(BF16)
    for t, piece in enumerate(_split3(c)):
        rows = jnp.where(sub == c0 + t, piece, rows)
    return rows


def _fox_prompt_body(qT_ref, k_ref, vT_ref, c_ref, oT_ref, kext, s0, s1, m_scr, acc_scr, *, tq):
    i = pl.program_id(1)
    tk = tq

    @pl.when(i == 0)
    def _():
        eye = (lax.broadcasted_iota(jnp.int32, (FOX_ONES, FOX_EXT), 0)
               == lax.broadcasted_iota(jnp.int32, (FOX_ONES, FOX_EXT), 1)).astype(BF16)
        for c in range(k_ref.shape[0] // tk):
            rows = _bias_rows(-c_ref[0, :, c * tk:(c + 1) * tk], True, FOX_ONES)
            kext[c * tk:(c + 1) * tk, :] = _dot_tn(rows, eye).astype(BF16)

    last = pl.multiple_of((i + 1) * tq - 128, 128)
    cq = c_ref[0, :, pl.ds(last, 128)][:, 127:128]
    qa = jnp.concatenate([qT_ref[...], jnp.broadcast_to(_bias_rows(cq, False, FOX_EXT), (FOX_EXT, tq))],
                         axis=0)
    ones = jnp.ones((FOX_ONES, tk), BF16)

    def scores(j, dst):
        ks = pl.multiple_of(j * tk, tk)
        ka = jnp.concatenate([k_ref[pl.ds(ks, tk), :], kext[pl.ds(ks, tk), :]], axis=1)
        dst[...] = _dot(ka, qa)

    def absorb(j, src, diag):
        ks = pl.multiple_of(j * tk, tk)
        sT = src[...]
        if diag:
            key = lax.broadcasted_iota(jnp.int32, (tk, tq), 0)
            qry = lax.broadcasted_iota(jnp.int32, (tk, tq), 1)
            sT = jnp.where(key <= qry, sT, MASK_VALUE)
        m = m_scr[...]
        m_new = jnp.maximum(m, jnp.max(sT, axis=0, keepdims=True))
        p = jnp.exp2(sT - m_new).astype(BF16)
        m_scr[...] = m_new
        va = jnp.concatenate([vT_ref[:, pl.ds(ks, tk)], ones], axis=0)
        acc_scr[...] = jnp.exp2(m - m_new) * acc_scr[...] + _dot(va, p)

    def finish(src):
        absorb(i, src, True)
        oT_ref[...] = (acc_scr[:FOX_HD, :] / acc_scr[FOX_HD:FOX_HD + 1, :]).astype(oT_ref.dtype)

    m_scr[...] = jnp.full(m_scr.shape, MASK_VALUE, F32)
    acc_scr[...] = jnp.zeros(acc_scr.shape, F32)
    scores(0, s0)

    def pair(t, carry):
        j = 2 * t
        scores(j + 1, s1)
        absorb(j, s0, False)
        scores(j + 2, s0)
        absorb(j + 1, s1, False)
        return carry

    lax.fori_loop(0, i // 2, pair, 0)

    @pl.when(i % 2 == 0)
    def _():
        finish(s0)

    @pl.when(i % 2 == 1)
    def _():
        scores(i, s1)
        absorb(i - 1, s0, False)
        finish(s1)


def _fox_prompt(q2T, k, vT, c2_rows, *, tq=1024):
    h, s = c2_rows.shape
    return pl.pallas_call(
        functools.partial(_fox_prompt_body, tq=tq),
        grid=(h, s // tq),
        in_specs=[pl.BlockSpec((FOX_HD, tq), lambda g, i: (g, i)),
                  pl.BlockSpec((s, FOX_HD), lambda g, i: (0, g)),
                  pl.BlockSpec((FOX_HD, s), lambda g, i: (g, 0)),
                  pl.BlockSpec((1, 1, s), lambda g, i: (g, 0, 0))],
        out_specs=pl.BlockSpec((FOX_HD, tq), lambda g, i: (g, i)),
        out_shape=jax.ShapeDtypeStruct((h * FOX_HD, s), BF16),
        scratch_shapes=[pltpu.VMEM((s, FOX_EXT), BF16), pltpu.VMEM((tq, tq), F32), pltpu.VMEM((tq, tq), F32),
                        pltpu.VMEM((1, tq), F32), pltpu.VMEM((FOX_HD + FOX_ONES, tq), F32)],
        compiler_params=_cparams(("parallel", "arbitrary")),
        name="fox_prompt_attention",
    )(q2T, k, vT, c2_rows[:, None, :])


def _ffn_fox_body(pt_ref, h_ref, wu_ref, wd_ref, g_ref, b_ref,
                  q_ref, kn_ref, vn_ref, lfn_ref, msk_ref, t4_ref, t5_ref, *rest, n_half, alpha):
    k_pages = rest[:n_half]
    v_pages = rest[n_half:2 * n_half]
    lf_pages = rest[2 * n_half:3 * n_half]
    y_ref, o_ref, hb_ref, s_scr, mbq_scr, m_scr, l_scr, acc_scr, later_scr = rest[3 * n_half:]
    kc = pl.program_id(1)
    pt0 = ((pl.program_id(0) * (pl.num_programs(1) // 2) + kc // 2) * 2 + (1 - kc % 2)) * n_half
    rows = q_ref.shape[1]
    pk = k_pages[0].shape[1]
    nh = FOX_HEADS
    q = q_ref[0]
    first = kc % 2 == 0

    @pl.when(kc == 0)
    def _():
        hb_ref[...] = h_ref[...].astype(BF16)
        y_ref[...] = jnp.zeros_like(y_ref)

    @pl.when(first)
    def _():
        r = lfn_ref[0] * LOG2E
        bq = _split3_dot(jnp.broadcast_to(r, (rows, rows)) * msk_ref[...], t4_ref[...])
        cn = _split3_dot(jnp.broadcast_to(r, (8, rows)), t5_ref[...])[0:1]
        row_h = lax.broadcasted_iota(jnp.int32, (rows, 128), 0) & (nh - 1)
        lane_h = lax.broadcasted_iota(jnp.int32, (rows, 128), 1) & (nh - 1)
        same_head = row_h == lane_h
        mbq = jnp.where(same_head, bq, MASK_VALUE)
        mbq_scr[...] = jnp.concatenate([mbq] * (pk // 128), axis=1)
        row_t = lax.broadcasted_iota(jnp.int32, (rows, rows), 0) >> 3
        lane_t = lax.broadcasted_iota(jnp.int32, (rows, rows), 1) >> 3
        ok_new = same_head[:, :rows] & (lane_t <= row_t)
        s_new = jnp.where(ok_new, _dot_nt(q, kn_ref[0].astype(BF16)) + (bq[:, :rows] - cn), MASK_VALUE)
        m0 = jnp.max(s_new, axis=-1, keepdims=True)
        p_new = jnp.exp2(s_new - m0)
        m_scr[...] = m0
        l_scr[...] = jnp.sum(p_new, axis=-1, keepdims=True)
        acc_scr[...] = _dot(p_new.astype(BF16), vn_ref[0].astype(BF16))
        later_scr[...] = jnp.zeros_like(later_scr)

    later = later_scr[...]
    m_old = m_scr[...]
    m_new = m_old
    for p in reversed(range(n_half)):
        lf = lf_pages[p][pl.ds(pt_ref[pt0 + p] & 7, 1), :]
        tot = lf[:, pk:]
        d = jnp.concatenate([later + tot] * (pk // 128), axis=1) - lf[:, :pk]
        later = later + tot
        s = _dot_nt(q, k_pages[p][0].astype(BF16)) + (mbq_scr[...] + d)
        s_scr[:, p * pk:(p + 1) * pk] = s
        m_new = jnp.maximum(m_new, jnp.max(s, axis=-1, keepdims=True))
    later_scr[...] = later

    a = jnp.square(jnp.maximum(_dot(hb_ref[...], wu_ref[...]), 0.0))
    y_ref[...] += _dot(a.astype(BF16), wd_ref[...])

    scale_old = jnp.exp2(m_old - m_new)
    l = scale_old * l_scr[...]
    acc = scale_old * acc_scr[...]
    for p in range(n_half):
        pr = jnp.exp2(s_scr[:, p * pk:(p + 1) * pk] - m_new)
        l = l + jnp.sum(pr, axis=-1, keepdims=True)
        acc = acc + _dot(pr.astype(BF16), v_pages[p][0].astype(BF16))
    m_scr[...] = m_new
    l_scr[...] = l
    acc_scr[...] = acc

    @pl.when(jnp.logical_not(first))
    def _():
        o_ref[0] = (acc / l).astype(o_ref.dtype)

    @pl.when(kc == pl.num_programs(1) - 1)
    def _():
        y_ref[...] = _layer_norm(alpha * h_ref[...] + y_ref[...], g_ref[...], b_ref[...])


def _ffn_fox(h, w_up, w_down, g, b, q, k_new, v_new, lf_new, cache_k, cache_v, lf_cache, page_table,
             *, alpha, tm=512, tf=512):
    m, d = h.shape
    dff = w_up.shape[1]
    nb, rows, hd = q.shape
    n_pages = page_table.shape[1]
    n_half = n_pages // 2
    pk = cache_k.shape[1]
    ni, nk = m // tm, dff // tf
    per_i = nk // 2
    assert rows == 64 and FOX_HEADS == 8 and pk % 128 == 0 and n_pages % 2 == 0
    assert m % tm == 0 and dff % tf == 0 and nk % 2 == 0 and ni * per_i == nb
    assert lf_cache.shape[0] % 8 == 0

    rr = np.arange(rows)
    tt, hh = rr // FOX_HEADS, rr % FOX_HEADS
    lane = np.arange(128)
    msk = jnp.asarray((tt[None, :] <= tt[:, None]).astype(np.float32))
    t4 = jnp.asarray((hh[:, None] == (lane % FOX_HEADS)[None, :]).astype(np.float32), BF16)
    t5 = jnp.asarray(((hh[:, None] == hh[None, :]) & (tt[:, None] <= tt[None, :])).astype(np.float32), BF16)

    batch = lambda i, k: i * per_i + k // 2

    def page(i, k, pt, p):
        return pt[batch(i, k) * n_pages + (1 - k % 2) * n_half + p]

    def page_map(p):
        return lambda i, k, pt: (page(i, k, pt, p), 0, 0)

    def page_sums_map(p):
        return lambda i, k, pt: (page(i, k, pt, p) // 8, 0)

    per_b = lambda i, k, pt: (batch(i, k), 0, 0)
    const = lambda i, k, pt: (0, 0)
    in_specs = [pl.BlockSpec((tm, d), lambda i, k, pt: (i, 0)),
                pl.BlockSpec((d, tf), lambda i, k, pt: (0, k)),
                pl.BlockSpec((tf, d), lambda i, k, pt: (k, 0)),
                pl.BlockSpec((1, d), const),
                pl.BlockSpec((1, d), const),
                pl.BlockSpec((1, rows, hd), per_b),
                pl.BlockSpec((1, rows, hd), per_b),
                pl.BlockSpec((1, rows, hd), per_b),
                pl.BlockSpec((1, 1, rows), per_b),
                pl.BlockSpec((rows, rows), const),
                pl.BlockSpec((rows, 128), const),
                pl.BlockSpec((rows, rows), const)]
    in_specs += [pl.BlockSpec((1, pk, hd), page_map(p)) for p in range(n_half)]
    in_specs += [pl.BlockSpec((1, pk, hd), page_map(p)) for p in range(n_half)]
    in_specs += [pl.BlockSpec((8, pk + 128), page_sums_map(p)) for p in range(n_half)]
    grid_spec = pltpu.PrefetchScalarGridSpec(
        num_scalar_prefetch=1, grid=(ni, nk), in_specs=in_specs,
        out_specs=[pl.BlockSpec((tm, d), lambda i, k, pt: (i, 0)),
                   pl.BlockSpec((1, rows, hd), per_b)],
        scratch_shapes=[pltpu.VMEM((tm, d), BF16), pltpu.VMEM((rows, n_half * pk), F32),
                        pltpu.VMEM((rows, pk), F32), pltpu.VMEM((rows, 1), F32), pltpu.VMEM((rows, 1), F32),
                        pltpu.VMEM((rows, hd), F32), pltpu.VMEM((1, 128), F32)])
    return pl.pallas_call(
        functools.partial(_ffn_fox_body, n_half=n_half, alpha=alpha),
        grid_spec=grid_spec,
        out_shape=[jax.ShapeDtypeStruct((m, d), F32), jax.ShapeDtypeStruct((nb, rows, hd), BF16)],
        compiler_params=_cparams(("arbitrary", "arbitrary")),
        name="ffn_ln_fox_sample",
    )(page_table.reshape(-1), h, w_up, w_down, g, b, q, k_new, v_new, lf_new, msk, t4, t5,
      *([cache_k] * n_half), *([cache_v] * n_half), *([lf_cache] * n_half))


def _page_logf_sums(cache_logf):
    p, page, h = cache_logf.shape
    n = page * h
    pos, head = np.arange(n) // h, np.arange(n) % h
    prefix = (head[:, None] == head[None, :]) & (pos[:, None] <= pos[None, :])
    total = head[:, None] == (np.arange(128) % h)[None, :]
    t = jnp.asarray(np.concatenate([prefix, total], axis=1).astype(np.float32), BF16)
    tm = next(c for c in (512, 256, 128, 64, 32, 16, 8) if p % c == 0)

    def body(x_ref, t_ref, o_ref):
        o_ref[...] = _split3_dot(x_ref[...], t_ref[...]) * LOG2E

    return pl.pallas_call(
        body, grid=(p // tm,),
        in_specs=[pl.BlockSpec((tm, n), lambda i: (i, 0)),
                  pl.BlockSpec((n, n + 128), lambda i: (0, 0))],
        out_specs=pl.BlockSpec((tm, n + 128), lambda i: (i, 0)),
        out_shape=jax.ShapeDtypeStruct((p, n + 128), F32),
        compiler_params=_cparams(("parallel",)),
        name="page_logf_sums",
    )(cache_logf.reshape(p, n), t)


def _pool_body(u_ref, halo_ref, pos_ref, wp_ref, sc_ref, o_ref, *, zero_first):
    u = u_ref[...]
    halo = halo_ref[...]
    if zero_first:
        halo = jnp.where(pl.program_id(0) == 0, 0.0, halo)
    pos1 = pos_ref[...] + 1.0
    outs = []
    for g, w in enumerate(POOL_WINDOWS):
        sl = slice(g * POOL_GW, (g + 1) * POOL_GW)
        s = jnp.concatenate([halo[:, sl], u[:, sl]], axis=0)
        sh = 1
        while sh < w:
            s = s + pltpu.roll(s, sh, 0)
            sh *= 2
        d = s[POOL_HALO:] / jnp.minimum(float(w), pos1) - u[:, sl]
        outs.append(_dot(d.astype(BF16), wp_ref[g]))
    o_ref[...] = (jnp.concatenate(outs, axis=1) * sc_ref[...]).astype(o_ref.dtype)


def _pool(u, pos, w_pool, pool_scale, *, tm, zero_first):
    assert all(w & (w - 1) == 0 for w in POOL_WINDOWS) and tm % POOL_HALO == 0
    r, width = u.shape
    hb = tm // POOL_HALO
    return pl.pallas_call(
        functools.partial(_pool_body, zero_first=zero_first),
        grid=(r // tm,),
        in_specs=[pl.BlockSpec((tm, width), lambda i: (i, 0)),
                  pl.BlockSpec((POOL_HALO, width), lambda i: (jnp.maximum(i * hb - 1, 0), 0)),
                  pl.BlockSpec((tm, 1), lambda i: (i, 0)),
                  pl.BlockSpec(w_pool.shape, lambda i: (0, 0, 0)),
                  pl.BlockSpec((1, width), lambda i: (0, 0))],
        out_specs=pl.BlockSpec((tm, width), lambda i: (i, 0)),
        out_shape=jax.ShapeDtypeStruct((r, width), BF16),
        compiler_params=_cparams(("arbitrary",)),
        name="pool_mixer",
    )(u, u, pos, w_pool, pool_scale)


def _mem_prompt_body(q_ref, k_ref, v_ref, o_ref, *, scale):
    for h in range(MEM_HEADS):
        sl = slice(h * MEM_HD, (h + 1) * MEM_HD)
        s = _dot_nt(q_ref[:, sl], k_ref[:, sl]) * scale
        p = jnp.exp(s - jnp.max(s, axis=-1, keepdims=True))
        l = jnp.sum(p, axis=-1, keepdims=True)
        o_ref[:, sl] = (_dot(p.astype(BF16), v_ref[:, sl]) / l).astype(o_ref.dtype)


def _mem_prompt(qm, mk, mv, *, tq=1024):
    s, width = qm.shape
    m = mk.shape[0]
    return pl.pallas_call(
        functools.partial(_mem_prompt_body, scale=MEM_HD ** -0.5),
        grid=(s // tq,),
        in_specs=[pl.BlockSpec((tq, width), lambda i: (i, 0)),
                  pl.BlockSpec((m, width), lambda i: (0, 0)),
                  pl.BlockSpec((m, width), lambda i: (0, 0))],
        out_specs=pl.BlockSpec((tq, width), lambda i: (i, 0)),
        out_shape=jax.ShapeDtypeStruct((s, width), BF16),
        compiler_params=_cparams(("parallel",)),
        name="mem_prompt_attention",
    )(qm, mk, mv)


def _merge_body(x_ref, of_ref, op_ref, om_ref, wf_ref, wp_ref, wm_ref, g0_ref, g1_ref, g2_ref, *rest,
                fox_transposed, rider_bb):
    if rider_bb:
        mq_ref, mk_ref, mv_ref, o_ref, mo_ref = rest
        rows = mq_ref.shape[1]
        mt, nh, hd = mk_ref.shape[1:]
        row_h = lax.broadcasted_iota(jnp.int32, (rows, mt * nh), 0) & (nh - 1)
        lane_h = lax.broadcasted_iota(jnp.int32, (rows, mt * nh), 1) & (nh - 1)
        own = row_h == lane_h
        probs = []
        for b in range(rider_bb):
            k2 = mk_ref[b].reshape(mt * nh, hd).astype(BF16)
            s = jnp.where(own, _dot_nt(mq_ref[b], k2) * hd ** -0.5, MASK_VALUE)
            p = jnp.exp(s - jnp.max(s, axis=-1, keepdims=True))
            probs.append((p.astype(BF16), jnp.sum(p, axis=-1, keepdims=True)))
    else:
        (o_ref,) = rest

    x = x_ref[...]
    fox = (_dot_tn if fox_transposed else _dot)(of_ref[...], wf_ref[...])
    merged = (jax.nn.sigmoid(_dot_nt(x, g0_ref[...])) * fox
              + jax.nn.sigmoid(_dot_nt(x, g1_ref[...])) * _dot(op_ref[...], wp_ref[...])
              + jax.nn.sigmoid(_dot_nt(x, g2_ref[...])) * _dot(om_ref[...], wm_ref[...]))
    o_ref[...] = merged.astype(o_ref.dtype)

    for b in range(rider_bb):
        p, l = probs[b]
        v2 = mv_ref[b].reshape(mt * nh, hd).astype(BF16)
        mo_ref[b] = (_dot(p, v2) / l).astype(mo_ref.dtype)


def _merge(x, o_fox, o_pool, o_mem, w_f, w_p, w_m, w_gates_t, *, fox_transposed, rider=None,
           tm=512, tn=512):
    m, d = x.shape
    kb = o_pool.shape[1]
    n = w_f.shape[1]
    tm = min(tm, m)
    nb = n // tn
    steps = (m // tm) * nb
    act = pl.BlockSpec((tm, kb), lambda i, j: (i, 0))
    act_fox = pl.BlockSpec((kb, tm), lambda i, j: (0, i)) if fox_transposed else act
    wgt = pl.BlockSpec((kb, tn), lambda i, j: (0, j))
    gate = lambda g: pl.BlockSpec((tn, d), lambda i, j: (g * nb + j, 0))
    in_specs = [pl.BlockSpec((tm, d), lambda i, j: (i, 0)), act_fox, act, act, wgt, wgt, wgt,
                gate(0), gate(1), gate(2)]
    out_specs = [pl.BlockSpec((tm, tn), lambda i, j: (i, j))]
    out_shape = [jax.ShapeDtypeStruct((m, n), BF16)]
    args = [x, o_fox, o_pool, o_mem, w_f, w_p, w_m, w_gates_t, w_gates_t, w_gates_t]
    bb = 0
    if rider is not None:
        qm, ck, cv = rider
        nbatch, rows, hd = qm.shape
        _, mt, nh, _ = ck.shape
        assert nh & (nh - 1) == 0 and nbatch % steps == 0
        bb = nbatch // steps
        per_step = lambda i, j: (i * nb + j, 0, 0)
        in_specs += [pl.BlockSpec((bb, rows, hd), per_step),
                     pl.BlockSpec((bb, mt, nh, hd), lambda i, j: (i * nb + j, 0, 0, 0)),
                     pl.BlockSpec((bb, mt, nh, hd), lambda i, j: (i * nb + j, 0, 0, 0))]
        out_specs.append(pl.BlockSpec((bb, rows, hd), per_step))
        out_shape.append(jax.ShapeDtypeStruct((nbatch, rows, hd), BF16))
        args += [qm, ck, cv]
    return pl.pallas_call(
        functools.partial(_merge_body, fox_transposed=fox_transposed, rider_bb=bb), grid=(m // tm, nb),
        in_specs=in_specs, out_specs=out_specs, out_shape=out_shape,
        compiler_params=_cparams(("arbitrary", "arbitrary")),
        name="branch_merge",
    )(*args)


def _ep_residual_ln(acc, x, g, b, *, alpha):
    return (_layer_norm(alpha * x + acc, g, b),)


def _ffn_body(h_ref, wu_ref, wd_ref, g_ref, b_ref, o_ref, hb_ref, *, alpha):
    k = pl.program_id(1)

    @pl.when(k == 0)
    def _():
        hb_ref[...] = h_ref[...].astype(BF16)
        o_ref[...] = jnp.zeros_like(o_ref)

    a = jnp.square(jnp.maximum(_dot(hb_ref[...], wu_ref[...]), 0.0))
    o_ref[...] += _dot(a.astype(BF16), wd_ref[...])

    @pl.when(k == pl.num_programs(1) - 1)
    def _():
        o_ref[...] = _layer_norm(alpha * h_ref[...] + o_ref[...], g_ref[...], b_ref[...])


def _ffn(h, w_up, w_down, g, b, *, alpha, tm=1024, tf=512):
    m, d = h.shape
    dff = w_up.shape[1]
    tm = min(tm, m)
    return pl.pallas_call(
        functools.partial(_ffn_body, alpha=alpha),
        grid=(m // tm, dff // tf),
        in_specs=[pl.BlockSpec((tm, d), lambda i, k: (i, 0)),
                  pl.BlockSpec((d, tf), lambda i, k: (0, k)),
                  pl.BlockSpec((tf, d), lambda i, k: (k, 0)),
                  pl.BlockSpec((1, d), lambda i, k: (0, 0)),
                  pl.BlockSpec((1, d), lambda i, k: (0, 0))],
        out_specs=pl.BlockSpec((tm, d), lambda i, k: (i, 0)),
        out_shape=jax.ShapeDtypeStruct((m, d), F32),
        scratch_shapes=[pltpu.VMEM((tm, d), BF16)],
        compiler_params=_cparams(("parallel", "arbitrary")),
        name="ffn_ln",
    )(h, w_up, w_down, g, b)


def _mix(x, x_bf, o_fox, o_pool, o_mem, w, *, alpha, fox_transposed, rider=None):
    d = x.shape[1]
    merged, *rode = _merge(x_bf, o_fox, o_pool, o_mem, w["br_fox"], w["br_pool"], w["br_mem"],
                           w["gates_t"], fox_transposed=fox_transposed, rider=rider)
    row = lambda i, j: (0, 0)
    h, = _mm(merged, w["out"], [(F32, False)], functools.partial(_ep_residual_ln, alpha=alpha),
             extras=[(x, (min(512, x.shape[0]), d), lambda i, j: (i, 0)),
                     (w["ln1_g"], (1, d), row), (w["ln1_b"], (1, d), row)],
             tm=512, tn=d, name="out_proj_ln")
    return (h, *rode)


def kernel(x_prompt, x_sample, mem_prompt, cache_fox_k, cache_fox_v, cache_fox_logf, state_pool,
           cache_mem_k, cache_mem_v, page_table, w_in, b_fgate, w_pool, pool_scale, w_br_fox,
           w_br_pool, w_br_mem, w_mem_kv, w_out, ln1_g, ln1_b, w_up, w_down, ln2_g, ln2_b):
    batch, seq, d_model = x_prompt.shape
    dec_batch, dec_seq, _ = x_sample.shape
    depth = w_in.shape[0]
    n_phys, page_size = cache_fox_k.shape[1], cache_fox_k.shape[2]
    n_pages = page_table.shape[1]
    past = n_pages * page_size
    fox_w = FOX_HEADS * FOX_HD
    pool_w = len(POOL_WINDOWS) * POOL_GW
    mem_w = MEM_HEADS * MEM_HD
    mem_tokens = mem_prompt.shape[1]
    assert batch == 1 and dec_seq == FOX_HEADS == 8
    assert fox_w == pool_w == mem_w
    alpha = (2 * depth) ** 0.25
    nat, tr = False, True

    hp = x_prompt.reshape(seq, d_model)
    hs = x_sample.reshape(dec_batch * dec_seq, d_model)
    seg = POOL_HALO + dec_seq
    pos_p = jnp.arange(seq, dtype=F32)[:, None]
    pos_s = jnp.tile(jnp.concatenate([jnp.full((POOL_HALO,), 1e9, F32),
                                      past + jnp.arange(dec_seq, dtype=F32)]), dec_batch)[:, None]

    outs = {n: [] for n in ("pk", "pv", "pf", "pp", "pmk", "pmv", "sk", "sv", "sf", "sp")}
    for l in range(depth):
        w_t = jnp.swapaxes(w_in[l], 0, 1)
        r_q, r_k, r_v, r_f = 0, fox_w, 2 * fox_w, 3 * fox_w
        r_u = r_f + FOX_HEADS
        r_qm = r_u + pool_w
        r_g = r_qm + mem_w
        b_f = jnp.pad(b_fgate[l], (0, 128 - FOX_HEADS))[None, :]
        w = dict(gates_t=w_t[r_g:].astype(BF16),
                 br_fox=w_br_fox[l].astype(BF16), br_pool=w_br_pool[l].astype(BF16),
                 br_mem=w_br_mem[l].astype(BF16), out=w_out[l].astype(BF16),
                 up=w_up[l].astype(BF16), down=w_down[l].astype(BF16),
                 ln1_g=ln1_g[l][None, :], ln1_b=ln1_b[l][None, :],
                 ln2_g=ln2_g[l][None, :], ln2_b=ln2_b[l][None, :])
        wp_bf = w_pool[l].astype(BF16)
        psc = pool_scale[l][None, :]
        logf_of = functools.partial(_mm, w=w_t, outs=[(F32, nat)], epilogue=_ep_log_sigmoid,
                                    extras=[(b_f, (1, 128), lambda i, j: (0, 0))], tn=128, row0=r_f,
                                    ncols=128, name="proj_logf")

        xs_bf, q_s = _mm(hs, w_t, [(BF16, nat)], functools.partial(_ep_scaled, scale=FOX_HD ** -0.5 * LOG2E),
                         row0=r_q, ncols=fox_w, tm=512, emit_x=True, name="proj_q")
        k_s, = _mm(xs_bf, w_t, [(F32, nat)], _ep_identity, row0=r_k, ncols=fox_w, name="proj_k")
        v_s, = _mm(xs_bf, w_t, [(F32, nat)], _ep_identity, row0=r_v, ncols=fox_w, name="proj_v")
        u_s, = _mm(xs_bf, w_t, [(F32, nat)], _ep_identity, row0=r_u, ncols=pool_w, name="proj_u")
        qm_s, = _mm(xs_bf, w_t, [(BF16, nat)], _ep_identity, row0=r_qm, ncols=mem_w, name="proj_qm")
        logf_s = logf_of(xs_bf)[0][:, :FOX_HEADS]

        x_bf, q2T = _mm(hp, w_t, [(BF16, tr)], functools.partial(_ep_scaled_t, scale=FOX_HD ** -0.5 * LOG2E),
                        row0=r_q, ncols=fox_w, tm=512, emit_x=True, name="proj_q")
        k, k_bf = _mm(x_bf, w_t, [(F32, nat), (BF16, nat)], _ep_dup, row0=r_k, ncols=fox_w, name="proj_k")
        v, vT = _mm(x_bf, w_t, [(F32, nat), (BF16, tr)], _ep_with_t, row0=r_v, ncols=fox_w, name="proj_v")
        u, = _mm(x_bf, w_t, [(F32, nat)], _ep_identity, row0=r_u, ncols=pool_w, name="proj_u")
        qm, = _mm(x_bf, w_t, [(BF16, nat)], _ep_identity, row0=r_qm, ncols=mem_w, name="proj_qm")
        logf = logf_of(x_bf)[0][:, :FOX_HEADS]
        c2_rows = _cumsum_lanes(logf.T, scale=LOG2E)
        o_foxT = _fox_prompt(q2T, k_bf, vT, c2_rows)
        o_pool = _pool(u, pos_p, wp_bf, psc, tm=1024, zero_first=True)
        mkv, mkv_bf = _mm(mem_prompt.reshape(mem_tokens, d_model), w_mem_kv[l], [(F32, nat), (BF16, nat)],
                          _ep_dup, name="mem_kv")
        o_mem = _mem_prompt(qm, mkv_bf[:, :mem_w], mkv_bf[:, mem_w:])
        outs["pk"].append(k.reshape(batch, seq, FOX_HEADS, FOX_HD))
        outs["pv"].append(v.reshape(batch, seq, FOX_HEADS, FOX_HD))
        outs["pf"].append(logf.reshape(batch, seq, FOX_HEADS))
        outs["pp"].append(u[-POOL_STATE:].reshape(batch, POOL_STATE, pool_w))
        outs["pmk"].append(mkv[:, :mem_w].reshape(batch, mem_tokens, MEM_HEADS, MEM_HD))
        outs["pmv"].append(mkv[:, mem_w:].reshape(batch, mem_tokens, MEM_HEADS, MEM_HD))
        h_p, o_mem_s = _mix(hp, x_bf, o_foxT, o_pool, o_mem, w, alpha=alpha, fox_transposed=True,
                            rider=(qm_s.reshape(dec_batch, dec_seq * MEM_HEADS, MEM_HD), cache_mem_k[l],
                                   cache_mem_v[l]))
        rows = dec_seq * FOX_HEADS
        hp, o_fox_s = _ffn_fox(
            h_p, w["up"], w["down"], w["ln2_g"], w["ln2_b"],
            q_s.reshape(dec_batch, rows, FOX_HD), k_s.reshape(dec_batch, rows, FOX_HD),
            v_s.reshape(dec_batch, rows, FOX_HD), logf_s.reshape(dec_batch, 1, rows),
            cache_fox_k[l].reshape(n_phys, page_size * FOX_HEADS, FOX_HD),
            cache_fox_v[l].reshape(n_phys, page_size * FOX_HEADS, FOX_HD),
            _page_logf_sums(cache_fox_logf[l]), page_table, alpha=alpha)

        u3 = u_s.reshape(dec_batch, dec_seq, pool_w)
        state = state_pool[l].astype(F32)
        u_ext = jnp.concatenate([jnp.zeros((dec_batch, POOL_HALO - POOL_STATE, pool_w), F32), state, u3],
                                axis=1).reshape(dec_batch * seg, pool_w)
        o_pool_s = _pool(u_ext, pos_s, wp_bf, psc, tm=32 * seg, zero_first=False)
        o_pool_s = o_pool_s.reshape(dec_batch, seg, pool_w)[:, POOL_HALO:].reshape(dec_batch * dec_seq, pool_w)
        outs["sk"].append(k_s.reshape(dec_batch, dec_seq, FOX_HEADS, FOX_HD))
        outs["sv"].append(v_s.reshape(dec_batch, dec_seq, FOX_HEADS, FOX_HD))
        outs["sf"].append(logf_s.reshape(dec_batch, dec_seq, FOX_HEADS))
        outs["sp"].append(jnp.concatenate([state, u3], axis=1)[:, -POOL_STATE:])
        h_s, = _mix(hs, xs_bf, o_fox_s.reshape(dec_batch * dec_seq, fox_w), o_pool_s,
                    o_mem_s.reshape(dec_batch * dec_seq, mem_w), w, alpha=alpha, fox_transposed=False)
        hs = _ffn(h_s, w["up"], w["down"], w["ln2_g"], w["ln2_b"], alpha=alpha)

    st = lambda n: jnp.stack(outs[n])
    return (hp.reshape(batch, seq, d_model), hs.reshape(dec_batch, dec_seq, d_model),
            st("pk"), st("pv"), st("pf"), st("pp"), st("pmk"), st("pmv"),
            st("sk"), st("sv"), st("sf"), st("sp"))
```

```python
import functools

import numpy as np

import jax
import jax.numpy as jnp
from jax import lax
from jax.experimental import pallas as pl
from jax.experimental.pallas import tpu as pltpu

F32 = jnp.float32
BF16 = jnp.bfloat16

FOX_HEADS = 8
FOX_HD = 128
POOL_WINDOWS = (2, 4, 8, 16)
POOL_GW = 256
POOL_STATE = max(POOL_WINDOWS) - 1
POOL_HALO = 16
MEM_HEADS = 4
MEM_HD = 256
LN_EPS = 1e-5
MASK_VALUE = -1e30
LOG2E = 1.4426950408889634
FOX_EXT = 128
FOX_ONES = 16

V7X_VMEM_LIMIT_BYTES = 56 * 1024 * 1024


def _cparams(semantics):
    return pltpu.CompilerParams(dimension_semantics=semantics,
                                vmem_limit_bytes=V7X_VMEM_LIMIT_BYTES)


def _dot(a, b):
    return jnp.dot(a, b, preferred_element_type=F32)


def _dot_nt(a, b):
    return lax.dot_general(a, b, (((1,), (1,)), ((), ())), preferred_element_type=F32)


def _dot_tn(a, b):
    return lax.dot_general(a, b, (((0,), (0,)), ((), ())), preferred_element_type=F32)


def _split3(x):
    hi = x.astype(BF16)
    r1 = x - hi.astype(F32)
    mid = r1.astype(BF16)
    lo = (r1 - mid.astype(F32)).astype(BF16)
    return hi, mid, lo


def _split3_dot(x, t):
    hi, mid, lo = _split3(x)
    return _dot(hi, t) + _dot(mid, t) + _dot(lo, t)


def _layer_norm(z, g, b):
    mu = jnp.mean(z, axis=-1, keepdims=True)
    zc = z - mu
    var = jnp.mean(zc * zc, axis=-1, keepdims=True)
    return zc * lax.rsqrt(var + LN_EPS) * g + b


def _mm_body(*refs, epilogue, n_in, emit_x, w_rows):
    x_ref, w_ref = refs[0], refs[1]
    extras = refs[2:n_in]
    outs = list(refs[n_in:])
    x = x_ref[...].astype(BF16)
    if emit_x:
        outs.pop(0)[...] = x
    acc = (_dot_nt if w_rows else _dot)(x, w_ref[...].astype(BF16))
    res = epilogue(acc, *(e[...] for e in extras))
    for o, r in zip(outs, res):
        o[...] = r.astype(o.dtype)


def _mm(x, w, outs, epilogue, *, extras=(), tm=1024, tn=1024, row0=None, ncols=None, emit_x=False, name):
    m, k = x.shape
    n = w.shape[1] if ncols is None else ncols
    tm, tn = min(tm, m), min(tn, n)
    assert m % tm == 0 and n % tn == 0 and (n == tn or not emit_x)
    if row0 is None:
        w_spec = pl.BlockSpec((k, tn), lambda i, j: (0, j))
    else:
        assert row0 % 8 == 0 and tn % 8 == 0
        w_spec = pl.BlockSpec((pl.Element(tn), pl.Element(k)),
                              lambda i, j: (pl.multiple_of(row0 + j * tn, 8), 0))
    in_specs = [pl.BlockSpec((tm, k), lambda i, j: (i, 0)), w_spec]
    in_specs += [pl.BlockSpec(bs, im) for _, bs, im in extras]
    out_specs = [pl.BlockSpec((tn, tm), lambda i, j: (j, i)) if t else
                 pl.BlockSpec((tm, tn), lambda i, j: (i, j)) for _, t in outs]
    out_shape = [jax.ShapeDtypeStruct((n, m) if t else (m, n), dt) for dt, t in outs]
    if emit_x:
        out_specs.insert(0, pl.BlockSpec((tm, k), lambda i, j: (i, 0)))
        out_shape.insert(0, jax.ShapeDtypeStruct((m, k), BF16))
    return pl.pallas_call(
        functools.partial(_mm_body, epilogue=epilogue, n_in=2 + len(extras), emit_x=emit_x,
                          w_rows=row0 is not None),
        grid=(m // tm, n // tn),
        in_specs=in_specs, out_specs=out_specs, out_shape=out_shape,
        compiler_params=_cparams(("parallel", "arbitrary")),
        name=name,
    )(x, w, *(a for a, _, _ in extras))


def _ep_identity(acc):
    return (acc,)


def _ep_dup(acc):
    return (acc, acc)


def _ep_scaled(acc, *, scale):
    return (acc * scale,)


def _ep_scaled_t(acc, *, scale):
    return ((acc * scale).T,)


def _ep_with_t(acc):
    return (acc, acc.T)


def _ep_log_sigmoid(acc, bias):
    z = acc + bias
    return (jnp.minimum(z, 0.0) - jnp.log1p(jnp.exp(-jnp.abs(z))),)


def _cumsum_body(x_ref, tri_ref, o_ref, *, chunk, scale):
    n = x_ref.shape[1]
    carry = jnp.zeros((x_ref.shape[0], 1), F32)
    for c in range(n // chunk):
        cs = _split3_dot(x_ref[:, c * chunk:(c + 1) * chunk], tri_ref[...]) + carry
        o_ref[:, c * chunk:(c + 1) * chunk] = cs * scale
        carry = cs[:, chunk - 1:chunk]


def _cumsum_lanes(x, *, scale, chunk=512):
    rows, n = x.shape
    tri = jnp.asarray(np.triu(np.ones((chunk, chunk), np.float32)), BF16)
    return pl.pallas_call(
        functools.partial(_cumsum_body, chunk=chunk, scale=scale),
        out_shape=jax.ShapeDtypeStruct((rows, n), F32),
        name="logf_cumsum",
    )(x, tri)


def _bias_rows(c, ones_first, n):
    sub = lax.broadcasted_iota(jnp.int32, (n, c.shape[1]), 0)
    c0, o0 = (3, 0) if ones_first else (0, 3)
    rows = jnp.where((sub >= o0) & (sub < o0 + 3), 1.0, 0.0).astype(BF16)
    for t, piece in enumerate(_split3(c)):
        rows = jnp.where(sub == c0 + t, piece, rows)
    return rows


def _fox_prompt_body(qT_ref, k_ref, vT_ref, c_ref, oT_ref, kext, s0, s1, m_scr, acc_scr, *, tq):
    i = pl.program_id(1)
    tk = tq

    @pl.when(i == 0)
    def _():
        eye = (lax.broadcasted_iota(jnp.int32, (FOX_ONES, FOX_EXT), 0)
               == lax.broadcasted_iota(jnp.int32, (FOX_ONES, FOX_EXT), 1)).astype(BF16)
        for c in range(k_ref.shape[0] // tk):
            rows = _bias_rows(-c_ref[0, :, c * tk:(c + 1) * tk], True, FOX_ONES)
            kext[c * tk:(c + 1) * tk, :] = _dot_tn(rows, eye).astype(BF16)

    last = pl.multiple_of((i + 1) * tq - 128, 128)
    cq = c_ref[0, :, pl.ds(last, 128)][:, 127:128]
    qa = jnp.concatenate([qT_ref[...], jnp.broadcast_to(_bias_rows(cq, False, FOX_EXT), (FOX_EXT, tq))],
                         axis=0)
    ones = jnp.ones((FOX_ONES, tk), BF16)

    def scores(j, dst):
        ks = pl.multiple_of(j * tk, tk)
        ka = jnp.concatenate([k_ref[pl.ds(ks, tk), :].astype(BF16), kext[pl.ds(ks, tk), :]], axis=1)
        dst[...] = _dot(ka, qa)

    def absorb(j, src, diag):
        ks = pl.multiple_of(j * tk, tk)
        sT = src[...]
        if diag:
            key = lax.broadcasted_iota(jnp.int32, (tk, tq), 0)
            qry = lax.broadcasted_iota(jnp.int32, (tk, tq), 1)
            sT = jnp.where(key <= qry, sT, MASK_VALUE)
        m = m_scr[...]
        m_new = jnp.maximum(m, jnp.max(sT, axis=0, keepdims=True))
        p = jnp.exp2(sT - m_new).astype(BF16)
        m_scr[...] = m_new
        va = jnp.concatenate([vT_ref[:, pl.ds(ks, tk)], ones], axis=0)
        acc_scr[...] = jnp.exp2(m - m_new) * acc_scr[...] + _dot(va, p)

    def finish(src):
        absorb(i, src, True)
        oT_ref[...] = (acc_scr[:FOX_HD, :] / acc_scr[FOX_HD:FOX_HD + 1, :]).astype(oT_ref.dtype)

    m_scr[...] = jnp.full(m_scr.shape, MASK_VALUE, F32)
    acc_scr[...] = jnp.zeros(acc_scr.shape, F32)
    scores(0, s0)

    def pair(t, carry):
        j = 2 * t
        scores(j + 1, s1)
        absorb(j, s0, False)
        scores(j + 2, s0)
        absorb(j + 1, s1, False)
        return carry

    lax.fori_loop(0, i // 2, pair, 0)

    @pl.when(i % 2 == 0)
    def _():
        finish(s0)

    @pl.when(i % 2 == 1)
    def _():
        scores(i, s1)
        absorb(i - 1, s0, False)
        finish(s1)


def _fox_prompt(q2T, k, vT, c2_rows, *, tq=1024):
    h, s = c2_rows.shape
    return pl.pallas_call(
        functools.partial(_fox_prompt_body, tq=tq),
        grid=(h, s // tq),
        in_specs=[pl.BlockSpec((FOX_HD, tq), lambda g, i: (g, i)),
                  pl.BlockSpec((s, FOX_HD), lambda g, i: (0, g)),
                  pl.BlockSpec((FOX_HD, s), lambda g, i: (g, 0)),
                  pl.BlockSpec((1, 1, s), lambda g, i: (g, 0, 0))],
        out_specs=pl.BlockSpec((FOX_HD, tq), lambda g, i: (g, i)),
        out_shape=jax.ShapeDtypeStruct((h * FOX_HD, s), BF16),
        scratch_shapes=[pltpu.VMEM((s, FOX_EXT), BF16), pltpu.VMEM((tq, tq), F32), pltpu.VMEM((tq, tq), F32),
                        pltpu.VMEM((1, tq), F32), pltpu.VMEM((FOX_HD + FOX_ONES, tq), F32)],
        compiler_params=_cparams(("parallel", "arbitrary")),
        name="fox_prompt_attention",
    )(q2T, k, vT, c2_rows[:, None, :])


def _ffn_fox_body(pt_ref, h_ref, wu_ref, wd_ref, g_ref, b_ref,
                  q_ref, kn_ref, vn_ref, lfn_ref, msk_ref, t4_ref, t5_ref, *rest, n_half, alpha):
    k_pages = rest[:n_half]
    v_pages = rest[n_half:2 * n_half]
    lf_pages = rest[2 * n_half:3 * n_half]
    y_ref, o_ref, hb_ref, s_scr, mbq_scr, m_scr, l_scr, acc_scr, later_scr = rest[3 * n_half:]
    del pt_ref
    kc = pl.program_id(1)
    rows = q_ref.shape[1]
    pk = k_pages[0].shape[1]
    nh = FOX_HEADS
    q = q_ref[0]
    first = kc % 2 == 0

    @pl.when(kc == 0)
    def _():
        hb_ref[...] = h_ref[...].astype(BF16)
        y_ref[...] = jnp.zeros_like(y_ref)

    @pl.when(first)
    def _():
        r = lfn_ref[0] * LOG2E
        bq = _split3_dot(jnp.broadcast_to(r, (rows, rows)) * msk_ref[...], t4_ref[...])
        cn = _split3_dot(jnp.broadcast_to(r, (8, rows)), t5_ref[...])[0:1]
        row_h = lax.broadcasted_iota(jnp.int32, (rows, 128), 0) & (nh - 1)
        lane_h = lax.broadcasted_iota(jnp.int32, (rows, 128), 1) & (nh - 1)
        same_head = row_h == lane_h
        mbq = jnp.where(same_head, bq, MASK_VALUE)
        mbq_scr[...] = jnp.concatenate([mbq] * (pk // 128), axis=1)
        row_t = lax.broadcasted_iota(jnp.int32, (rows, rows), 0) >> 3
        lane_t = lax.broadcasted_iota(jnp.int32, (rows, rows), 1) >> 3
        ok_new = same_head[:, :rows] & (lane_t <= row_t)
        s_new = jnp.where(ok_new, _dot_nt(q, kn_ref[0].astype(BF16)) + (bq[:, :rows] - cn), MASK_VALUE)
        m0 = jnp.max(s_new, axis=-1, keepdims=True)
        p_new = jnp.exp2(s_new - m0)
        m_scr[...] = m0
        l_scr[...] = jnp.sum(p_new, axis=-1, keepdims=True)
        acc_scr[...] = _dot(p_new.astype(BF16), vn_ref[0].astype(BF16))
        later_scr[...] = jnp.zeros_like(later_scr)

    later = later_scr[...]
    m_old = m_scr[...]
    m_new = m_old
    for p in reversed(range(n_half)):
        lf = lf_pages[p][0]
        tot = lf[:, pk:]
        d = jnp.concatenate([later + tot] * (pk // 128), axis=1) - lf[:, :pk]
        later = later + tot
        s = _dot_nt(q, k_pages[p][0].astype(BF16)) + (mbq_scr[...] + d)
        s_scr[:, p * pk:(p + 1) * pk] = s
        m_new = jnp.maximum(m_new, jnp.max(s, axis=-1, keepdims=True))
    later_scr[...] = later

    a = jnp.square(jnp.maximum(_dot(hb_ref[...], wu_ref[...]), 0.0))
    y_ref[...] += _dot(a.astype(BF16), wd_ref[...])

    scale_old = jnp.exp2(m_old - m_new)
    l = scale_old * l_scr[...]
    acc = scale_old * acc_scr[...]
    for p in range(n_half):
        pr = jnp.exp2(s_scr[:, p * pk:(p + 1) * pk] - m_new)
        l = l + jnp.sum(pr, axis=-1, keepdims=True)
        acc = acc + _dot(pr.astype(BF16), v_pages[p][0].astype(BF16))
    m_scr[...] = m_new
    l_scr[...] = l
    acc_scr[...] = acc

    @pl.when(jnp.logical_not(first))
    def _():
        o_ref[0] = (acc / l).astype(o_ref.dtype)

    @pl.when(kc == pl.num_programs(1) - 1)
    def _():
        y_ref[...] = _layer_norm(alpha * h_ref[...] + y_ref[...], g_ref[...], b_ref[...])


def _ffn_fox(h, w_up, w_down, g, b, q, k_new, v_new, lf_new, cache_k, cache_v, lf_cache, page_table,
             *, alpha, tm=512, tf=512):
    m, d = h.shape
    dff = w_up.shape[1]
    nb, rows, hd = q.shape
    n_pages = page_table.shape[1]
    n_half = n_pages // 2
    pk = cache_k.shape[1]
    ni, nk = m // tm, dff // tf
    per_i = nk // 2
    assert rows == 64 and FOX_HEADS == 8 and pk % 128 == 0 and n_pages % 2 == 0
    assert m % tm == 0 and dff % tf == 0 and nk % 2 == 0 and ni * per_i == nb

    rr = np.arange(rows)
    tt, hh = rr // FOX_HEADS, rr % FOX_HEADS
    lane = np.arange(128)
    msk = jnp.asarray((tt[None, :] <= tt[:, None]).astype(np.float32))
    t4 = jnp.asarray((hh[:, None] == (lane % FOX_HEADS)[None, :]).astype(np.float32), BF16)
    t5 = jnp.asarray(((hh[:, None] == hh[None, :]) & (tt[:, None] <= tt[None, :])).astype(np.float32), BF16)

    batch = lambda i, k: i * per_i + k // 2

    def page_map(p):
        return lambda i, k, pt: (pt[batch(i, k) * n_pages + (1 - k % 2) * n_half + p], 0, 0)

    per_b = lambda i, k, pt: (batch(i, k), 0, 0)
    const = lambda i, k, pt: (0, 0)
    in_specs = [pl.BlockSpec((tm, d), lambda i, k, pt: (i, 0)),
                pl.BlockSpec((d, tf), lambda i, k, pt: (0, k)),
                pl.BlockSpec((tf, d), lambda i, k, pt: (k, 0)),
                pl.BlockSpec((1, d), const),
                pl.BlockSpec((1, d), const),
                pl.BlockSpec((1, rows, hd), per_b),
                pl.BlockSpec((1, rows, hd), per_b),
                pl.BlockSpec((1, rows, hd), per_b),
                pl.BlockSpec((1, 1, rows), per_b),
                pl.BlockSpec((rows, rows), const),
                pl.BlockSpec((rows, 128), const),
                pl.BlockSpec((rows, rows), const)]
    in_specs += [pl.BlockSpec((1, pk, hd), page_map(p)) for p in range(n_half)]
    in_specs += [pl.BlockSpec((1, pk, hd), page_map(p)) for p in range(n_half)]
    in_specs += [pl.BlockSpec((1, 1, pk + 128), page_map(p)) for p in range(n_half)]
    grid_spec = pltpu.PrefetchScalarGridSpec(
        num_scalar_prefetch=1, grid=(ni, nk), in_specs=in_specs,
        out_specs=[pl.BlockSpec((tm, d), lambda i, k, pt: (i, 0)),
                   pl.BlockSpec((1, rows, hd), per_b)],
        scratch_shapes=[pltpu.VMEM((tm, d), BF16), pltpu.VMEM((rows, n_half * pk), F32),
                        pltpu.VMEM((rows, pk), F32), pltpu.VMEM((rows, 1), F32), pltpu.VMEM((rows, 1), F32),
                        pltpu.VMEM((rows, hd), F32), pltpu.VMEM((1, 128), F32)])
    return pl.pallas_call(
        functools.partial(_ffn_fox_body, n_half=n_half, alpha=alpha),
        grid_spec=grid_spec,
        out_shape=[jax.ShapeDtypeStruct((m, d), F32), jax.ShapeDtypeStruct((nb, rows, hd), BF16)],
        compiler_params=_cparams(("arbitrary", "arbitrary")),
        name="ffn_ln_fox_sample",
    )(page_table.reshape(-1), h, w_up, w_down, g, b, q, k_new, v_new, lf_new, msk, t4, t5,
      *([cache_k] * n_half), *([cache_v] * n_half), *([lf_cache] * n_half))


def _page_logf_sums(cache_logf):
    p, page, h = cache_logf.shape
    n = page * h
    pos, head = np.arange(n) // h, np.arange(n) % h
    prefix = (head[:, None] == head[None, :]) & (pos[:, None] <= pos[None, :])
    total = head[:, None] == (np.arange(128) % h)[None, :]
    t = jnp.asarray(np.concatenate([prefix, total], axis=1).astype(np.float32), BF16)
    tm = next(c for c in (512, 256, 128, 64, 32, 16, 8) if p % c == 0)

    def body(x_ref, t_ref, o_ref):
        o_ref[...] = _split3_dot(x_ref[...], t_ref[...]) * LOG2E

    out = pl.pallas_call(
        body, grid=(p // tm,),
        in_specs=[pl.BlockSpec((tm, n), lambda i: (i, 0)),
                  pl.BlockSpec((n, n + 128), lambda i: (0, 0))],
        out_specs=pl.BlockSpec((tm, n + 128), lambda i: (i, 0)),
        out_shape=jax.ShapeDtypeStruct((p, n + 128), F32),
        compiler_params=_cparams(("parallel",)),
        name="page_logf_sums",
    )(cache_logf.reshape(p, n), t)
    return out.reshape(p, 1, n + 128)


def _pool_body(u_ref, halo_ref, pos_ref, wp_ref, sc_ref, o_ref, *, zero_first):
    u = u_ref[...]
    halo = halo_ref[...]
    if zero_first:
        halo = jnp.where(pl.program_id(0) == 0, 0.0, halo)
    pos1 = pos_ref[...] + 1.0
    outs = []
    for g, w in enumerate(POOL_WINDOWS):
        sl = slice(g * POOL_GW, (g + 1) * POOL_GW)
        s = jnp.concatenate([halo[:, sl], u[:, sl]], axis=0)
        sh = 1
        while sh < w:
            s = s + pltpu.roll(s, sh, 0)
            sh *= 2
        d = s[POOL_HALO:] / jnp.minimum(float(w), pos1) - u[:, sl]
        outs.append(_dot(d.astype(BF16), wp_ref[g]))
    o_ref[...] = (jnp.concatenate(outs, axis=1) * sc_ref[...]).astype(o_ref.dtype)


def _pool(u, pos, w_pool, pool_scale, *, tm, zero_first):
    assert all(w & (w - 1) == 0 for w in POOL_WINDOWS) and tm % POOL_HALO == 0
    r, width = u.shape
    hb = tm // POOL_HALO
    return pl.pallas_call(
        functools.partial(_pool_body, zero_first=zero_first),
        grid=(r // tm,),
        in_specs=[pl.BlockSpec((tm, width), lambda i: (i, 0)),
                  pl.BlockSpec((POOL_HALO, width), lambda i: (jnp.maximum(i * hb - 1, 0), 0)),
                  pl.BlockSpec((tm, 1), lambda i: (i, 0)),
                  pl.BlockSpec(w_pool.shape, lambda i: (0, 0, 0)),
                  pl.BlockSpec((1, width), lambda i: (0, 0))],
        out_specs=pl.BlockSpec((tm, width), lambda i: (i, 0)),
        out_shape=jax.ShapeDtypeStruct((r, width), BF16),
        compiler_params=_cparams(("arbitrary",)),
        name="pool_mixer",
    )(u, u, pos, w_pool, pool_scale)


def _mem_prompt_body(q_ref, k_ref, v_ref, o_ref, *, scale):
    for h in range(MEM_HEADS):
        sl = slice(h * MEM_HD, (h + 1) * MEM_HD)
        s = _dot_nt(q_ref[:, sl], k_ref[:, sl]) * scale
        p = jnp.exp(s - jnp.max(s, axis=-1, keepdims=True))
        l = jnp.sum(p, axis=-1, keepdims=True)
        o_ref[:, sl] = (_dot(p.astype(BF16), v_ref[:, sl]) / l).astype(o_ref.dtype)


def _mem_prompt(qm, mk, mv, *, tq=1024):
    s, width = qm.shape
    m = mk.shape[0]
    return pl.pallas_call(
        functools.partial(_mem_prompt_body, scale=MEM_HD ** -0.5),
        grid=(s // tq,),
        in_specs=[pl.BlockSpec((tq, width), lambda i: (i, 0)),
                  pl.BlockSpec((m, width), lambda i: (0, 0)),
                  pl.BlockSpec((m, width), lambda i: (0, 0))],
        out_specs=pl.BlockSpec((tq, width), lambda i: (i, 0)),
        out_shape=jax.ShapeDtypeStruct((s, width), BF16),
        compiler_params=_cparams(("parallel",)),
        name="mem_prompt_attention",
    )(qm, mk, mv)


def _merge_body(x_ref, of_ref, op_ref, om_ref, wf_ref, wp_ref, wm_ref, g0_ref, g1_ref, g2_ref, *rest,
                fox_transposed, rider_bb):
    if rider_bb:
        mq_ref, mk_ref, mv_ref, o_ref, mo_ref = rest
        rows = mq_ref.shape[1]
        mt, nh, hd = mk_ref.shape[1:]
        row_h = lax.broadcasted_iota(jnp.int32, (rows, mt * nh), 0) & (nh - 1)
        lane_h = lax.broadcasted_iota(jnp.int32, (rows, mt * nh), 1) & (nh - 1)
        own = row_h == lane_h
        probs = []
        for b in range(rider_bb):
            k2 = mk_ref[b].reshape(mt * nh, hd).astype(BF16)
            s = jnp.where(own, _dot_nt(mq_ref[b], k2) * hd ** -0.5, MASK_VALUE)
            p = jnp.exp(s - jnp.max(s, axis=-1, keepdims=True))
            probs.append((p.astype(BF16), jnp.sum(p, axis=-1, keepdims=True)))
    else:
        (o_ref,) = rest

    x = x_ref[...]
    fox = (_dot_tn if fox_transposed else _dot)(of_ref[...], wf_ref[...])
    merged = (jax.nn.sigmoid(_dot_nt(x, g0_ref[...])) * fox
              + jax.nn.sigmoid(_dot_nt(x, g1_ref[...])) * _dot(op_ref[...], wp_ref[...])
              + jax.nn.sigmoid(_dot_nt(x, g2_ref[...])) * _dot(om_ref[...], wm_ref[...]))
    o_ref[...] = merged.astype(o_ref.dtype)

    for b in range(rider_bb):
        p, l = probs[b]
        v2 = mv_ref[b].reshape(mt * nh, hd).astype(BF16)
        mo_ref[b] = (_dot(p, v2) / l).astype(mo_ref.dtype)


def _merge(x, o_fox, o_pool, o_mem, w_f, w_p, w_m, w_gates_t, *, fox_transposed, rider=None,
           tm=512, tn=512):
    m, d = x.shape
    kb = o_pool.shape[1]
    n = w_f.shape[1]
    tm = min(tm, m)
    nb = n // tn
    steps = (m // tm) * nb
    act = pl.BlockSpec((tm, kb), lambda i, j: (i, 0))
    act_fox = pl.BlockSpec((kb, tm), lambda i, j: (0, i)) if fox_transposed else act
    wgt = pl.BlockSpec((kb, tn), lambda i, j: (0, j))
    gate = lambda g: pl.BlockSpec((tn, d), lambda i, j: (g * nb + j, 0))
    in_specs = [pl.BlockSpec((tm, d), lambda i, j: (i, 0)), act_fox, act, act, wgt, wgt, wgt,
                gate(0), gate(1), gate(2)]
    out_specs = [pl.BlockSpec((tm, tn), lambda i, j: (i, j))]
    out_shape = [jax.ShapeDtypeStruct((m, n), BF16)]
    args = [x, o_fox, o_pool, o_mem, w_f, w_p, w_m, w_gates_t, w_gates_t, w_gates_t]
    bb = 0
    if rider is not None:
        qm, ck, cv = rider
        nbatch, rows, hd = qm.shape
        _, mt, nh, _ = ck.shape
        assert nh & (nh - 1) == 0 and nbatch % steps == 0
        bb = nbatch // steps
        per_step = lambda i, j: (i * nb + j, 0, 0)
        in_specs += [pl.BlockSpec((bb, rows, hd), per_step),
                     pl.BlockSpec((bb, mt, nh, hd), lambda i, j: (i * nb + j, 0, 0, 0)),
                     pl.BlockSpec((bb, mt, nh, hd), lambda i, j: (i * nb + j, 0, 0, 0))]
        out_specs.append(pl.BlockSpec((bb, rows, hd), per_step))
        out_shape.append(jax.ShapeDtypeStruct((nbatch, rows, hd), BF16))
        args += [qm, ck, cv]
    return pl.pallas_call(
        functools.partial(_merge_body, fox_transposed=fox_transposed, rider_bb=bb), grid=(m // tm, nb),
        in_specs=in_specs, out_specs=out_specs, out_shape=out_shape,
        compiler_params=_cparams(("arbitrary", "arbitrary")),
        name="branch_merge",
    )(*args)


def _ep_residual_ln(acc, x, g, b, *, alpha):
    return (_layer_norm(alpha * x + acc, g, b),)


def _ffn_body(h_ref, wu_ref, wd_ref, g_ref, b_ref, o_ref, hb_ref, *, alpha):
    k = pl.program_id(1)

    @pl.when(k == 0)
    def _():
        hb_ref[...] = h_ref[...].astype(BF16)
        o_ref[...] = jnp.zeros_like(o_ref)

    a = jnp.square(jnp.maximum(_dot(hb_ref[...], wu_ref[...]), 0.0))
    o_ref[...] += _dot(a.astype(BF16), wd_ref[...])

    @pl.when(k == pl.num_programs(1) - 1)
    def _():
        o_ref[...] = _layer_norm(alpha * h_ref[...] + o_ref[...], g_ref[...], b_ref[...])


def _ffn(h, w_up, w_down, g, b, *, alpha, tm=1024, tf=512):
    m, d = h.shape
    dff = w_up.shape[1]
    tm = min(tm, m)
    return pl.pallas_call(
        functools.partial(_ffn_body, alpha=alpha),
        grid=(m // tm, dff // tf),
        in_specs=[pl.BlockSpec((tm, d), lambda i, k: (i, 0)),
                  pl.BlockSpec((d, tf), lambda i, k: (0, k)),
                  pl.BlockSpec((tf, d), lambda i, k: (k, 0)),
                  pl.BlockSpec((1, d), lambda i, k: (0, 0)),
                  pl.BlockSpec((1, d), lambda i, k: (0, 0))],
        out_specs=pl.BlockSpec((tm, d), lambda i, k: (i, 0)),
        out_shape=jax.ShapeDtypeStruct((m, d), F32),
        scratch_shapes=[pltpu.VMEM((tm, d), BF16)],
        compiler_params=_cparams(("parallel", "arbitrary")),
        name="ffn_ln",
    )(h, w_up, w_down, g, b)


def _mix(x, x_bf, o_fox, o_pool, o_mem, w, *, alpha, fox_transposed, rider=None):
    d = x.shape[1]
    merged, *rode = _merge(x_bf, o_fox, o_pool, o_mem, w["br_fox"], w["br_pool"], w["br_mem"],
                           w["gates_t"], fox_transposed=fox_transposed, rider=rider)
    row = lambda i, j: (0, 0)
    h, = _mm(merged, w["out"], [(F32, False)], functools.partial(_ep_residual_ln, alpha=alpha),
             extras=[(x, (min(512, x.shape[0]), d), lambda i, j: (i, 0)),
                     (w["ln1_g"], (1, d), row), (w["ln1_b"], (1, d), row)],
             tm=512, tn=d, name="out_proj_ln")
    return (h, *rode)


def kernel(x_prompt, x_sample, mem_prompt, cache_fox_k, cache_fox_v, cache_fox_logf, state_pool,
           cache_mem_k, cache_mem_v, page_table, w_in, b_fgate, w_pool, pool_scale, w_br_fox,
           w_br_pool, w_br_mem, w_mem_kv, w_out, ln1_g, ln1_b, w_up, w_down, ln2_g, ln2_b):
    batch, seq, d_model = x_prompt.shape
    dec_batch, dec_seq, _ = x_sample.shape
    depth = w_in.shape[0]
    n_phys, page_size = cache_fox_k.shape[1], cache_fox_k.shape[2]
    n_pages = page_table.shape[1]
    past = n_pages * page_size
    fox_w = FOX_HEADS * FOX_HD
    pool_w = len(POOL_WINDOWS) * POOL_GW
    mem_w = MEM_HEADS * MEM_HD
    mem_tokens = mem_prompt.shape[1]
    assert batch == 1 and dec_seq == FOX_HEADS == 8
    assert fox_w == pool_w == mem_w
    alpha = (2 * depth) ** 0.25
    nat, tr = False, True

    hp = x_prompt.reshape(seq, d_model)
    hs = x_sample.reshape(dec_batch * dec_seq, d_model)
    seg = POOL_HALO + dec_seq
    pos_p = jnp.arange(seq, dtype=F32)[:, None]
    pos_s = jnp.tile(jnp.concatenate([jnp.full((POOL_HALO,), 1e9, F32),
                                      past + jnp.arange(dec_seq, dtype=F32)]), dec_batch)[:, None]

    outs = {n: [] for n in ("pk", "pv", "pf", "pp", "pmk", "pmv", "sk", "sv", "sf", "sp")}
    for l in range(depth):
        w_t = jnp.swapaxes(w_in[l], 0, 1)
        r_q, r_k, r_v, r_f = 0, fox_w, 2 * fox_w, 3 * fox_w
        r_u = r_f + FOX_HEADS
        r_qm = r_u + pool_w
        r_g = r_qm + mem_w
        b_f = jnp.pad(b_fgate[l], (0, 128 - FOX_HEADS))[None, :]
        w = dict(gates_t=w_t[r_g:].astype(BF16),
                 br_fox=w_br_fox[l].astype(BF16), br_pool=w_br_pool[l].astype(BF16),
                 br_mem=w_br_mem[l].astype(BF16), out=w_out[l].astype(BF16),
                 up=w_up[l].astype(BF16), down=w_down[l].astype(BF16),
                 ln1_g=ln1_g[l][None, :], ln1_b=ln1_b[l][None, :],
                 ln2_g=ln2_g[l][None, :], ln2_b=ln2_b[l][None, :])
        wp_bf = w_pool[l].astype(BF16)
        psc = pool_scale[l][None, :]
        logf_of = functools.partial(_mm, w=w_t, outs=[(F32, nat)], epilogue=_ep_log_sigmoid,
                                    extras=[(b_f, (1, 128), lambda i, j: (0, 0))], tn=128, row0=r_f,
                                    ncols=128, name="proj_logf")

        xs_bf, q_s = _mm(hs, w_t, [(BF16, nat)], functools.partial(_ep_scaled, scale=FOX_HD ** -0.5 * LOG2E),
                         row0=r_q, ncols=fox_w, tm=512, emit_x=True, name="proj_q")
        k_s, = _mm(xs_bf, w_t, [(F32, nat)], _ep_identity, row0=r_k, ncols=fox_w, name="proj_k")
        v_s, = _mm(xs_bf, w_t, [(F32, nat)], _ep_identity, row0=r_v, ncols=fox_w, name="proj_v")
        u_s, = _mm(xs_bf, w_t, [(F32, nat)], _ep_identity, row0=r_u, ncols=pool_w, name="proj_u")
        qm_s, = _mm(xs_bf, w_t, [(BF16, nat)], _ep_identity, row0=r_qm, ncols=mem_w, name="proj_qm")
        logf_s = logf_of(xs_bf)[0][:, :FOX_HEADS]

        x_bf, q2T = _mm(hp, w_t, [(BF16, tr)], functools.partial(_ep_scaled_t, scale=FOX_HD ** -0.5 * LOG2E),
                        row0=r_q, ncols=fox_w, tm=512, emit_x=True, name="proj_q")
        k, = _mm(x_bf, w_t, [(F32, nat)], _ep_identity, row0=r_k, ncols=fox_w, name="proj_k")
        v, vT = _mm(x_bf, w_t, [(F32, nat), (BF16, tr)], _ep_with_t, row0=r_v, ncols=fox_w, name="proj_v")
        u, = _mm(x_bf, w_t, [(F32, nat)], _ep_identity, row0=r_u, ncols=pool_w, name="proj_u")
        qm, = _mm(x_bf, w_t, [(BF16, nat)], _ep_identity, row0=r_qm, ncols=mem_w, name="proj_qm")
        logf = logf_of(x_bf)[0][:, :FOX_HEADS]
        c2_rows = _cumsum_lanes(logf.T, scale=LOG2E)
        o_foxT = _fox_prompt(q2T, k, vT, c2_rows)
        o_pool = _pool(u, pos_p, wp_bf, psc, tm=1024, zero_first=True)
        mkv, mkv_bf = _mm(mem_prompt.reshape(mem_tokens, d_model), w_mem_kv[l], [(F32, nat), (BF16, nat)],
                          _ep_dup, name="mem_kv")
        o_mem = _mem_prompt(qm, mkv_bf[:, :mem_w], mkv_bf[:, mem_w:])
        outs["pk"].append(k.reshape(batch, seq, FOX_HEADS, FOX_HD))
        outs["pv"].append(v.reshape(batch, seq, FOX_HEADS, FOX_HD))
        outs["pf"].append(logf.reshape(batch, seq, FOX_HEADS))
        outs["pp"].append(u[-POOL_STATE:].reshape(batch, POOL_STATE, pool_w))
        outs["pmk"].append(mkv[:, :mem_w].reshape(batch, mem_tokens, MEM_HEADS, MEM_HD))
        outs["pmv"].append(mkv[:, mem_w:].reshape(batch, mem_tokens, MEM_HEADS, MEM_HD))
        h_p, o_mem_s = _mix(hp, x_bf, o_foxT, o_pool, o_mem, w, alpha=alpha, fox_transposed=True,
                            rider=(qm_s.reshape(dec_batch, dec_seq * MEM_HEADS, MEM_HD), cache_mem_k[l],
                                   cache_mem_v[l]))
        rows = dec_seq * FOX_HEADS
        hp, o_fox_s = _ffn_fox(
            h_p, w["up"], w["down"], w["ln2_g"], w["ln2_b"],
            q_s.reshape(dec_batch, rows, FOX_HD), k_s.reshape(dec_batch, rows, FOX_HD),
            v_s.reshape(dec_batch, rows, FOX_HD), logf_s.reshape(dec_batch, 1, rows),
            cache_fox_k[l].reshape(n_phys, page_size * FOX_HEADS, FOX_HD),
            cache_fox_v[l].reshape(n_phys, page_size * FOX_HEADS, FOX_HD),
            _page_logf_sums(cache_fox_logf[l]), page_table, alpha=alpha)

        u3 = u_s.reshape(dec_batch, dec_seq, pool_w)
        state = state_pool[l].astype(F32)
        u_ext = jnp.concatenate([jnp.zeros((dec_batch, POOL_HALO - POOL_STATE, pool_w), F32), state, u3],
                                axis=1).reshape(dec_batch * seg, pool_w)
        o_pool_s = _pool(u_ext, pos_s, wp_bf, psc, tm=32 * seg, zero_first=False)
        o_pool_s = o_pool_s.reshape(dec_batch, seg, pool_w)[:, POOL_HALO:].reshape(dec_batch * dec_seq, pool_w)
        outs["sk"].append(k_s.reshape(dec_batch, dec_seq, FOX_HEADS, FOX_HD))
        outs["sv"].append(v_s.reshape(dec_batch, dec_seq, FOX_HEADS, FOX_HD))
        outs["sf"].append(logf_s.reshape(dec_batch, dec_seq, FOX_HEADS))
        outs["sp"].append(jnp.concatenate([state, u3], axis=1)[:, -POOL_STATE:])
        h_s, = _mix(hs, xs_bf, o_fox_s.reshape(dec_batch * dec_seq, fox_w), o_pool_s,
                    o_mem_s.reshape(dec_batch * dec_seq, mem_w), w, alpha=alpha, fox_transposed=False)
        hs = _ffn(h_s, w["up"], w["down"], w["ln2_g"], w["ln2_b"], alpha=alpha)

    st = lambda n: jnp.stack(outs[n])
    return (hp.reshape(batch, seq, d_model), hs.reshape(dec_batch, dec_seq, d_model),
            st("pk"), st("pv"), st("pf"), st("pp"), st("pmk"), st("pmv"),
            st("sk"), st("sv"), st("sf"), st("sp"))
```

```python
import functools

import numpy as np

import jax
import jax.numpy as jnp
from jax import lax
from jax.experimental import pallas as pl
from jax.experimental.pallas import tpu as pltpu

F32 = jnp.float32
BF16 = jnp.bfloat16

FOX_HEADS = 8
FOX_HD = 128
POOL_WINDOWS = (2, 4, 8, 16)
POOL_GW = 256
POOL_STATE = max(POOL_WINDOWS) - 1
POOL_HALO = 16
MEM_HEADS = 4
MEM_HD = 256
LN_EPS = 1e-5
MASK_VALUE = -1e30
LOG2E = 1.4426950408889634
FOX_EXT = 128
FOX_ONES = 16

V7X_VMEM_LIMIT_BYTES = 56 * 1024 * 1024


def _cparams(semantics):
    return pltpu.CompilerParams(dimension_semantics=semantics,
                                vmem_limit_bytes=V7X_VMEM_LIMIT_BYTES)


def _dot(a, b):
    return jnp.dot(a, b, preferred_element_type=F32)


def _dot_nt(a, b):
    return lax.dot_general(a, b, (((1,), (1,)), ((), ())), preferred_element_type=F32)


def _dot_tn(a, b):
    return lax.dot_general(a, b, (((0,), (0,)), ((), ())), preferred_element_type=F32)


def _split3(x):
    hi = x.astype(BF16)
    r1 = x - hi.astype(F32)
    mid = r1.astype(BF16)
    lo = (r1 - mid.astype(F32)).astype(BF16)
    return hi, mid, lo


def _split3_dot(x, t):
    hi, mid, lo = _split3(x)
    return _dot(hi, t) + _dot(mid, t) + _dot(lo, t)


def _layer_norm(z, g, b):
    mu = jnp.mean(z, axis=-1, keepdims=True)
    zc = z - mu
    var = jnp.mean(zc * zc, axis=-1, keepdims=True)
    return zc * lax.rsqrt(var + LN_EPS) * g + b


def _cast_specs(casts, steps, step_of):
    in_specs, out_specs, out_shape = [], [], []
    for src, row0, rows in casts:
        slab, cols = rows // steps, src.shape[1]
        assert rows % steps == 0 and slab % 16 == 0 and row0 % 8 == 0
        in_specs.append(pl.BlockSpec(
            (pl.Element(slab), pl.Element(cols)),
            lambda *g, row0=row0, slab=slab: (pl.multiple_of(row0 + step_of(*g) * slab, 8), 0)))
        out_specs.append(pl.BlockSpec((slab, cols), lambda *g: (step_of(*g), 0)))
        out_shape.append(jax.ShapeDtypeStruct((rows, cols), BF16))
    return in_specs, out_specs, out_shape


def _mm_body(*refs, epilogue, n_extra, n_cast, emit_x, w_rows):
    x_ref, w_ref = refs[0], refs[1]
    extras = refs[2:2 + n_extra]
    cast_in = refs[2 + n_extra:2 + n_extra + n_cast]
    outs = list(refs[2 + n_extra + n_cast:])
    cast_out = [outs.pop() for _ in range(n_cast)][::-1]
    x = x_ref[...].astype(BF16)
    if emit_x:
        outs.pop(0)[...] = x
    acc = (_dot_nt if w_rows else _dot)(x, w_ref[...].astype(BF16))
    res = epilogue(acc, *(e[...] for e in extras))
    for o, r in zip(outs, res):
        o[...] = r.astype(o.dtype)
    for s, o in zip(cast_in, cast_out):
        o[...] = s[...].astype(BF16)


def _mm(x, w, outs, epilogue, *, extras=(), casts=(), tm=1024, tn=1024, row0=None, ncols=None,
        emit_x=False, name):
    m, k = x.shape
    n = w.shape[1] if ncols is None else ncols
    tm, tn = min(tm, m), min(tn, n)
    assert m % tm == 0 and n % tn == 0 and (n == tn or not (emit_x or casts))
    if row0 is None:
        w_spec = pl.BlockSpec((k, tn), lambda i, j: (0, j))
    else:
        assert row0 % 8 == 0 and tn % 8 == 0
        w_spec = pl.BlockSpec((pl.Element(tn), pl.Element(k)),
                              lambda i, j: (pl.multiple_of(row0 + j * tn, 8), 0))
    in_specs = [pl.BlockSpec((tm, k), lambda i, j: (i, 0)), w_spec]
    in_specs += [pl.BlockSpec(bs, im) for _, bs, im in extras]
    out_specs = [pl.BlockSpec((tn, tm), lambda i, j: (j, i)) if t else
                 pl.BlockSpec((tm, tn), lambda i, j: (i, j)) for _, t in outs]
    out_shape = [jax.ShapeDtypeStruct((n, m) if t else (m, n), dt) for dt, t in outs]
    if emit_x:
        out_specs.insert(0, pl.BlockSpec((tm, k), lambda i, j: (i, 0)))
        out_shape.insert(0, jax.ShapeDtypeStruct((m, k), BF16))
    c_in, c_out, c_shape = _cast_specs(casts, m // tm, lambda i, j: i)
    return pl.pallas_call(
        functools.partial(_mm_body, epilogue=epilogue, n_extra=len(extras), n_cast=len(casts),
                          emit_x=emit_x, w_rows=row0 is not None),
        grid=(m // tm, n // tn),
        in_specs=in_specs + c_in, out_specs=out_specs + c_out, out_shape=out_shape + c_shape,
        compiler_params=_cparams(("parallel", "arbitrary")),
        name=name,
    )(x, w, *(a for a, _, _ in extras), *(src for src, _, _ in casts))


def _ep_identity(acc):
    return (acc,)


def _ep_dup(acc):
    return (acc, acc)


def _ep_scaled(acc, *, scale):
    return (acc * scale,)


def _ep_scaled_t(acc, *, scale):
    return ((acc * scale).T,)


def _ep_with_t(acc):
    return (acc, acc.T)


def _ep_log_sigmoid(acc, bias):
    z = acc + bias
    return (jnp.minimum(z, 0.0) - jnp.log1p(jnp.exp(-jnp.abs(z))),)


def _cumsum_body(x_ref, tri_ref, o_ref, *, chunk, scale):
    n = x_ref.shape[1]
    carry = jnp.zeros((x_ref.shape[0], 1), F32)
    for c in range(n // chunk):
        cs = _split3_dot(x_ref[:, c * chunk:(c + 1) * chunk], tri_ref[...]) + carry
        o_ref[:, c * chunk:(c + 1) * chunk] = cs * scale
        carry = cs[:, chunk - 1:chunk]


def _cumsum_lanes(x, *, scale, chunk=512):
    rows, n = x.shape
    tri = jnp.asarray(np.triu(np.ones((chunk, chunk), np.float32)), BF16)
    return pl.pallas_call(
        functools.partial(_cumsum_body, chunk=chunk, scale=scale),
        out_shape=jax.ShapeDtypeStruct((rows, n), F32),
        name="logf_cumsum",
    )(x, tri)


def _bias_rows(c, ones_first, n):
    sub = lax.broadcasted_iota(jnp.int32, (n, c.shape[1]), 0)
    c0, o0 = (3, 0) if ones_first else (0, 3)
    rows = jnp.where((sub >= o0) & (sub < o0 + 3), 1.0, 0.0).astype(BF16)
    for t, piece in enumerate(_split3(c)):
        rows = jnp.where(sub == c0 + t, piece, rows)
    return rows


def _fox_prompt_body(qT_ref, k_ref, vT_ref, c_ref, oT_ref, kext, s0, s1, m_scr, acc_scr, *, tq):
    i = pl.program_id(1)
    tk = tq

    @pl.when(i == 0)
    def _():
        eye = (lax.broadcasted_iota(jnp.int32, (FOX_ONES, FOX_EXT), 0)
               == lax.broadcasted_iota(jnp.int32, (FOX_ONES, FOX_EXT), 1)).astype(BF16)
        for c in range(k_ref.shape[0] // tk):
            rows = _bias_rows(-c_ref[0, :, c * tk:(c + 1) * tk], True, FOX_ONES)
            kext[c * tk:(c + 1) * tk, :] = _dot_tn(rows, eye).astype(BF16)

    last = pl.multiple_of((i + 1) * tq - 128, 128)
    cq = c_ref[0, :, pl.ds(last, 128)][:, 127:128]
    qa = jnp.concatenate([qT_ref[...], jnp.broadcast_to(_bias_rows(cq, False, FOX_EXT), (FOX_EXT, tq))],
                         axis=0)
    ones = jnp.ones((FOX_ONES, tk), BF16)

    def scores(j, dst):
        ks = pl.multiple_of(j * tk, tk)
        ka = jnp.concatenate([k_ref[pl.ds(ks, tk), :].astype(BF16), kext[pl.ds(ks, tk), :]], axis=1)
        dst[...] = _dot(ka, qa)

    def absorb(j, src, diag):
        ks = pl.multiple_of(j * tk, tk)
        sT = src[...]
        if diag:
            key = lax.broadcasted_iota(jnp.int32, (tk, tq), 0)
            qry = lax.broadcasted_iota(jnp.int32, (tk, tq), 1)
            sT = jnp.where(key <= qry, sT, MASK_VALUE)
        m = m_scr[...]
        m_new = jnp.maximum(m, jnp.max(sT, axis=0, keepdims=True))
        p = jnp.exp2(sT - m_new).astype(BF16)
        m_scr[...] = m_new
        va = jnp.concatenate([vT_ref[:, pl.ds(ks, tk)], ones], axis=0)
        acc_scr[...] = jnp.exp2(m - m_new) * acc_scr[...] + _dot(va, p)

    def finish(src):
        absorb(i, src, True)
        oT_ref[...] = (acc_scr[:FOX_HD, :] / acc_scr[FOX_HD:FOX_HD + 1, :]).astype(oT_ref.dtype)

    m_scr[...] = jnp.full(m_scr.shape, MASK_VALUE, F32)
    acc_scr[...] = jnp.zeros(acc_scr.shape, F32)
    scores(0, s0)

    def pair(t, carry):
        j = 2 * t
        scores(j + 1, s1)
        absorb(j, s0, False)
        scores(j + 2, s0)
        absorb(j + 1, s1, False)
        return carry

    lax.fori_loop(0, i // 2, pair, 0)

    @pl.when(i % 2 == 0)
    def _():
        finish(s0)

    @pl.when(i % 2 == 1)
    def _():
        scores(i, s1)
        absorb(i - 1, s0, False)
        finish(s1)


def _fox_prompt(q2T, k, vT, c2_rows, *, tq=1024):
    h, s = c2_rows.shape
    return pl.pallas_call(
        functools.partial(_fox_prompt_body, tq=tq),
        grid=(h, s // tq),
        in_specs=[pl.BlockSpec((FOX_HD, tq), lambda g, i: (g, i)),
                  pl.BlockSpec((s, FOX_HD), lambda g, i: (0, g)),
                  pl.BlockSpec((FOX_HD, s), lambda g, i: (g, 0)),
                  pl.BlockSpec((1, 1, s), lambda g, i: (g, 0, 0))],
        out_specs=pl.BlockSpec((FOX_HD, tq), lambda g, i: (g, i)),
        out_shape=jax.ShapeDtypeStruct((h * FOX_HD, s), BF16),
        scratch_shapes=[pltpu.VMEM((s, FOX_EXT), BF16), pltpu.VMEM((tq, tq), F32), pltpu.VMEM((tq, tq), F32),
                        pltpu.VMEM((1, tq), F32), pltpu.VMEM((FOX_HD + FOX_ONES, tq), F32)],
        compiler_params=_cparams(("parallel", "arbitrary")),
        name="fox_prompt_attention",
    )(q2T, k, vT, c2_rows[:, None, :])


def _ffn_fox_body(pt_ref, h_ref, wu_ref, wd_ref, g_ref, b_ref,
                  q_ref, kn_ref, vn_ref, lfn_ref, msk_ref, t4_ref, t5_ref, *rest, n_half, alpha):
    k_pages = rest[:n_half]
    v_pages = rest[n_half:2 * n_half]
    lf_pages = rest[2 * n_half:3 * n_half]
    y_ref, o_ref, hb_ref, s_scr, mbq_scr, m_scr, l_scr, acc_scr, later_scr = rest[3 * n_half:]
    del pt_ref
    kc = pl.program_id(1)
    rows = q_ref.shape[1]
    pk = k_pages[0].shape[1]
    nh = FOX_HEADS
    q = q_ref[0]
    first = kc % 2 == 0

    @pl.when(kc == 0)
    def _():
        hb_ref[...] = h_ref[...].astype(BF16)
        y_ref[...] = jnp.zeros_like(y_ref)

    @pl.when(first)
    def _():
        r = lfn_ref[0] * LOG2E
        bq = _split3_dot(jnp.broadcast_to(r, (rows, rows)) * msk_ref[...], t4_ref[...])
        cn = _split3_dot(jnp.broadcast_to(r, (8, rows)), t5_ref[...])[0:1]
        row_h = lax.broadcasted_iota(jnp.int32, (rows, 128), 0) & (nh - 1)
        lane_h = lax.broadcasted_iota(jnp.int32, (rows, 128), 1) & (nh - 1)
        same_head = row_h == lane_h
        mbq = jnp.where(same_head, bq, MASK_VALUE)
        mbq_scr[...] = jnp.concatenate([mbq] * (pk // 128), axis=1)
        row_t = lax.broadcasted_iota(jnp.int32, (rows, rows), 0) >> 3
        lane_t = lax.broadcasted_iota(jnp.int32, (rows, rows), 1) >> 3
        ok_new = same_head[:, :rows] & (lane_t <= row_t)
        s_new = jnp.where(ok_new, _dot_nt(q, kn_ref[0].astype(BF16)) + (bq[:, :rows] - cn), MASK_VALUE)
        m0 = jnp.max(s_new, axis=-1, keepdims=True)
        p_new = jnp.exp2(s_new - m0)
        m_scr[...] = m0
        l_scr[...] = jnp.sum(p_new, axis=-1, keepdims=True)
        acc_scr[...] = _dot(p_new.astype(BF16), vn_ref[0].astype(BF16))
        later_scr[...] = jnp.zeros_like(later_scr)

    later = later_scr[...]
    m_old = m_scr[...]
    m_new = m_old
    for p in reversed(range(n_half)):
        lf = lf_pages[p][0]
        tot = lf[:, pk:]
        d = jnp.concatenate([later + tot] * (pk // 128), axis=1) - lf[:, :pk]
        later = later + tot
        s = _dot_nt(q, k_pages[p][0].astype(BF16)) + (mbq_scr[...] + d)
        s_scr[:, p * pk:(p + 1) * pk] = s
        m_new = jnp.maximum(m_new, jnp.max(s, axis=-1, keepdims=True))
    later_scr[...] = later

    a = jnp.square(jnp.maximum(_dot(hb_ref[...], wu_ref[...]), 0.0))
    y_ref[...] += _dot(a.astype(BF16), wd_ref[...])

    scale_old = jnp.exp2(m_old - m_new)
    l = scale_old * l_scr[...]
    acc = scale_old * acc_scr[...]
    for p in range(n_half):
        pr = jnp.exp2(s_scr[:, p * pk:(p + 1) * pk] - m_new)
        l = l + jnp.sum(pr, axis=-1, keepdims=True)
        acc = acc + _dot(pr.astype(BF16), v_pages[p][0].astype(BF16))
    m_scr[...] = m_new
    l_scr[...] = l
    acc_scr[...] = acc

    @pl.when(jnp.logical_not(first))
    def _():
        o_ref[0] = (acc / l).astype(o_ref.dtype)

    @pl.when(kc == pl.num_programs(1) - 1)
    def _():
        y_ref[...] = _layer_norm(alpha * h_ref[...] + y_ref[...], g_ref[...], b_ref[...])


def _ffn_fox(h, w_up, w_down, g, b, q, k_new, v_new, lf_new, cache_k, cache_v, lf_cache, page_table,
             *, alpha, tm=512, tf=512):
    m, d = h.shape
    dff = w_up.shape[1]
    nb, rows, hd = q.shape
    n_pages = page_table.shape[1]
    n_half = n_pages // 2
    pk = cache_k.shape[1]
    ni, nk = m // tm, dff // tf
    per_i = nk // 2
    assert rows == 64 and FOX_HEADS == 8 and pk % 128 == 0 and n_pages % 2 == 0
    assert m % tm == 0 and dff % tf == 0 and nk % 2 == 0 and ni * per_i == nb

    rr = np.arange(rows)
    tt, hh = rr // FOX_HEADS, rr % FOX_HEADS
    lane = np.arange(128)
    msk = jnp.asarray((tt[None, :] <= tt[:, None]).astype(np.float32))
    t4 = jnp.asarray((hh[:, None] == (lane % FOX_HEADS)[None, :]).astype(np.float32), BF16)
    t5 = jnp.asarray(((hh[:, None] == hh[None, :]) & (tt[:, None] <= tt[None, :])).astype(np.float32), BF16)

    batch = lambda i, k: i * per_i + k // 2

    def page_map(p):
        return lambda i, k, pt: (pt[batch(i, k) * n_pages + (1 - k % 2) * n_half + p], 0, 0)

    per_b = lambda i, k, pt: (batch(i, k), 0, 0)
    const = lambda i, k, pt: (0, 0)
    in_specs = [pl.BlockSpec((tm, d), lambda i, k, pt: (i, 0)),
                pl.BlockSpec((d, tf), lambda i, k, pt: (0, k)),
                pl.BlockSpec((tf, d), lambda i, k, pt: (k, 0)),
                pl.BlockSpec((1, d), const),
                pl.BlockSpec((1, d), const),
                pl.BlockSpec((1, rows, hd), per_b),
                pl.BlockSpec((1, rows, hd), per_b),
                pl.BlockSpec((1, rows, hd), per_b),
                pl.BlockSpec((1, 1, rows), per_b),
                pl.BlockSpec((rows, rows), const),
                pl.BlockSpec((rows, 128), const),
                pl.BlockSpec((rows, rows), const)]
    in_specs += [pl.BlockSpec((1, pk, hd), page_map(p)) for p in range(n_half)]
    in_specs += [pl.BlockSpec((1, pk, hd), page_map(p)) for p in range(n_half)]
    in_specs += [pl.BlockSpec((1, 1, pk + 128), page_map(p)) for p in range(n_half)]
    grid_spec = pltpu.PrefetchScalarGridSpec(
        num_scalar_prefetch=1, grid=(ni, nk), in_specs=in_specs,
        out_specs=[pl.BlockSpec((tm, d), lambda i, k, pt: (i, 0)),
                   pl.BlockSpec((1, rows, hd), per_b)],
        scratch_shapes=[pltpu.VMEM((tm, d), BF16), pltpu.VMEM((rows, n_half * pk), F32),
                        pltpu.VMEM((rows, pk), F32), pltpu.VMEM((rows, 1), F32), pltpu.VMEM((rows, 1), F32),
                        pltpu.VMEM((rows, hd), F32), pltpu.VMEM((1, 128), F32)])
    return pl.pallas_call(
        functools.partial(_ffn_fox_body, n_half=n_half, alpha=alpha),
        grid_spec=grid_spec,
        out_shape=[jax.ShapeDtypeStruct((m, d), F32), jax.ShapeDtypeStruct((nb, rows, hd), BF16)],
        compiler_params=_cparams(("arbitrary", "arbitrary")),
        name="ffn_ln_fox_sample",
    )(page_table.reshape(-1), h, w_up, w_down, g, b, q, k_new, v_new, lf_new, msk, t4, t5,
      *([cache_k] * n_half), *([cache_v] * n_half), *([lf_cache] * n_half))


def _page_logf_sums(cache_logf):
    p, page, h = cache_logf.shape
    n = page * h
    pos, head = np.arange(n) // h, np.arange(n) % h
    prefix = (head[:, None] == head[None, :]) & (pos[:, None] <= pos[None, :])
    total = head[:, None] == (np.arange(128) % h)[None, :]
    t = jnp.asarray(np.concatenate([prefix, total], axis=1).astype(np.float32), BF16)
    tm = next(c for c in (512, 256, 128, 64, 32, 16, 8) if p % c == 0)

    def body(x_ref, t_ref, o_ref):
        o_ref[...] = _split3_dot(x_ref[...], t_ref[...]) * LOG2E

    out = pl.pallas_call(
        body, grid=(p // tm,),
        in_specs=[pl.BlockSpec((tm, n), lambda i: (i, 0)),
                  pl.BlockSpec((n, n + 128), lambda i: (0, 0))],
        out_specs=pl.BlockSpec((tm, n + 128), lambda i: (i, 0)),
        out_shape=jax.ShapeDtypeStruct((p, n + 128), F32),
        compiler_params=_cparams(("parallel",)),
        name="page_logf_sums",
    )(cache_logf.reshape(p, n), t)
    return out.reshape(p, 1, n + 128)


def _pool_body(u_ref, halo_ref, pos_ref, wp_ref, sc_ref, o_ref, *, zero_first):
    u = u_ref[...]
    halo = halo_ref[...]
    if zero_first:
        halo = jnp.where(pl.program_id(0) == 0, 0.0, halo)
    pos1 = pos_ref[...] + 1.0
    outs = []
    for g, w in enumerate(POOL_WINDOWS):
        sl = slice(g * POOL_GW, (g + 1) * POOL_GW)
        s = jnp.concatenate([halo[:, sl], u[:, sl]], axis=0)
        sh = 1
        while sh < w:
            s = s + pltpu.roll(s, sh, 0)
            sh *= 2
        d = s[POOL_HALO:] / jnp.minimum(float(w), pos1) - u[:, sl]
        outs.append(_dot(d.astype(BF16), wp_ref[g]))
    o_ref[...] = (jnp.concatenate(outs, axis=1) * sc_ref[...]).astype(o_ref.dtype)


def _pool(u, pos, w_pool, pool_scale, *, tm, zero_first):
    assert all(w & (w - 1) == 0 for w in POOL_WINDOWS) and tm % POOL_HALO == 0
    r, width = u.shape
    hb = tm // POOL_HALO
    return pl.pallas_call(
        functools.partial(_pool_body, zero_first=zero_first),
        grid=(r // tm,),
        in_specs=[pl.BlockSpec((tm, width), lambda i: (i, 0)),
                  pl.BlockSpec((POOL_HALO, width), lambda i: (jnp.maximum(i * hb - 1, 0), 0)),
                  pl.BlockSpec((tm, 1), lambda i: (i, 0)),
                  pl.BlockSpec(w_pool.shape, lambda i: (0, 0, 0)),
                  pl.BlockSpec((1, width), lambda i: (0, 0))],
        out_specs=pl.BlockSpec((tm, width), lambda i: (i, 0)),
        out_shape=jax.ShapeDtypeStruct((r, width), BF16),
        compiler_params=_cparams(("arbitrary",)),
        name="pool_mixer",
    )(u, u, pos, w_pool, pool_scale)


def _mem_prompt_body(q_ref, k_ref, v_ref, o_ref, *, scale):
    for h in range(MEM_HEADS):
        sl = slice(h * MEM_HD, (h + 1) * MEM_HD)
        s = _dot_nt(q_ref[:, sl], k_ref[:, sl]) * scale
        p = jnp.exp(s - jnp.max(s, axis=-1, keepdims=True))
        l = jnp.sum(p, axis=-1, keepdims=True)
        o_ref[:, sl] = (_dot(p.astype(BF16), v_ref[:, sl]) / l).astype(o_ref.dtype)


def _mem_prompt(qm, mk, mv, *, tq=1024):
    s, width = qm.shape
    m = mk.shape[0]
    return pl.pallas_call(
        functools.partial(_mem_prompt_body, scale=MEM_HD ** -0.5),
        grid=(s // tq,),
        in_specs=[pl.BlockSpec((tq, width), lambda i: (i, 0)),
                  pl.BlockSpec((m, width), lambda i: (0, 0)),
                  pl.BlockSpec((m, width), lambda i: (0, 0))],
        out_specs=pl.BlockSpec((tq, width), lambda i: (i, 0)),
        out_shape=jax.ShapeDtypeStruct((s, width), BF16),
        compiler_params=_cparams(("parallel",)),
        name="mem_prompt_attention",
    )(qm, mk, mv)


def _merge_body(x_ref, of_ref, op_ref, om_ref, wf_ref, wp_ref, wm_ref, g0_ref, g1_ref, g2_ref, *rest,
                fox_transposed, rider_bb, n_cast):
    rest = list(rest)
    cast_out = [rest.pop() for _ in range(n_cast)][::-1]
    if rider_bb:
        mq_ref, mk_ref, mv_ref, *cast_in, o_ref, mo_ref = rest
        rows = mq_ref.shape[1]
        mt, nh, hd = mk_ref.shape[1:]
        row_h = lax.broadcasted_iota(jnp.int32, (rows, mt * nh), 0) & (nh - 1)
        lane_h = lax.broadcasted_iota(jnp.int32, (rows, mt * nh), 1) & (nh - 1)
        own = row_h == lane_h
        probs = []
        for b in range(rider_bb):
            k2 = mk_ref[b].reshape(mt * nh, hd).astype(BF16)
            s = jnp.where(own, _dot_nt(mq_ref[b], k2) * hd ** -0.5, MASK_VALUE)
            p = jnp.exp(s - jnp.max(s, axis=-1, keepdims=True))
            probs.append((p.astype(BF16), jnp.sum(p, axis=-1, keepdims=True)))
    else:
        *cast_in, o_ref = rest
    for s, o in zip(cast_in, cast_out):
        o[...] = s[...].astype(BF16)

    x = x_ref[...]
    fox = (_dot_tn if fox_transposed else _dot)(of_ref[...], wf_ref[...])
    merged = (jax.nn.sigmoid(_dot_nt(x, g0_ref[...])) * fox
              + jax.nn.sigmoid(_dot_nt(x, g1_ref[...])) * _dot(op_ref[...], wp_ref[...])
              + jax.nn.sigmoid(_dot_nt(x, g2_ref[...])) * _dot(om_ref[...], wm_ref[...]))
    o_ref[...] = merged.astype(o_ref.dtype)

    for b in range(rider_bb):
        p, l = probs[b]
        v2 = mv_ref[b].reshape(mt * nh, hd).astype(BF16)
        mo_ref[b] = (_dot(p, v2) / l).astype(mo_ref.dtype)


def _merge(x, o_fox, o_pool, o_mem, w_f, w_p, w_m, w_gates_t, *, fox_transposed, rider=None, casts=(),
           tm=512, tn=512):
    m, d = x.shape
    kb = o_pool.shape[1]
    n = w_f.shape[1]
    tm = min(tm, m)
    nb = n // tn
    steps = (m // tm) * nb
    act = pl.BlockSpec((tm, kb), lambda i, j: (i, 0))
    act_fox = pl.BlockSpec((kb, tm), lambda i, j: (0, i)) if fox_transposed else act
    wgt = pl.BlockSpec((kb, tn), lambda i, j: (0, j))
    gate = lambda g: pl.BlockSpec((tn, d), lambda i, j: (g * nb + j, 0))
    in_specs = [pl.BlockSpec((tm, d), lambda i, j: (i, 0)), act_fox, act, act, wgt, wgt, wgt,
                gate(0), gate(1), gate(2)]
    out_specs = [pl.BlockSpec((tm, tn), lambda i, j: (i, j))]
    out_shape = [jax.ShapeDtypeStruct((m, n), BF16)]
    args = [x, o_fox, o_pool, o_mem, w_f, w_p, w_m, w_gates_t, w_gates_t, w_gates_t]
    bb = 0
    if rider is not None:
        qm, ck, cv = rider
        nbatch, rows, hd = qm.shape
        _, mt, nh, _ = ck.shape
        assert nh & (nh - 1) == 0 and nbatch % steps == 0
        bb = nbatch // steps
        per_step = lambda i, j: (i * nb + j, 0, 0)
        in_specs += [pl.BlockSpec((bb, rows, hd), per_step),
                     pl.BlockSpec((bb, mt, nh, hd), lambda i, j: (i * nb + j, 0, 0, 0)),
                     pl.BlockSpec((bb, mt, nh, hd), lambda i, j: (i * nb + j, 0, 0, 0))]
        out_specs.append(pl.BlockSpec((bb, rows, hd), per_step))
        out_shape.append(jax.ShapeDtypeStruct((nbatch, rows, hd), BF16))
        args += [qm, ck, cv]
    c_in, c_out, c_shape = _cast_specs(casts, steps, lambda i, j: i * nb + j)
    in_specs, out_specs, out_shape = in_specs + c_in, out_specs + c_out, out_shape + c_shape
    args += [src for src, _, _ in casts]
    return pl.pallas_call(
        functools.partial(_merge_body, fox_transposed=fox_transposed, rider_bb=bb, n_cast=len(casts)),
        grid=(m // tm, nb),
        in_specs=in_specs, out_specs=out_specs, out_shape=out_shape,
        compiler_params=_cparams(("arbitrary", "arbitrary")),
        name="branch_merge",
    )(*args)


def _ep_residual_ln(acc, x, g, b, *, alpha):
    return (_layer_norm(alpha * x + acc, g, b),)


def _ffn_body(h_ref, wu_ref, wd_ref, g_ref, b_ref, o_ref, hb_ref, *, alpha):
    k = pl.program_id(1)

    @pl.when(k == 0)
    def _():
        hb_ref[...] = h_ref[...].astype(BF16)
        o_ref[...] = jnp.zeros_like(o_ref)

    a = jnp.square(jnp.maximum(_dot(hb_ref[...], wu_ref[...]), 0.0))
    o_ref[...] += _dot(a.astype(BF16), wd_ref[...])

    @pl.when(k == pl.num_programs(1) - 1)
    def _():
        o_ref[...] = _layer_norm(alpha * h_ref[...] + o_ref[...], g_ref[...], b_ref[...])


def _ffn(h, w_up, w_down, g, b, *, alpha, tm=1024, tf=512):
    m, d = h.shape
    dff = w_up.shape[1]
    tm = min(tm, m)
    return pl.pallas_call(
        functools.partial(_ffn_body, alpha=alpha),
        grid=(m // tm, dff // tf),
        in_specs=[pl.BlockSpec((tm, d), lambda i, k: (i, 0)),
                  pl.BlockSpec((d, tf), lambda i, k: (0, k)),
                  pl.BlockSpec((tf, d), lambda i, k: (k, 0)),
                  pl.BlockSpec((1, d), lambda i, k: (0, 0)),
                  pl.BlockSpec((1, d), lambda i, k: (0, 0))],
        out_specs=pl.BlockSpec((tm, d), lambda i, k: (i, 0)),
        out_shape=jax.ShapeDtypeStruct((m, d), F32),
        scratch_shapes=[pltpu.VMEM((tm, d), BF16)],
        compiler_params=_cparams(("parallel", "arbitrary")),
        name="ffn_ln",
    )(h, w_up, w_down, g, b)


def _mix(x, x_bf, o_fox, o_pool, o_mem, w, *, alpha, fox_transposed, rider=None, merge_casts=(),
         out_casts=()):
    d = x.shape[1]
    merged, *side = _merge(x_bf, o_fox, o_pool, o_mem, w["br_fox"], w["br_pool"], w["br_mem"],
                           w["gates_t"], fox_transposed=fox_transposed, rider=rider, casts=merge_casts)
    row = lambda i, j: (0, 0)
    h, *copies = _mm(merged, w["out"], [(F32, False)], functools.partial(_ep_residual_ln, alpha=alpha),
                     extras=[(x, (min(512, x.shape[0]), d), lambda i, j: (i, 0)),
                             (w["ln1_g"], (1, d), row), (w["ln1_b"], (1, d), row)],
                     casts=out_casts, tm=512, tn=d, name="out_proj_ln")
    return h, side + copies


def kernel(x_prompt, x_sample, mem_prompt, cache_fox_k, cache_fox_v, cache_fox_logf, state_pool,
           cache_mem_k, cache_mem_v, page_table, w_in, b_fgate, w_pool, pool_scale, w_br_fox,
           w_br_pool, w_br_mem, w_mem_kv, w_out, ln1_g, ln1_b, w_up, w_down, ln2_g, ln2_b):
    batch, seq, d_model = x_prompt.shape
    dec_batch, dec_seq, _ = x_sample.shape
    depth = w_in.shape[0]
    n_phys, page_size = cache_fox_k.shape[1], cache_fox_k.shape[2]
    n_pages = page_table.shape[1]
    past = n_pages * page_size
    fox_w = FOX_HEADS * FOX_HD
    pool_w = len(POOL_WINDOWS) * POOL_GW
    mem_w = MEM_HEADS * MEM_HD
    mem_tokens = mem_prompt.shape[1]
    assert batch == 1 and dec_seq == FOX_HEADS == 8
    assert fox_w == pool_w == mem_w
    alpha = (2 * depth) ** 0.25
    nat, tr = False, True

    hp = x_prompt.reshape(seq, d_model)
    hs = x_sample.reshape(dec_batch * dec_seq, d_model)
    seg = POOL_HALO + dec_seq
    pos_p = jnp.arange(seq, dtype=F32)[:, None]
    pos_s = jnp.tile(jnp.concatenate([jnp.full((POOL_HALO,), 1e9, F32),
                                      past + jnp.arange(dec_seq, dtype=F32)]), dec_batch)[:, None]

    outs = {n: [] for n in ("pk", "pv", "pf", "pp", "pmk", "pmv", "sk", "sv", "sf", "sp")}
    for l in range(depth):
        w_t = jnp.swapaxes(w_in[l], 0, 1)
        r_q, r_k, r_v, r_f = 0, fox_w, 2 * fox_w, 3 * fox_w
        r_u = r_f + FOX_HEADS
        r_qm = r_u + pool_w
        r_g = r_qm + mem_w
        b_f = jnp.pad(b_fgate[l], (0, 128 - FOX_HEADS))[None, :]
        w = dict(ln1_g=ln1_g[l][None, :], ln1_b=ln1_b[l][None, :],
                 ln2_g=ln2_g[l][None, :], ln2_b=ln2_b[l][None, :])
        whole = lambda a: (a, 0, a.shape[0])
        wp_bf = w_pool[l].astype(BF16)
        psc = pool_scale[l][None, :]
        logf_of = functools.partial(_mm, w=w_t, outs=[(F32, nat)], epilogue=_ep_log_sigmoid,
                                    extras=[(b_f, (1, 128), lambda i, j: (0, 0))], tn=128, row0=r_f,
                                    ncols=128, name="proj_logf")

        xs_bf, q_s = _mm(hs, w_t, [(BF16, nat)], functools.partial(_ep_scaled, scale=FOX_HD ** -0.5 * LOG2E),
                         row0=r_q, ncols=fox_w, tm=512, emit_x=True, name="proj_q")
        k_s, = _mm(xs_bf, w_t, [(F32, nat)], _ep_identity, row0=r_k, ncols=fox_w, name="proj_k")
        v_s, = _mm(xs_bf, w_t, [(F32, nat)], _ep_identity, row0=r_v, ncols=fox_w, name="proj_v")
        u_s, = _mm(xs_bf, w_t, [(F32, nat)], _ep_identity, row0=r_u, ncols=pool_w, name="proj_u")
        qm_s, = _mm(xs_bf, w_t, [(BF16, nat)], _ep_identity, row0=r_qm, ncols=mem_w, name="proj_qm")
        logf_s = logf_of(xs_bf)[0][:, :FOX_HEADS]

        x_bf, q2T = _mm(hp, w_t, [(BF16, tr)], functools.partial(_ep_scaled_t, scale=FOX_HD ** -0.5 * LOG2E),
                        row0=r_q, ncols=fox_w, tm=512, emit_x=True, name="proj_q")
        k, w["gates_t"] = _mm(x_bf, w_t, [(F32, nat)], _ep_identity, row0=r_k, ncols=fox_w,
                              casts=[(w_t, r_g, 3 * d_model)], name="proj_k")
        v, vT = _mm(x_bf, w_t, [(F32, nat), (BF16, tr)], _ep_with_t, row0=r_v, ncols=fox_w, name="proj_v")
        u, w["br_fox"], w["br_pool"], w["br_mem"] = _mm(
            x_bf, w_t, [(F32, nat)], _ep_identity, row0=r_u, ncols=pool_w,
            casts=[whole(w_br_fox[l]), whole(w_br_pool[l]), whole(w_br_mem[l])], name="proj_u")
        qm, w["out"] = _mm(x_bf, w_t, [(BF16, nat)], _ep_identity, row0=r_qm, ncols=mem_w,
                           casts=[whole(w_out[l])], name="proj_qm")
        logf = logf_of(x_bf)[0][:, :FOX_HEADS]
        c2_rows = _cumsum_lanes(logf.T, scale=LOG2E)
        o_foxT = _fox_prompt(q2T, k, vT, c2_rows)
        o_pool = _pool(u, pos_p, wp_bf, psc, tm=1024, zero_first=True)
        mkv, mkv_bf = _mm(mem_prompt.reshape(mem_tokens, d_model), w_mem_kv[l], [(F32, nat), (BF16, nat)],
                          _ep_dup, name="mem_kv")
        o_mem = _mem_prompt(qm, mkv_bf[:, :mem_w], mkv_bf[:, mem_w:])
        outs["pk"].append(k.reshape(batch, seq, FOX_HEADS, FOX_HD))
        outs["pv"].append(v.reshape(batch, seq, FOX_HEADS, FOX_HD))
        outs["pf"].append(logf.reshape(batch, seq, FOX_HEADS))
        outs["pp"].append(u[-POOL_STATE:].reshape(batch, POOL_STATE, pool_w))
        outs["pmk"].append(mkv[:, :mem_w].reshape(batch, mem_tokens, MEM_HEADS, MEM_HD))
        outs["pmv"].append(mkv[:, mem_w:].reshape(batch, mem_tokens, MEM_HEADS, MEM_HD))
        h_p, (o_mem_s, w["down"], w["up"]) = _mix(
            hp, x_bf, o_foxT, o_pool, o_mem, w, alpha=alpha, fox_transposed=True,
            rider=(qm_s.reshape(dec_batch, dec_seq * MEM_HEADS, MEM_HD), cache_mem_k[l], cache_mem_v[l]),
            merge_casts=[whole(w_down[l])], out_casts=[whole(w_up[l])])
        rows = dec_seq * FOX_HEADS
        hp, o_fox_s = _ffn_fox(
            h_p, w["up"], w["down"], w["ln2_g"], w["ln2_b"],
            q_s.reshape(dec_batch, rows, FOX_HD), k_s.reshape(dec_batch, rows, FOX_HD),
            v_s.reshape(dec_batch, rows, FOX_HD), logf_s.reshape(dec_batch, 1, rows),
            cache_fox_k[l].reshape(n_phys, page_size * FOX_HEADS, FOX_HD),
            cache_fox_v[l].reshape(n_phys, page_size * FOX_HEADS, FOX_HD),
            _page_logf_sums(cache_fox_logf[l]), page_table, alpha=alpha)

        u3 = u_s.reshape(dec_batch, dec_seq, pool_w)
        state = state_pool[l].astype(F32)
        u_ext = jnp.concatenate([jnp.zeros((dec_batch, POOL_HALO - POOL_STATE, pool_w), F32), state, u3],
                                axis=1).reshape(dec_batch * seg, pool_w)
        o_pool_s = _pool(u_ext, pos_s, wp_bf, psc, tm=32 * seg, zero_first=False)
        o_pool_s = o_pool_s.reshape(dec_batch, seg, pool_w)[:, POOL_HALO:].reshape(dec_batch * dec_seq, pool_w)
        outs["sk"].append(k_s.reshape(dec_batch, dec_seq, FOX_HEADS, FOX_HD))
        outs["sv"].append(v_s.reshape(dec_batch, dec_seq, FOX_HEADS, FOX_HD))
        outs["sf"].append(logf_s.reshape(dec_batch, dec_seq, FOX_HEADS))
        outs["sp"].append(jnp.concatenate([state, u3], axis=1)[:, -POOL_STATE:])
        h_s, _ = _mix(hs, xs_bf, o_fox_s.reshape(dec_batch * dec_seq, fox_w), o_pool_s,
                      o_mem_s.reshape(dec_batch * dec_seq, mem_w), w, alpha=alpha, fox_transposed=False)
        hs = _ffn(h_s, w["up"], w["down"], w["ln2_g"], w["ln2_b"], alpha=alpha)

    st = lambda n: jnp.stack(outs[n])
    return (hp.reshape(batch, seq, d_model), hs.reshape(dec_batch, dec_seq, d_model),
            st("pk"), st("pv"), st("pf"), st("pp"), st("pmk"), st("pmv"),
            st("sk"), st("sv"), st("sf"), st("sp"))
```

```python
import functools

import numpy as np

import jax
import jax.numpy as jnp
from jax import lax
from jax.experimental import pallas as pl
from jax.experimental.pallas import tpu as pltpu

F32 = jnp.float32
BF16 = jnp.bfloat16

FOX_HEADS = 8
FOX_HD = 128
POOL_WINDOWS = (2, 4, 8, 16)
POOL_GW = 256
POOL_STATE = max(POOL_WINDOWS) - 1
POOL_HALO = 16
MEM_HEADS = 4
MEM_HD = 256
LN_EPS = 1e-5
MASK_VALUE = -1e30
LOG2E = 1.4426950408889634
FOX_EXT = 128
FOX_ONES = 16

V7X_VMEM_LIMIT_BYTES = 56 * 1024 * 1024


def _cparams(semantics):
    return pltpu.CompilerParams(dimension_semantics=semantics,
                                vmem_limit_bytes=V7X_VMEM_LIMIT_BYTES)


def _dot(a, b):
    return jnp.dot(a, b, preferred_element_type=F32)


def _dot_nt(a, b):
    return lax.dot_general(a, b, (((1,), (1,)), ((), ())), preferred_element_type=F32)


def _dot_tn(a, b):
    return lax.dot_general(a, b, (((0,), (0,)), ((), ())), preferred_element_type=F32)


def _split3(x):
    hi = x.astype(BF16)
    r1 = x - hi.astype(F32)
    mid = r1.astype(BF16)
    lo = (r1 - mid.astype(F32)).astype(BF16)
    return hi, mid, lo


def _split3_dot(x, t):
    hi, mid, lo = _split3(x)
    return _dot(hi, t) + _dot(mid, t) + _dot(lo, t)


def _layer_norm(z, g, b):
    mu = jnp.mean(z, axis=-1, keepdims=True)
    zc = z - mu
    var = jnp.mean(zc * zc, axis=-1, keepdims=True)
    return zc * lax.rsqrt(var + LN_EPS) * g + b


def _cast_specs(casts, steps, step_of):
    in_specs, out_specs, out_shape = [], [], []
    for src, row0, rows in casts:
        slab, cols = rows // steps, src.shape[1]
        assert rows % steps == 0 and slab % 16 == 0 and row0 % 8 == 0
        in_specs.append(pl.BlockSpec(
            (pl.Element(slab), pl.Element(cols)),
            lambda *g, row0=row0, slab=slab: (pl.multiple_of(row0 + step_of(*g) * slab, 8), 0)))
        out_specs.append(pl.BlockSpec((slab, cols), lambda *g: (step_of(*g), 0)))
        out_shape.append(jax.ShapeDtypeStruct((rows, cols), BF16))
    return in_specs, out_specs, out_shape


def _mm_body(*refs, epilogue, n_extra, n_cast, emit_x, w_rows):
    x_ref, w_ref = refs[0], refs[1]
    extras = refs[2:2 + n_extra]
    cast_in = refs[2 + n_extra:2 + n_extra + n_cast]
    outs = list(refs[2 + n_extra + n_cast:])
    cast_out = [outs.pop() for _ in range(n_cast)][::-1]
    x = x_ref[...].astype(BF16)
    if emit_x:
        outs.pop(0)[...] = x
    acc = (_dot_nt if w_rows else _dot)(x, w_ref[...].astype(BF16))
    res = epilogue(acc, *(e[...] for e in extras))
    for o, r in zip(outs, res):
        o[...] = r.astype(o.dtype)
    for s, o in zip(cast_in, cast_out):
        o[...] = s[...].astype(BF16)


def _mm(x, w, outs, epilogue, *, extras=(), casts=(), tm=1024, tn=1024, row0=None, ncols=None,
        emit_x=False, name):
    m, k = x.shape
    n = w.shape[1] if ncols is None else ncols
    tm, tn = min(tm, m), min(tn, n)
    assert m % tm == 0 and n % tn == 0 and (n == tn or not (emit_x or casts))
    if row0 is None:
        w_spec = pl.BlockSpec((k, tn), lambda i, j: (0, j))
    else:
        assert row0 % 8 == 0 and tn % 8 == 0
        w_spec = pl.BlockSpec((pl.Element(tn), pl.Element(k)),
                              lambda i, j: (pl.multiple_of(row0 + j * tn, 8), 0))
    in_specs = [pl.BlockSpec((tm, k), lambda i, j: (i, 0)), w_spec]
    in_specs += [pl.BlockSpec(bs, im) for _, bs, im in extras]
    out_specs = [pl.BlockSpec((tn, tm), lambda i, j: (j, i)) if t else
                 pl.BlockSpec((tm, tn), lambda i, j: (i, j)) for _, t in outs]
    out_shape = [jax.ShapeDtypeStruct((n, m) if t else (m, n), dt) for dt, t in outs]
    if emit_x:
        out_specs.insert(0, pl.BlockSpec((tm, k), lambda i, j: (i, 0)))
        out_shape.insert(0, jax.ShapeDtypeStruct((m, k), BF16))
    c_in, c_out, c_shape = _cast_specs(casts, m // tm, lambda i, j: i)
    return pl.pallas_call(
        functools.partial(_mm_body, epilogue=epilogue, n_extra=len(extras), n_cast=len(casts),
                          emit_x=emit_x, w_rows=row0 is not None),
        grid=(m // tm, n // tn),
        in_specs=in_specs + c_in, out_specs=out_specs + c_out, out_shape=out_shape + c_shape,
        compiler_params=_cparams(("parallel", "arbitrary")),
        name=name,
    )(x, w, *(a for a, _, _ in extras), *(src for src, _, _ in casts))


def _ep_identity(acc):
    return (acc,)


def _ep_dup(acc):
    return (acc, acc)


def _ep_scaled(acc, *, scale):
    return (acc * scale,)


def _ep_scaled_t(acc, *, scale):
    return ((acc * scale).T,)


def _ep_with_t(acc):
    return (acc, acc.T)


def _ep_log_sigmoid(acc, bias):
    z = acc + bias
    return (jnp.minimum(z, 0.0) - jnp.log1p(jnp.exp(-jnp.abs(z))),)


def _cumsum_body(x_ref, tri_ref, o_ref, *, chunk, scale):
    n = x_ref.shape[1]
    carry = jnp.zeros((x_ref.shape[0], 1), F32)
    for c in range(n // chunk):
        cs = _split3_dot(x_ref[:, c * chunk:(c + 1) * chunk], tri_ref[...]) + carry
        o_ref[:, c * chunk:(c + 1) * chunk] = cs * scale
        carry = cs[:, chunk - 1:chunk]


def _cumsum_lanes(x, *, scale, chunk=512):
    rows, n = x.shape
    tri = jnp.asarray(np.triu(np.ones((chunk, chunk), np.float32)), BF16)
    return pl.pallas_call(
        functools.partial(_cumsum_body, chunk=chunk, scale=scale),
        out_shape=jax.ShapeDtypeStruct((rows, n), F32),
        name="logf_cumsum",
    )(x, tri)


def _bias_rows(c, ones_first, n):
    sub = lax.broadcasted_iota(jnp.int32, (n, c.shape[1]), 0)
    c0, o0 = (3, 0) if ones_first else (0, 3)
    rows = jnp.where((sub >= o0) & (sub < o0 + 3), 1.0, 0.0).astype(BF16)
    for t, piece in enumerate(_split3(c)):
        rows = jnp.where(sub == c0 + t, piece, rows)
    return rows


def _fox_prompt_body(qT_ref, k_ref, vT_ref, c_ref, oT_ref, kext, s0, s1, m_scr, acc_scr, *, tq):
    i = pl.program_id(1)
    tk = tq

    @pl.when(i == 0)
    def _():
        eye = (lax.broadcasted_iota(jnp.int32, (FOX_ONES, FOX_EXT), 0)
               == lax.broadcasted_iota(jnp.int32, (FOX_ONES, FOX_EXT), 1)).astype(BF16)
        for c in range(k_ref.shape[0] // tk):
            rows = _bias_rows(-c_ref[0, :, c * tk:(c + 1) * tk], True, FOX_ONES)
            kext[c * tk:(c + 1) * tk, :] = _dot_tn(rows, eye).astype(BF16)

    last = pl.multiple_of((i + 1) * tq - 128, 128)
    cq = c_ref[0, :, pl.ds(last, 128)][:, 127:128]
    qa = jnp.concatenate([qT_ref[...], jnp.broadcast_to(_bias_rows(cq, False, FOX_EXT), (FOX_EXT, tq))],
                         axis=0)
    ones = jnp.ones((FOX_ONES, tk), BF16)

    def scores(j, dst):
        ks = pl.multiple_of(j * tk, tk)
        ka = jnp.concatenate([k_ref[pl.ds(ks, tk), :].astype(BF16), kext[pl.ds(ks, tk), :]], axis=1)
        dst[...] = _dot(ka, qa)

    def absorb(j, src, diag):
        ks = pl.multiple_of(j * tk, tk)
        sT = src[...]
        if diag:
            key = lax.broadcasted_iota(jnp.int32, (tk, tq), 0)
            qry = lax.broadcasted_iota(jnp.int32, (tk, tq), 1)
            sT = jnp.where(key <= qry, sT, MASK_VALUE)
        m = m_scr[...]
        m_new = jnp.maximum(m, jnp.max(sT, axis=0, keepdims=True))
        p = jnp.exp2(sT - m_new).astype(BF16)
        m_scr[...] = m_new
        va = jnp.concatenate([vT_ref[:, pl.ds(ks, tk)], ones], axis=0)
        acc_scr[...] = jnp.exp2(m - m_new) * acc_scr[...] + _dot(va, p)

    def finish(src):
        absorb(i, src, True)
        oT_ref[...] = (acc_scr[:FOX_HD, :] / acc_scr[FOX_HD:FOX_HD + 1, :]).astype(oT_ref.dtype)

    m_scr[...] = jnp.full(m_scr.shape, MASK_VALUE, F32)
    acc_scr[...] = jnp.zeros(acc_scr.shape, F32)
    scores(0, s0)

    def pair(t, carry):
        j = 2 * t
        scores(j + 1, s1)
        absorb(j, s0, False)
        scores(j + 2, s0)
        absorb(j + 1, s1, False)
        return carry

    lax.fori_loop(0, i // 2, pair, 0)

    @pl.when(i % 2 == 0)
    def _():
        finish(s0)

    @pl.when(i % 2 == 1)
    def _():
        scores(i, s1)
        absorb(i - 1, s0, False)
        finish(s1)


def _fox_prompt(q2T, k, vT, c2_rows, *, tq=1024):
    h, s = c2_rows.shape
    return pl.pallas_call(
        functools.partial(_fox_prompt_body, tq=tq),
        grid=(h, s // tq),
        in_specs=[pl.BlockSpec((FOX_HD, tq), lambda g, i: (g, i)),
                  pl.BlockSpec((s, FOX_HD), lambda g, i: (0, g)),
                  pl.BlockSpec((FOX_HD, s), lambda g, i: (g, 0)),
                  pl.BlockSpec((1, 1, s), lambda g, i: (g, 0, 0))],
        out_specs=pl.BlockSpec((FOX_HD, tq), lambda g, i: (g, i)),
        out_shape=jax.ShapeDtypeStruct((h * FOX_HD, s), BF16),
        scratch_shapes=[pltpu.VMEM((s, FOX_EXT), BF16), pltpu.VMEM((tq, tq), F32), pltpu.VMEM((tq, tq), F32),
                        pltpu.VMEM((1, tq), F32), pltpu.VMEM((FOX_HD + FOX_ONES, tq), F32)],
        compiler_params=_cparams(("parallel", "arbitrary")),
        name="fox_prompt_attention",
    )(q2T, k, vT, c2_rows[:, None, :])


def _ffn_fox_body(pt_ref, h_ref, wu_ref, wd_ref, g_ref, b_ref,
                  q_ref, kn_ref, vn_ref, lfn_ref, msk_ref, t4_ref, t5_ref, e_ref, *rest, n_half, alpha):
    k_pages = rest[:n_half]
    v_pages = rest[n_half:2 * n_half]
    lf_pages = rest[2 * n_half:3 * n_half]
    y_ref, o_ref, hb_ref, s_scr, mbq_scr, m_scr, l_scr, acc_scr, later_scr = rest[3 * n_half:]
    del pt_ref
    kc = pl.program_id(1)
    rows = q_ref.shape[1]
    pk = k_pages[0].shape[1]
    nh = FOX_HEADS
    q = q_ref[0]
    first = kc % 2 == 0

    @pl.when(kc == 0)
    def _():
        hb_ref[...] = h_ref[...].astype(BF16)
        y_ref[...] = jnp.zeros_like(y_ref)

    @pl.when(first)
    def _():
        r = lfn_ref[0] * LOG2E
        bq = _split3_dot(jnp.broadcast_to(r, (rows, rows)) * msk_ref[...], t4_ref[...])
        cn = _split3_dot(jnp.broadcast_to(r, (8, rows)), t5_ref[...])[0:1]
        row_h = lax.broadcasted_iota(jnp.int32, (rows, 128), 0) & (nh - 1)
        lane_h = lax.broadcasted_iota(jnp.int32, (rows, 128), 1) & (nh - 1)
        same_head = row_h == lane_h
        mbq = jnp.where(same_head, bq, MASK_VALUE)
        mbq_scr[...] = jnp.concatenate([mbq] * (pk // 128), axis=1)
        row_t = lax.broadcasted_iota(jnp.int32, (rows, rows), 0) >> 3
        lane_t = lax.broadcasted_iota(jnp.int32, (rows, rows), 1) >> 3
        ok_new = same_head[:, :rows] & (lane_t <= row_t)
        s_new = jnp.where(ok_new, _dot_nt(q, kn_ref[0].astype(BF16)) + (bq[:, :rows] - cn), MASK_VALUE)
        m0 = jnp.max(s_new, axis=-1, keepdims=True)
        p_new = jnp.exp2(s_new - m0)
        m_scr[...] = m0
        l_scr[...] = jnp.sum(p_new, axis=-1, keepdims=True)
        acc_scr[...] = _dot(p_new.astype(BF16), vn_ref[0].astype(BF16))
        later_scr[...] = jnp.zeros_like(later_scr)

    later = later_scr[...]
    suffix = [None] * n_half
    for p in reversed(range(n_half)):
        y = lf_pages[p][0]
        tot = jnp.broadcast_to(y[:, y.shape[1] - 1:], y.shape)
        suffix[p] = later + tot - y
        later = later + tot
    later_scr[...] = later
    spread = _split3_dot(jnp.concatenate(suffix, axis=0), e_ref[...])
    own = ((lax.broadcasted_iota(jnp.int32, spread.shape, 0) & (nh - 1))
           == (lax.broadcasted_iota(jnp.int32, spread.shape, 1) & (nh - 1)))
    d_all = jnp.sum(jnp.where(own, spread, 0.0).reshape(n_half, nh, pk), axis=1)
    m_old = m_scr[...]
    m_new = m_old
    for p in range(n_half):
        s = _dot_nt(q, k_pages[p][0].astype(BF16)) + (mbq_scr[...] + d_all[p:p + 1, :])
        s_scr[:, p * pk:(p + 1) * pk] = s
        m_new = jnp.maximum(m_new, jnp.max(s, axis=-1, keepdims=True))

    a = jnp.square(jnp.maximum(_dot(hb_ref[...], wu_ref[...]), 0.0))
    y_ref[...] += _dot(a.astype(BF16), wd_ref[...])

    scale_old = jnp.exp2(m_old - m_new)
    l = scale_old * l_scr[...]
    acc = scale_old * acc_scr[...]
    for p in range(n_half):
        pr = jnp.exp2(s_scr[:, p * pk:(p + 1) * pk] - m_new)
        l = l + jnp.sum(pr, axis=-1, keepdims=True)
        acc = acc + _dot(pr.astype(BF16), v_pages[p][0].astype(BF16))
    m_scr[...] = m_new
    l_scr[...] = l
    acc_scr[...] = acc

    @pl.when(jnp.logical_not(first))
    def _():
        o_ref[0] = (acc / l).astype(o_ref.dtype)

    @pl.when(kc == pl.num_programs(1) - 1)
    def _():
        y_ref[...] = _layer_norm(alpha * h_ref[...] + y_ref[...], g_ref[...], b_ref[...])


def _ffn_fox(h, w_up, w_down, g, b, q, k_new, v_new, lf_new, cache_k, cache_v, lf_cache, page_table,
             *, alpha, tm=512, tf=512):
    m, d = h.shape
    dff = w_up.shape[1]
    nb, rows, hd = q.shape
    n_pages = page_table.shape[1]
    n_half = n_pages // 2
    pk = cache_k.shape[1]
    _, nh, page = lf_cache.shape
    ni, nk = m // tm, dff // tf
    per_i = nk // 2
    assert rows == 64 and nh == FOX_HEADS == 8 and pk == page * nh and n_pages % 2 == 0
    assert m % tm == 0 and dff % tf == 0 and nk % 2 == 0 and ni * per_i == nb

    rr = np.arange(rows)
    tt, hh = rr // FOX_HEADS, rr % FOX_HEADS
    lane = np.arange(128)
    msk = jnp.asarray((tt[None, :] <= tt[:, None]).astype(np.float32))
    t4 = jnp.asarray((hh[:, None] == (lane % FOX_HEADS)[None, :]).astype(np.float32), BF16)
    t5 = jnp.asarray(((hh[:, None] == hh[None, :]) & (tt[:, None] <= tt[None, :])).astype(np.float32), BF16)
    spread = jnp.asarray((np.arange(page)[:, None] == (np.arange(pk) // nh)[None, :]).astype(np.float32), BF16)

    batch = lambda i, k: i * per_i + k // 2

    def page_map(p):
        return lambda i, k, pt: (pt[batch(i, k) * n_pages + (1 - k % 2) * n_half + p], 0, 0)

    per_b = lambda i, k, pt: (batch(i, k), 0, 0)
    const = lambda i, k, pt: (0, 0)
    in_specs = [pl.BlockSpec((tm, d), lambda i, k, pt: (i, 0)),
                pl.BlockSpec((d, tf), lambda i, k, pt: (0, k)),
                pl.BlockSpec((tf, d), lambda i, k, pt: (k, 0)),
                pl.BlockSpec((1, d), const),
                pl.BlockSpec((1, d), const),
                pl.BlockSpec((1, rows, hd), per_b),
                pl.BlockSpec((1, rows, hd), per_b),
                pl.BlockSpec((1, rows, hd), per_b),
                pl.BlockSpec((1, 1, rows), per_b),
                pl.BlockSpec((rows, rows), const),
                pl.BlockSpec((rows, 128), const),
                pl.BlockSpec((rows, rows), const),
                pl.BlockSpec((page, pk), const)]
    in_specs += [pl.BlockSpec((1, pk, hd), page_map(p)) for p in range(n_half)]
    in_specs += [pl.BlockSpec((1, pk, hd), page_map(p)) for p in range(n_half)]
    in_specs += [pl.BlockSpec((1, nh, page), page_map(p)) for p in range(n_half)]
    grid_spec = pltpu.PrefetchScalarGridSpec(
        num_scalar_prefetch=1, grid=(ni, nk), in_specs=in_specs,
        out_specs=[pl.BlockSpec((tm, d), lambda i, k, pt: (i, 0)),
                   pl.BlockSpec((1, rows, hd), per_b)],
        scratch_shapes=[pltpu.VMEM((tm, d), BF16), pltpu.VMEM((rows, n_half * pk), F32),
                        pltpu.VMEM((rows, pk), F32), pltpu.VMEM((rows, 1), F32), pltpu.VMEM((rows, 1), F32),
                        pltpu.VMEM((rows, hd), F32), pltpu.VMEM((nh, page), F32)])
    return pl.pallas_call(
        functools.partial(_ffn_fox_body, n_half=n_half, alpha=alpha),
        grid_spec=grid_spec,
        out_shape=[jax.ShapeDtypeStruct((m, d), F32), jax.ShapeDtypeStruct((nb, rows, hd), BF16)],
        compiler_params=_cparams(("arbitrary", "arbitrary")),
        name="ffn_ln_fox_sample",
    )(page_table.reshape(-1), h, w_up, w_down, g, b, q, k_new, v_new, lf_new, msk, t4, t5, spread,
      *([cache_k] * n_half), *([cache_v] * n_half), *([lf_cache] * n_half))


def _page_logf_prefix(cache_logf):
    p, page, h = cache_logf.shape
    tri = jnp.asarray(np.triu(np.ones((page, page), np.float32)), BF16)
    n = p * h
    tm = next(c for c in (2048, 1024, 512, 256, 128, 64, 32, 16, 8) if n % c == 0)

    def body(x_ref, t_ref, o_ref):
        o_ref[...] = _split3_dot(x_ref[...], t_ref[...]) * LOG2E

    out = pl.pallas_call(
        body, grid=(n // tm,),
        in_specs=[pl.BlockSpec((tm, page), lambda i: (i, 0)),
                  pl.BlockSpec((page, page), lambda i: (0, 0))],
        out_specs=pl.BlockSpec((tm, page), lambda i: (i, 0)),
        out_shape=jax.ShapeDtypeStruct((n, page), F32),
        compiler_params=_cparams(("parallel",)),
        name="page_logf_prefix",
    )(jnp.swapaxes(cache_logf, 1, 2).reshape(n, page), tri)
    return out.reshape(p, h, page)


def _pool_body(u_ref, halo_ref, pos_ref, wp_ref, sc_ref, o_ref, *, zero_first):
    u = u_ref[...]
    halo = halo_ref[...]
    if zero_first:
        halo = jnp.where(pl.program_id(0) == 0, 0.0, halo)
    pos1 = pos_ref[...] + 1.0
    outs = []
    for g, w in enumerate(POOL_WINDOWS):
        sl = slice(g * POOL_GW, (g + 1) * POOL_GW)
        s = jnp.concatenate([halo[:, sl], u[:, sl]], axis=0)
        sh = 1
        while sh < w:
            s = s + pltpu.roll(s, sh, 0)
            sh *= 2
        d = s[POOL_HALO:] / jnp.minimum(float(w), pos1) - u[:, sl]
        outs.append(_dot(d.astype(BF16), wp_ref[g]))
    o_ref[...] = (jnp.concatenate(outs, axis=1) * sc_ref[...]).astype(o_ref.dtype)


def _pool(u, pos, w_pool, pool_scale, *, tm, zero_first):
    assert all(w & (w - 1) == 0 for w in POOL_WINDOWS) and tm % POOL_HALO == 0
    r, width = u.shape
    hb = tm // POOL_HALO
    return pl.pallas_call(
        functools.partial(_pool_body, zero_first=zero_first),
        grid=(r // tm,),
        in_specs=[pl.BlockSpec((tm, width), lambda i: (i, 0)),
                  pl.BlockSpec((POOL_HALO, width), lambda i: (jnp.maximum(i * hb - 1, 0), 0)),
                  pl.BlockSpec((tm, 1), lambda i: (i, 0)),
                  pl.BlockSpec(w_pool.shape, lambda i: (0, 0, 0)),
                  pl.BlockSpec((1, width), lambda i: (0, 0))],
        out_specs=pl.BlockSpec((tm, width), lambda i: (i, 0)),
        out_shape=jax.ShapeDtypeStruct((r, width), BF16),
        compiler_params=_cparams(("arbitrary",)),
        name="pool_mixer",
    )(u, u, pos, w_pool, pool_scale)


def _mem_prompt_body(q_ref, k_ref, v_ref, o_ref, *, scale):
    for h in range(MEM_HEADS):
        sl = slice(h * MEM_HD, (h + 1) * MEM_HD)
        s = _dot_nt(q_ref[:, sl], k_ref[:, sl]) * scale
        p = jnp.exp(s - jnp.max(s, axis=-1, keepdims=True))
        l = jnp.sum(p, axis=-1, keepdims=True)
        o_ref[:, sl] = (_dot(p.astype(BF16), v_ref[:, sl]) / l).astype(o_ref.dtype)


def _mem_prompt(qm, mk, mv, *, tq=1024):
    s, width = qm.shape
    m = mk.shape[0]
    return pl.pallas_call(
        functools.partial(_mem_prompt_body, scale=MEM_HD ** -0.5),
        grid=(s // tq,),
        in_specs=[pl.BlockSpec((tq, width), lambda i: (i, 0)),
                  pl.BlockSpec((m, width), lambda i: (0, 0)),
                  pl.BlockSpec((m, width), lambda i: (0, 0))],
        out_specs=pl.BlockSpec((tq, width), lambda i: (i, 0)),
        out_shape=jax.ShapeDtypeStruct((s, width), BF16),
        compiler_params=_cparams(("parallel",)),
        name="mem_prompt_attention",
    )(qm, mk, mv)


def _merge_body(x_ref, of_ref, op_ref, om_ref, wf_ref, wp_ref, wm_ref, g0_ref, g1_ref, g2_ref, *rest,
                fox_transposed, rider_bb, n_cast):
    rest = list(rest)
    cast_out = [rest.pop() for _ in range(n_cast)][::-1]
    if rider_bb:
        mq_ref, mk_ref, mv_ref, *cast_in, o_ref, mo_ref = rest
        rows = mq_ref.shape[1]
        mt, nh, hd = mk_ref.shape[1:]
        row_h = lax.broadcasted_iota(jnp.int32, (rows, mt * nh), 0) & (nh - 1)
        lane_h = lax.broadcasted_iota(jnp.int32, (rows, mt * nh), 1) & (nh - 1)
        own = row_h == lane_h
        probs = []
        for b in range(rider_bb):
            k2 = mk_ref[b].reshape(mt * nh, hd).astype(BF16)
            s = jnp.where(own, _dot_nt(mq_ref[b], k2) * hd ** -0.5, MASK_VALUE)
            p = jnp.exp(s - jnp.max(s, axis=-1, keepdims=True))
            probs.append((p.astype(BF16), jnp.sum(p, axis=-1, keepdims=True)))
    else:
        *cast_in, o_ref = rest
    for s, o in zip(cast_in, cast_out):
        o[...] = s[...].astype(BF16)

    x = x_ref[...]
    fox = (_dot_tn if fox_transposed else _dot)(of_ref[...], wf_ref[...])
    merged = (jax.nn.sigmoid(_dot_nt(x, g0_ref[...])) * fox
              + jax.nn.sigmoid(_dot_nt(x, g1_ref[...])) * _dot(op_ref[...], wp_ref[...])
              + jax.nn.sigmoid(_dot_nt(x, g2_ref[...])) * _dot(om_ref[...], wm_ref[...]))
    o_ref[...] = merged.astype(o_ref.dtype)

    for b in range(rider_bb):
        p, l = probs[b]
        v2 = mv_ref[b].reshape(mt * nh, hd).astype(BF16)
        mo_ref[b] = (_dot(p, v2) / l).astype(mo_ref.dtype)


def _merge(x, o_fox, o_pool, o_mem, w_f, w_p, w_m, w_gates_t, *, fox_transposed, rider=None, casts=(),
           tm=512, tn=512):
    m, d = x.shape
    kb = o_pool.shape[1]
    n = w_f.shape[1]
    tm = min(tm, m)
    nb = n // tn
    steps = (m // tm) * nb
    act = pl.BlockSpec((tm, kb), lambda i, j: (i, 0))
    act_fox = pl.BlockSpec((kb, tm), lambda i, j: (0, i)) if fox_transposed else act
    wgt = pl.BlockSpec((kb, tn), lambda i, j: (0, j))
    gate = pl.BlockSpec((tn, d), lambda i, j: (j, 0))
    in_specs = [pl.BlockSpec((tm, d), lambda i, j: (i, 0)), act_fox, act, act, wgt, wgt, wgt,
                gate, gate, gate]
    out_specs = [pl.BlockSpec((tm, tn), lambda i, j: (i, j))]
    out_shape = [jax.ShapeDtypeStruct((m, n), BF16)]
    args = [x, o_fox, o_pool, o_mem, w_f, w_p, w_m, *w_gates_t]
    bb = 0
    if rider is not None:
        qm, ck, cv = rider
        nbatch, rows, hd = qm.shape
        _, mt, nh, _ = ck.shape
        assert nh & (nh - 1) == 0 and nbatch % steps == 0
        bb = nbatch // steps
        per_step = lambda i, j: (i * nb + j, 0, 0)
        in_specs += [pl.BlockSpec((bb, rows, hd), per_step),
                     pl.BlockSpec((bb, mt, nh, hd), lambda i, j: (i * nb + j, 0, 0, 0)),
                     pl.BlockSpec((bb, mt, nh, hd), lambda i, j: (i * nb + j, 0, 0, 0))]
        out_specs.append(pl.BlockSpec((bb, rows, hd), per_step))
        out_shape.append(jax.ShapeDtypeStruct((nbatch, rows, hd), BF16))
        args += [qm, ck, cv]
    c_in, c_out, c_shape = _cast_specs(casts, steps, lambda i, j: i * nb + j)
    in_specs, out_specs, out_shape = in_specs + c_in, out_specs + c_out, out_shape + c_shape
    args += [src for src, _, _ in casts]
    return pl.pallas_call(
        functools.partial(_merge_body, fox_transposed=fox_transposed, rider_bb=bb, n_cast=len(casts)),
        grid=(m // tm, nb),
        in_specs=in_specs, out_specs=out_specs, out_shape=out_shape,
        compiler_params=_cparams(("arbitrary", "arbitrary")),
        name="branch_merge",
    )(*args)


def _ep_residual_ln(acc, x, g, b, *, alpha):
    return (_layer_norm(alpha * x + acc, g, b),)


def _ffn_body(h_ref, wu_ref, wd_ref, g_ref, b_ref, o_ref, hb_ref, *, alpha):
    k = pl.program_id(1)

    @pl.when(k == 0)
    def _():
        hb_ref[...] = h_ref[...].astype(BF16)
        o_ref[...] = jnp.zeros_like(o_ref)

    a = jnp.square(jnp.maximum(_dot(hb_ref[...], wu_ref[...]), 0.0))
    o_ref[...] += _dot(a.astype(BF16), wd_ref[...])

    @pl.when(k == pl.num_programs(1) - 1)
    def _():
        o_ref[...] = _layer_norm(alpha * h_ref[...] + o_ref[...], g_ref[...], b_ref[...])


def _ffn(h, w_up, w_down, g, b, *, alpha, tm=1024, tf=512):
    m, d = h.shape
    dff = w_up.shape[1]
    tm = min(tm, m)
    return pl.pallas_call(
        functools.partial(_ffn_body, alpha=alpha),
        grid=(m // tm, dff // tf),
        in_specs=[pl.BlockSpec((tm, d), lambda i, k: (i, 0)),
                  pl.BlockSpec((d, tf), lambda i, k: (0, k)),
                  pl.BlockSpec((tf, d), lambda i, k: (k, 0)),
                  pl.BlockSpec((1, d), lambda i, k: (0, 0)),
                  pl.BlockSpec((1, d), lambda i, k: (0, 0))],
        out_specs=pl.BlockSpec((tm, d), lambda i, k: (i, 0)),
        out_shape=jax.ShapeDtypeStruct((m, d), F32),
        scratch_shapes=[pltpu.VMEM((tm, d), BF16)],
        compiler_params=_cparams(("parallel", "arbitrary")),
        name="ffn_ln",
    )(h, w_up, w_down, g, b)


def _mix(x, x_bf, o_fox, o_pool, o_mem, w, *, alpha, fox_transposed, rider=None, merge_casts=(),
         out_casts=()):
    d = x.shape[1]
    merged, *side = _merge(x_bf, o_fox, o_pool, o_mem, w["br_fox"], w["br_pool"], w["br_mem"],
                           w["gates_t"], fox_transposed=fox_transposed, rider=rider, casts=merge_casts)
    row = lambda i, j: (0, 0)
    h, *copies = _mm(merged, w["out"], [(F32, False)], functools.partial(_ep_residual_ln, alpha=alpha),
                     extras=[(x, (min(512, x.shape[0]), d), lambda i, j: (i, 0)),
                             (w["ln1_g"], (1, d), row), (w["ln1_b"], (1, d), row)],
                     casts=out_casts, tm=512, tn=d, name="out_proj_ln")
    return h, side + copies


def kernel(x_prompt, x_sample, mem_prompt, cache_fox_k, cache_fox_v, cache_fox_logf, state_pool,
           cache_mem_k, cache_mem_v, page_table, w_in, b_fgate, w_pool, pool_scale, w_br_fox,
           w_br_pool, w_br_mem, w_mem_kv, w_out, ln1_g, ln1_b, w_up, w_down, ln2_g, ln2_b):
    batch, seq, d_model = x_prompt.shape
    dec_batch, dec_seq, _ = x_sample.shape
    depth = w_in.shape[0]
    n_phys, page_size = cache_fox_k.shape[1], cache_fox_k.shape[2]
    n_pages = page_table.shape[1]
    past = n_pages * page_size
    fox_w = FOX_HEADS * FOX_HD
    pool_w = len(POOL_WINDOWS) * POOL_GW
    mem_w = MEM_HEADS * MEM_HD
    mem_tokens = mem_prompt.shape[1]
    assert batch == 1 and dec_seq == FOX_HEADS == 8
    assert fox_w == pool_w == mem_w
    alpha = (2 * depth) ** 0.25
    nat, tr = False, True

    hp = x_prompt.reshape(seq, d_model)
    hs = x_sample.reshape(dec_batch * dec_seq, d_model)
    seg = POOL_HALO + dec_seq
    pos_p = jnp.arange(seq, dtype=F32)[:, None]
    pos_s = jnp.tile(jnp.concatenate([jnp.full((POOL_HALO,), 1e9, F32),
                                      past + jnp.arange(dec_seq, dtype=F32)]), dec_batch)[:, None]

    outs = {n: [] for n in ("pk", "pv", "pf", "pp", "pmk", "pmv", "sk", "sv", "sf", "sp")}
    for l in range(depth):
        w_t = jnp.swapaxes(w_in[l], 0, 1)
        r_q, r_k, r_v, r_f = 0, fox_w, 2 * fox_w, 3 * fox_w
        r_u = r_f + FOX_HEADS
        r_qm = r_u + pool_w
        r_g = r_qm + mem_w
        b_f = jnp.pad(b_fgate[l], (0, 128 - FOX_HEADS))[None, :]
        w = dict(ln1_g=ln1_g[l][None, :], ln1_b=ln1_b[l][None, :],
                 ln2_g=ln2_g[l][None, :], ln2_b=ln2_b[l][None, :])
        whole = lambda a: (a, 0, a.shape[0])
        wp_bf = w_pool[l].astype(BF16)
        psc = pool_scale[l][None, :]
        logf_of = functools.partial(_mm, w=w_t, outs=[(F32, nat)], epilogue=_ep_log_sigmoid,
                                    extras=[(b_f, (1, 128), lambda i, j: (0, 0))], tn=128, row0=r_f,
                                    ncols=128, name="proj_logf")

        xs_bf, q_s = _mm(hs, w_t, [(BF16, nat)], functools.partial(_ep_scaled, scale=FOX_HD ** -0.5 * LOG2E),
                         row0=r_q, ncols=fox_w, tm=512, emit_x=True, name="proj_q")
        k_s, = _mm(xs_bf, w_t, [(F32, nat)], _ep_identity, row0=r_k, ncols=fox_w, name="proj_k")
        v_s, = _mm(xs_bf, w_t, [(F32, nat)], _ep_identity, row0=r_v, ncols=fox_w, name="proj_v")
        u_s, = _mm(xs_bf, w_t, [(F32, nat)], _ep_identity, row0=r_u, ncols=pool_w, name="proj_u")
        qm_s, = _mm(xs_bf, w_t, [(BF16, nat)], _ep_identity, row0=r_qm, ncols=mem_w, name="proj_qm")
        logf_s = logf_of(xs_bf)[0][:, :FOX_HEADS]

        x_bf, q2T = _mm(hp, w_t, [(BF16, tr)], functools.partial(_ep_scaled_t, scale=FOX_HD ** -0.5 * LOG2E),
                        row0=r_q, ncols=fox_w, tm=512, emit_x=True, name="proj_q")
        gate_rows = lambda g: (w_t, r_g + g * d_model, d_model)
        k, g0 = _mm(x_bf, w_t, [(F32, nat)], _ep_identity, row0=r_k, ncols=fox_w,
                    casts=[gate_rows(0)], name="proj_k")
        v, vT, g1 = _mm(x_bf, w_t, [(F32, nat), (BF16, tr)], _ep_with_t, row0=r_v, ncols=fox_w,
                        casts=[gate_rows(1)], name="proj_v")
        u, w["br_fox"], w["br_pool"], w["br_mem"] = _mm(
            x_bf, w_t, [(F32, nat)], _ep_identity, row0=r_u, ncols=pool_w,
            casts=[whole(w_br_fox[l]), whole(w_br_pool[l]), whole(w_br_mem[l])], name="proj_u")
        qm, g2, w["out"] = _mm(x_bf, w_t, [(BF16, nat)], _ep_identity, row0=r_qm, ncols=mem_w,
                               casts=[gate_rows(2), whole(w_out[l])], name="proj_qm")
        w["gates_t"] = (g0, g1, g2)
        logf = logf_of(x_bf)[0][:, :FOX_HEADS]
        c2_rows = _cumsum_lanes(logf.T, scale=LOG2E)
        o_foxT = _fox_prompt(q2T, k, vT, c2_rows)
        o_pool = _pool(u, pos_p, wp_bf, psc, tm=1024, zero_first=True)
        mkv, mkv_bf = _mm(mem_prompt.reshape(mem_tokens, d_model), w_mem_kv[l], [(F32, nat), (BF16, nat)],
                          _ep_dup, name="mem_kv")
        o_mem = _mem_prompt(qm, mkv_bf[:, :mem_w], mkv_bf[:, mem_w:])
        outs["pk"].append(k.reshape(batch, seq, FOX_HEADS, FOX_HD))
        outs["pv"].append(v.reshape(batch, seq, FOX_HEADS, FOX_HD))
        outs["pf"].append(logf.reshape(batch, seq, FOX_HEADS))
        outs["pp"].append(u[-POOL_STATE:].reshape(batch, POOL_STATE, pool_w))
        outs["pmk"].append(mkv[:, :mem_w].reshape(batch, mem_tokens, MEM_HEADS, MEM_HD))
        outs["pmv"].append(mkv[:, mem_w:].reshape(batch, mem_tokens, MEM_HEADS, MEM_HD))
        h_p, (o_mem_s, w["down"], w["up"]) = _mix(
            hp, x_bf, o_foxT, o_pool, o_mem, w, alpha=alpha, fox_transposed=True,
            rider=(qm_s.reshape(dec_batch, dec_seq * MEM_HEADS, MEM_HD), cache_mem_k[l], cache_mem_v[l]),
            merge_casts=[whole(w_down[l])], out_casts=[whole(w_up[l])])
        rows = dec_seq * FOX_HEADS
        hp, o_fox_s = _ffn_fox(
            h_p, w["up"], w["down"], w["ln2_g"], w["ln2_b"],
            q_s.reshape(dec_batch, rows, FOX_HD), k_s.reshape(dec_batch, rows, FOX_HD),
            v_s.reshape(dec_batch, rows, FOX_HD), logf_s.reshape(dec_batch, 1, rows),
            cache_fox_k[l].reshape(n_phys, page_size * FOX_HEADS, FOX_HD),
            cache_fox_v[l].reshape(n_phys, page_size * FOX_HEADS, FOX_HD),
            _page_logf_prefix(cache_fox_logf[l]), page_table, alpha=alpha)

        u3 = u_s.reshape(dec_batch, dec_seq, pool_w)
        state = state_pool[l].astype(F32)
        u_ext = jnp.concatenate([jnp.zeros((dec_batch, POOL_HALO - POOL_STATE, pool_w), F32), state, u3],
                                axis=1).reshape(dec_batch * seg, pool_w)
        o_pool_s = _pool(u_ext, pos_s, wp_bf, psc, tm=32 * seg, zero_first=False)
        o_pool_s = o_pool_s.reshape(dec_batch, seg, pool_w)[:, POOL_HALO:].reshape(dec_batch * dec_seq, pool_w)
        outs["sk"].append(k_s.reshape(dec_batch, dec_seq, FOX_HEADS, FOX_HD))
        outs["sv"].append(v_s.reshape(dec_batch, dec_seq, FOX_HEADS, FOX_HD))
        outs["sf"].append(logf_s.reshape(dec_batch, dec_seq, FOX_HEADS))
        outs["sp"].append(jnp.concatenate([state, u3], axis=1)[:, -POOL_STATE:])
        h_s, _ = _mix(hs, xs_bf, o_fox_s.reshape(dec_batch * dec_seq, fox_w), o_pool_s,
                      o_mem_s.reshape(dec_batch * dec_seq, mem_w), w, alpha=alpha, fox_transposed=False)
        hs = _ffn(h_s, w["up"], w["down"], w["ln2_g"], w["ln2_b"], alpha=alpha)

    st = lambda n: jnp.stack(outs[n])
    return (hp.reshape(batch, seq, d_model), hs.reshape(dec_batch, dec_seq, d_model),
            st("pk"), st("pv"), st("pf"), st("pp"), st("pmk"), st("pmv"),
            st("sk"), st("sv"), st("sf"), st("sp"))
```

```python
import functools

import numpy as np

import jax
import jax.numpy as jnp
from jax import lax
from jax.experimental import pallas as pl
from jax.experimental.pallas import tpu as pltpu

F32 = jnp.float32
BF16 = jnp.bfloat16

FOX_HEADS = 8
FOX_HD = 128
POOL_WINDOWS = (2, 4, 8, 16)
POOL_GW = 256
POOL_STATE = max(POOL_WINDOWS) - 1
POOL_HALO = 16
MEM_HEADS = 4
MEM_HD = 256
LN_EPS = 1e-5
MASK_VALUE = -1e30
LOG2E = 1.4426950408889634
FOX_EXT = 128
FOX_ONES = 16

V7X_VMEM_LIMIT_BYTES = 56 * 1024 * 1024


def _cparams(semantics):
    return pltpu.CompilerParams(dimension_semantics=semantics,
                                vmem_limit_bytes=V7X_VMEM_LIMIT_BYTES)


def _dot(a, b):
    return jnp.dot(a, b, preferred_element_type=F32)


def _dot_nt(a, b):
    return lax.dot_general(a, b, (((1,), (1,)), ((), ())), preferred_element_type=F32)


def _dot_tn(a, b):
    return lax.dot_general(a, b, (((0,), (0,)), ((), ())), preferred_element_type=F32)


def _split3(x):
    hi = x.astype(BF16)
    r1 = x - hi.astype(F32)
    mid = r1.astype(BF16)
    lo = (r1 - mid.astype(F32)).astype(BF16)
    return hi, mid, lo


def _split3_dot(x, t):
    hi, mid, lo = _split3(x)
    return _dot(hi, t) + _dot(mid, t) + _dot(lo, t)


def _layer_norm(z, g, b):
    mu = jnp.mean(z, axis=-1, keepdims=True)
    zc = z - mu
    var = jnp.mean(zc * zc, axis=-1, keepdims=True)
    return zc * lax.rsqrt(var + LN_EPS) * g + b


def _cast_specs(casts, steps, step_of):
    in_specs, out_specs, out_shape = [], [], []
    for src, row0, rows in casts:
        slab, cols = rows // steps, src.shape[1]
        assert rows % steps == 0 and slab % 16 == 0 and row0 % 8 == 0
        in_specs.append(pl.BlockSpec(
            (pl.Element(slab), pl.Element(cols)),
            lambda *g, row0=row0, slab=slab: (pl.multiple_of(row0 + step_of(*g) * slab, 8), 0)))
        out_specs.append(pl.BlockSpec((slab, cols), lambda *g: (step_of(*g), 0)))
        out_shape.append(jax.ShapeDtypeStruct((rows, cols), BF16))
    return in_specs, out_specs, out_shape


def _mm_body(*refs, epilogue, n_extra, n_cast, emit_x, w_rows):
    x_ref, w_ref = refs[0], refs[1]
    extras = refs[2:2 + n_extra]
    cast_in = refs[2 + n_extra:2 + n_extra + n_cast]
    outs = list(refs[2 + n_extra + n_cast:])
    cast_out = [outs.pop() for _ in range(n_cast)][::-1]
    x = x_ref[...].astype(BF16)
    if emit_x:
        outs.pop(0)[...] = x
    acc = (_dot_nt if w_rows else _dot)(x, w_ref[...].astype(BF16))
    res = epilogue(acc, *(e[...] for e in extras))
    for o, r in zip(outs, res):
        o[...] = r.astype(o.dtype)
    for s, o in zip(cast_in, cast_out):
        o[...] = s[...].astype(BF16)


def _mm(x, w, outs, epilogue, *, extras=(), casts=(), tm=1024, tn=1024, row0=None, ncols=None,
        emit_x=False, name):
    m, k = x.shape
    n = w.shape[1] if ncols is None else ncols
    tm, tn = min(tm, m), min(tn, n)
    assert m % tm == 0 and n % tn == 0 and (n == tn or not (emit_x or casts))
    if row0 is None:
        w_spec = pl.BlockSpec((k, tn), lambda i, j: (0, j))
    else:
        assert row0 % 8 == 0 and tn % 8 == 0
        w_spec = pl.BlockSpec((pl.Element(tn), pl.Element(k)),
                              lambda i, j: (pl.multiple_of(row0 + j * tn, 8), 0))
    in_specs = [pl.BlockSpec((tm, k), lambda i, j: (i, 0)), w_spec]
    in_specs += [pl.BlockSpec(bs, im) for _, bs, im in extras]
    out_specs = [pl.BlockSpec((tn, tm), lambda i, j: (j, i)) if t else
                 pl.BlockSpec((tm, tn), lambda i, j: (i, j)) for _, t in outs]
    out_shape = [jax.ShapeDtypeStruct((n, m) if t else (m, n), dt) for dt, t in outs]
    if emit_x:
        out_specs.insert(0, pl.BlockSpec((tm, k), lambda i, j: (i, 0)))
        out_shape.insert(0, jax.ShapeDtypeStruct((m, k), BF16))
    c_in, c_out, c_shape = _cast_specs(casts, m // tm, lambda i, j: i)
    return pl.pallas_call(
        functools.partial(_mm_body, epilogue=epilogue, n_extra=len(extras), n_cast=len(casts),
                          emit_x=emit_x, w_rows=row0 is not None),
        grid=(m // tm, n // tn),
        in_specs=in_specs + c_in, out_specs=out_specs + c_out, out_shape=out_shape + c_shape,
        compiler_params=_cparams(("parallel", "arbitrary")),
        name=name,
    )(x, w, *(a for a, _, _ in extras), *(src for src, _, _ in casts))


def _ep_identity(acc):
    return (acc,)


def _ep_dup(acc):
    return (acc, acc)


def _ep_scaled(acc, *, scale):
    return (acc * scale,)


def _ep_scaled_t(acc, *, scale):
    return ((acc * scale).T,)


def _ep_with_t(acc):
    return (acc, acc.T)


def _ep_log_sigmoid(acc, bias):
    z = acc + bias
    return (jnp.minimum(z, 0.0) - jnp.log1p(jnp.exp(-jnp.abs(z))),)


def _cumsum_body(x_ref, tri_ref, o_ref, *, chunk, scale):
    n = x_ref.shape[1]
    carry = jnp.zeros((x_ref.shape[0], 1), F32)
    for c in range(n // chunk):
        cs = _split3_dot(x_ref[:, c * chunk:(c + 1) * chunk], tri_ref[...]) + carry
        o_ref[:, c * chunk:(c + 1) * chunk] = cs * scale
        carry = cs[:, chunk - 1:chunk]


def _cumsum_lanes(x, *, scale, chunk=512):
    rows, n = x.shape
    tri = jnp.asarray(np.triu(np.ones((chunk, chunk), np.float32)), BF16)
    return pl.pallas_call(
        functools.partial(_cumsum_body, chunk=chunk, scale=scale),
        out_shape=jax.ShapeDtypeStruct((rows, n), F32),
        name="logf_cumsum",
    )(x, tri)


def _bias_rows(c, ones_first, n):
    sub = lax.broadcasted_iota(jnp.int32, (n, c.shape[1]), 0)
    c0, o0 = (3, 0) if ones_first else (0, 3)
    rows = jnp.where((sub >= o0) & (sub < o0 + 3), 1.0, 0.0).astype(BF16)
    for t, piece in enumerate(_split3(c)):
        rows = jnp.where(sub == c0 + t, piece, rows)
    return rows


def _fox_prompt_body(qT_ref, k_ref, vT_ref, c_ref, oT_ref, kext, causal, s0, s1, m_scr, acc_scr, *, tq):
    i = pl.program_id(1)
    tk = tq

    @pl.when(i == 0)
    def _():
        key = lax.broadcasted_iota(jnp.int32, (tk, tq), 0)
        qry = lax.broadcasted_iota(jnp.int32, (tk, tq), 1)
        causal[...] = jnp.where(key <= qry, 0.0, MASK_VALUE)
        eye = (lax.broadcasted_iota(jnp.int32, (FOX_ONES, FOX_EXT), 0)
               == lax.broadcasted_iota(jnp.int32, (FOX_ONES, FOX_EXT), 1)).astype(BF16)
        for c in range(k_ref.shape[0] // tk):
            rows = _bias_rows(-c_ref[0, :, c * tk:(c + 1) * tk], True, FOX_ONES)
            kext[c * tk:(c + 1) * tk, :] = _dot_tn(rows, eye).astype(BF16)

    last = pl.multiple_of((i + 1) * tq - 128, 128)
    cq = c_ref[0, :, pl.ds(last, 128)][:, 127:128]
    qa = jnp.concatenate([qT_ref[...], jnp.broadcast_to(_bias_rows(cq, False, FOX_EXT), (FOX_EXT, tq))],
                         axis=0)
    ones = jnp.ones((FOX_ONES, tk), BF16)

    def scores(j, dst):
        ks = pl.multiple_of(j * tk, tk)
        ka = jnp.concatenate([k_ref[pl.ds(ks, tk), :].astype(BF16), kext[pl.ds(ks, tk), :]], axis=1)
        dst[...] = _dot(ka, qa)

    def absorb(j, src, diag):
        ks = pl.multiple_of(j * tk, tk)
        sT = src[...]
        if diag:
            sT = sT + causal[...]
        m = m_scr[...]
        m_new = jnp.maximum(m, jnp.max(sT, axis=0, keepdims=True))
        p = jnp.exp2(sT - m_new).astype(BF16)
        m_scr[...] = m_new
        va = jnp.concatenate([vT_ref[:, pl.ds(ks, tk)], ones], axis=0)
        acc_scr[...] = jnp.exp2(m - m_new) * acc_scr[...] + _dot(va, p)

    def finish(src):
        absorb(i, src, True)
        oT_ref[...] = (acc_scr[:FOX_HD, :] / acc_scr[FOX_HD:FOX_HD + 1, :]).astype(oT_ref.dtype)

    m_scr[...] = jnp.full(m_scr.shape, MASK_VALUE, F32)
    acc_scr[...] = jnp.zeros(acc_scr.shape, F32)
    scores(0, s0)

    def pair(t, carry):
        j = 2 * t
        scores(j + 1, s1)
        absorb(j, s0, False)
        scores(j + 2, s0)
        absorb(j + 1, s1, False)
        return carry

    lax.fori_loop(0, i // 2, pair, 0)

    @pl.when(i % 2 == 0)
    def _():
        finish(s0)

    @pl.when(i % 2 == 1)
    def _():
        scores(i, s1)
        absorb(i - 1, s0, False)
        finish(s1)


def _fox_prompt(q2T, k, vT, c2_rows, *, tq=1024):
    h, s = c2_rows.shape
    return pl.pallas_call(
        functools.partial(_fox_prompt_body, tq=tq),
        grid=(h, s // tq),
        in_specs=[pl.BlockSpec((FOX_HD, tq), lambda g, i: (g, i)),
                  pl.BlockSpec((s, FOX_HD), lambda g, i: (0, g)),
                  pl.BlockSpec((FOX_HD, s), lambda g, i: (g, 0)),
                  pl.BlockSpec((1, 1, s), lambda g, i: (g, 0, 0))],
        out_specs=pl.BlockSpec((FOX_HD, tq), lambda g, i: (g, i)),
        out_shape=jax.ShapeDtypeStruct((h * FOX_HD, s), BF16),
        scratch_shapes=[pltpu.VMEM((s, FOX_EXT), BF16), pltpu.VMEM((tq, tq), F32), pltpu.VMEM((tq, tq), F32),
                        pltpu.VMEM((tq, tq), F32), pltpu.VMEM((1, tq), F32),
                        pltpu.VMEM((FOX_HD + FOX_ONES, tq), F32)],
        compiler_params=_cparams(("parallel", "arbitrary")),
        name="fox_prompt_attention",
    )(q2T, k, vT, c2_rows[:, None, :])


def _ffn_fox_body(pt_ref, h_ref, wu_ref, wd_ref, g_ref, b_ref,
                  q_ref, kn_ref, vn_ref, lfn_ref, msk_ref, t4_ref, t5_ref, *rest, n_half, alpha):
    k_pages = rest[:n_half]
    v_pages = rest[n_half:2 * n_half]
    lf_pages = rest[2 * n_half:3 * n_half]
    y_ref, o_ref, hb_ref, s_scr, mbq_scr, m_scr, l_scr, acc_scr, later_scr = rest[3 * n_half:]
    del pt_ref
    kc = pl.program_id(1)
    rows = q_ref.shape[1]
    pk = k_pages[0].shape[1]
    nh = FOX_HEADS
    q = q_ref[0]
    first = kc % 2 == 0

    @pl.when(kc == 0)
    def _():
        hb_ref[...] = h_ref[...].astype(BF16)
        y_ref[...] = jnp.zeros_like(y_ref)

    @pl.when(first)
    def _():
        r = lfn_ref[0] * LOG2E
        bq = _split3_dot(jnp.broadcast_to(r, (rows, rows)) * msk_ref[...], t4_ref[...])
        cn = _split3_dot(jnp.broadcast_to(r, (8, rows)), t5_ref[...])[0:1]
        row_h = lax.broadcasted_iota(jnp.int32, (rows, 128), 0) & (nh - 1)
        lane_h = lax.broadcasted_iota(jnp.int32, (rows, 128), 1) & (nh - 1)
        same_head = row_h == lane_h
        mbq = jnp.where(same_head, bq, MASK_VALUE)
        mbq_scr[...] = jnp.concatenate([mbq] * (pk // 128), axis=1)
        row_t = lax.broadcasted_iota(jnp.int32, (rows, rows), 0) >> 3
        lane_t = lax.broadcasted_iota(jnp.int32, (rows, rows), 1) >> 3
        ok_new = same_head[:, :rows] & (lane_t <= row_t)
        s_new = jnp.where(ok_new, _dot_nt(q, kn_ref[0].astype(BF16)) + (bq[:, :rows] - cn), MASK_VALUE)
        m0 = jnp.max(s_new, axis=-1, keepdims=True)
        p_new = jnp.exp2(s_new - m0)
        m_scr[...] = m0
        l_scr[...] = jnp.sum(p_new, axis=-1, keepdims=True)
        acc_scr[...] = _dot(p_new.astype(BF16), vn_ref[0].astype(BF16))
        later_scr[...] = jnp.zeros_like(later_scr)

    later = later_scr[...]
    m_old = m_scr[...]
    m_new = m_old
    for p in reversed(range(n_half)):
        lf = lf_pages[p][0]
        tot = lf[:, pk:]
        d = jnp.concatenate([later + tot] * (pk // 128), axis=1) - lf[:, :pk]
        later = later + tot
        s = _dot_nt(q, k_pages[p][0].astype(BF16)) + (mbq_scr[...] + d)
        s_scr[:, p * pk:(p + 1) * pk] = s
        m_new = jnp.maximum(m_new, jnp.max(s, axis=-1, keepdims=True))
    later_scr[...] = later

    a = jnp.square(jnp.maximum(_dot(hb_ref[...], wu_ref[...]), 0.0))
    y_ref[...] += _dot(a.astype(BF16), wd_ref[...])

    scale_old = jnp.exp2(m_old - m_new)
    l = scale_old * l_scr[...]
    acc = scale_old * acc_scr[...]
    for p in range(n_half):
        pr = jnp.exp2(s_scr[:, p * pk:(p + 1) * pk] - m_new)
        l = l + jnp.sum(pr, axis=-1, keepdims=True)
        acc = acc + _dot(pr.astype(BF16), v_pages[p][0].astype(BF16))
    m_scr[...] = m_new
    l_scr[...] = l
    acc_scr[...] = acc

    @pl.when(jnp.logical_not(first))
    def _():
        o_ref[0] = (acc / l).astype(o_ref.dtype)

    @pl.when(kc == pl.num_programs(1) - 1)
    def _():
        y_ref[...] = _layer_norm(alpha * h_ref[...] + y_ref[...], g_ref[...], b_ref[...])


def _ffn_fox(h, w_up, w_down, g, b, q, k_new, v_new, lf_new, cache_k, cache_v, lf_cache, page_table,
             *, alpha, tm=512, tf=512):
    m, d = h.shape
    dff = w_up.shape[1]
    nb, rows, hd = q.shape
    n_pages = page_table.shape[1]
    n_half = n_pages // 2
    pk = cache_k.shape[1]
    ni, nk = m // tm, dff // tf
    per_i = nk // 2
    assert rows == 64 and FOX_HEADS == 8 and pk % 128 == 0 and n_pages % 2 == 0
    assert m % tm == 0 and dff % tf == 0 and nk % 2 == 0 and ni * per_i == nb

    rr = np.arange(rows)
    tt, hh = rr // FOX_HEADS, rr % FOX_HEADS
    lane = np.arange(128)
    msk = jnp.asarray((tt[None, :] <= tt[:, None]).astype(np.float32))
    t4 = jnp.asarray((hh[:, None] == (lane % FOX_HEADS)[None, :]).astype(np.float32), BF16)
    t5 = jnp.asarray(((hh[:, None] == hh[None, :]) & (tt[:, None] <= tt[None, :])).astype(np.float32), BF16)

    batch = lambda i, k: i * per_i + k // 2

    def page_map(p):
        return lambda i, k, pt: (pt[batch(i, k) * n_pages + (1 - k % 2) * n_half + p], 0, 0)

    per_b = lambda i, k, pt: (batch(i, k), 0, 0)
    const = lambda i, k, pt: (0, 0)
    in_specs = [pl.BlockSpec((tm, d), lambda i, k, pt: (i, 0)),
                pl.BlockSpec((d, tf), lambda i, k, pt: (0, k)),
                pl.BlockSpec((tf, d), lambda i, k, pt: (k, 0)),
                pl.BlockSpec((1, d), const),
                pl.BlockSpec((1, d), const),
                pl.BlockSpec((1, rows, hd), per_b),
                pl.BlockSpec((1, rows, hd), per_b),
                pl.BlockSpec((1, rows, hd), per_b),
                pl.BlockSpec((1, 1, rows), per_b),
                pl.BlockSpec((rows, rows), const),
                pl.BlockSpec((rows, 128), const),
                pl.BlockSpec((rows, rows), const)]
    in_specs += [pl.BlockSpec((1, pk, hd), page_map(p)) for p in range(n_half)]
    in_specs += [pl.BlockSpec((1, pk, hd), page_map(p)) for p in range(n_half)]
    in_specs += [pl.BlockSpec((1, 1, pk + 128), page_map(p)) for p in range(n_half)]
    grid_spec = pltpu.PrefetchScalarGridSpec(
        num_scalar_prefetch=1, grid=(ni, nk), in_specs=in_specs,
        out_specs=[pl.BlockSpec((tm, d), lambda i, k, pt: (i, 0)),
                   pl.BlockSpec((1, rows, hd), per_b)],
        scratch_shapes=[pltpu.VMEM((tm, d), BF16), pltpu.VMEM((rows, n_half * pk), F32),
                        pltpu.VMEM((rows, pk), F32), pltpu.VMEM((rows, 1), F32), pltpu.VMEM((rows, 1), F32),
                        pltpu.VMEM((rows, hd), F32), pltpu.VMEM((1, 128), F32)])
    return pl.pallas_call(
        functools.partial(_ffn_fox_body, n_half=n_half, alpha=alpha),
        grid_spec=grid_spec,
        out_shape=[jax.ShapeDtypeStruct((m, d), F32), jax.ShapeDtypeStruct((nb, rows, hd), BF16)],
        compiler_params=_cparams(("arbitrary", "arbitrary")),
        name="ffn_ln_fox_sample",
    )(page_table.reshape(-1), h, w_up, w_down, g, b, q, k_new, v_new, lf_new, msk, t4, t5,
      *([cache_k] * n_half), *([cache_v] * n_half), *([lf_cache] * n_half))


def _page_logf_sums(cache_logf):
    p, page, h = cache_logf.shape
    n = page * h
    pos, head = np.arange(n) // h, np.arange(n) % h
    prefix = (head[:, None] == head[None, :]) & (pos[:, None] <= pos[None, :])
    total = head[:, None] == (np.arange(128) % h)[None, :]
    t = jnp.asarray(np.concatenate([prefix, total], axis=1).astype(np.float32), BF16)
    tm = next(c for c in (512, 256, 128, 64, 32, 16, 8) if p % c == 0)

    def body(x_ref, t_ref, o_ref):
        o_ref[...] = _split3_dot(x_ref[...], t_ref[...]) * LOG2E

    out = pl.pallas_call(
        body, grid=(p // tm,),
        in_specs=[pl.BlockSpec((tm, n), lambda i: (i, 0)),
                  pl.BlockSpec((n, n + 128), lambda i: (0, 0))],
        out_specs=pl.BlockSpec((tm, n + 128), lambda i: (i, 0)),
        out_shape=jax.ShapeDtypeStruct((p, n + 128), F32),
        compiler_params=_cparams(("parallel",)),
        name="page_logf_sums",
    )(cache_logf.reshape(p, n), t)
    return out.reshape(p, 1, n + 128)


def _pool_body(u_ref, halo_ref, pos_ref, wp_ref, sc_ref, o_ref, *, zero_first):
    u = u_ref[...]
    halo = halo_ref[...]
    if zero_first:
        halo = jnp.where(pl.program_id(0) == 0, 0.0, halo)
    pos1 = pos_ref[...] + 1.0
    outs = []
    for g, w in enumerate(POOL_WINDOWS):
        sl = slice(g * POOL_GW, (g + 1) * POOL_GW)
        s = jnp.concatenate([halo[:, sl], u[:, sl]], axis=0)
        sh = 1
        while sh < w:
            s = s + pltpu.roll(s, sh, 0)
            sh *= 2
        d = s[POOL_HALO:] / jnp.minimum(float(w), pos1) - u[:, sl]
        outs.append(_dot(d.astype(BF16), wp_ref[g]))
    o_ref[...] = (jnp.concatenate(outs, axis=1) * sc_ref[...]).astype(o_ref.dtype)


def _pool(u, pos, w_pool, pool_scale, *, tm, zero_first):
    assert all(w & (w - 1) == 0 for w in POOL_WINDOWS) and tm % POOL_HALO == 0
    r, width = u.shape
    hb = tm // POOL_HALO
    return pl.pallas_call(
        functools.partial(_pool_body, zero_first=zero_first),
        grid=(r // tm,),
        in_specs=[pl.BlockSpec((tm, width), lambda i: (i, 0)),
                  pl.BlockSpec((POOL_HALO, width), lambda i: (jnp.maximum(i * hb - 1, 0), 0)),
                  pl.BlockSpec((tm, 1), lambda i: (i, 0)),
                  pl.BlockSpec(w_pool.shape, lambda i: (0, 0, 0)),
                  pl.BlockSpec((1, width), lambda i: (0, 0))],
        out_specs=pl.BlockSpec((tm, width), lambda i: (i, 0)),
        out_shape=jax.ShapeDtypeStruct((r, width), BF16),
        compiler_params=_cparams(("arbitrary",)),
        name="pool_mixer",
    )(u, u, pos, w_pool, pool_scale)


def _mem_prompt_body(q_ref, k_ref, v_ref, o_ref, *, scale):
    for h in range(MEM_HEADS):
        sl = slice(h * MEM_HD, (h + 1) * MEM_HD)
        s = _dot_nt(q_ref[:, sl], k_ref[:, sl]) * scale
        p = jnp.exp(s - jnp.max(s, axis=-1, keepdims=True))
        l = jnp.sum(p, axis=-1, keepdims=True)
        o_ref[:, sl] = (_dot(p.astype(BF16), v_ref[:, sl]) / l).astype(o_ref.dtype)


def _mem_prompt(qm, mk, mv, *, tq=1024):
    s, width = qm.shape
    m = mk.shape[0]
    return pl.pallas_call(
        functools.partial(_mem_prompt_body, scale=MEM_HD ** -0.5),
        grid=(s // tq,),
        in_specs=[pl.BlockSpec((tq, width), lambda i: (i, 0)),
                  pl.BlockSpec((m, width), lambda i: (0, 0)),
                  pl.BlockSpec((m, width), lambda i: (0, 0))],
        out_specs=pl.BlockSpec((tq, width), lambda i: (i, 0)),
        out_shape=jax.ShapeDtypeStruct((s, width), BF16),
        compiler_params=_cparams(("parallel",)),
        name="mem_prompt_attention",
    )(qm, mk, mv)


def _merge_body(x_ref, of_ref, op_ref, om_ref, wf_ref, wp_ref, wm_ref, g0_ref, g1_ref, g2_ref, *rest,
                fox_transposed, rider_bb, n_cast):
    rest = list(rest)
    cast_out = [rest.pop() for _ in range(n_cast)][::-1]
    if rider_bb:
        mq_ref, mk_ref, mv_ref, *cast_in, o_ref, mo_ref = rest
        rows = mq_ref.shape[1]
        mt, nh, hd = mk_ref.shape[1:]
        row_h = lax.broadcasted_iota(jnp.int32, (rows, mt * nh), 0) & (nh - 1)
        lane_h = lax.broadcasted_iota(jnp.int32, (rows, mt * nh), 1) & (nh - 1)
        own = row_h == lane_h
        probs = []
        for b in range(rider_bb):
            k2 = mk_ref[b].reshape(mt * nh, hd).astype(BF16)
            s = jnp.where(own, _dot_nt(mq_ref[b], k2) * hd ** -0.5, MASK_VALUE)
            p = jnp.exp(s - jnp.max(s, axis=-1, keepdims=True))
            probs.append((p.astype(BF16), jnp.sum(p, axis=-1, keepdims=True)))
    else:
        *cast_in, o_ref = rest
    for s, o in zip(cast_in, cast_out):
        o[...] = s[...].astype(BF16)

    x = x_ref[...]
    fox = (_dot_tn if fox_transposed else _dot)(of_ref[...], wf_ref[...])
    merged = (jax.nn.sigmoid(_dot_nt(x, g0_ref[...])) * fox
              + jax.nn.sigmoid(_dot_nt(x, g1_ref[...])) * _dot(op_ref[...], wp_ref[...])
              + jax.nn.sigmoid(_dot_nt(x, g2_ref[...])) * _dot(om_ref[...], wm_ref[...]))
    o_ref[...] = merged.astype(o_ref.dtype)

    for b in range(rider_bb):
        p, l = probs[b]
        v2 = mv_ref[b].reshape(mt * nh, hd).astype(BF16)
        mo_ref[b] = (_dot(p, v2) / l).astype(mo_ref.dtype)


def _merge(x, o_fox, o_pool, o_mem, w_f, w_p, w_m, w_gates_t, *, fox_transposed, rider=None, casts=(),
           tm=512, tn=512):
    m, d = x.shape
    kb = o_pool.shape[1]
    n = w_f.shape[1]
    tm = min(tm, m)
    nb = n // tn
    steps = (m // tm) * nb
    act = pl.BlockSpec((tm, kb), lambda i, j: (i, 0))
    act_fox = pl.BlockSpec((kb, tm), lambda i, j: (0, i)) if fox_transposed else act
    wgt = pl.BlockSpec((kb, tn), lambda i, j: (0, j))
    gate = lambda g: pl.BlockSpec((tn, d), lambda i, j: (g * nb + j, 0))
    in_specs = [pl.BlockSpec((tm, d), lambda i, j: (i, 0)), act_fox, act, act, wgt, wgt, wgt,
                gate(0), gate(1), gate(2)]
    out_specs = [pl.BlockSpec((tm, tn), lambda i, j: (i, j))]
    out_shape = [jax.ShapeDtypeStruct((m, n), BF16)]
    args = [x, o_fox, o_pool, o_mem, w_f, w_p, w_m, w_gates_t, w_gates_t, w_gates_t]
    bb = 0
    if rider is not None:
        qm, ck, cv = rider
        nbatch, rows, hd = qm.shape
        _, mt, nh, _ = ck.shape
        assert nh & (nh - 1) == 0 and nbatch % steps == 0
        bb = nbatch // steps
        per_step = lambda i, j: (i * nb + j, 0, 0)
        in_specs += [pl.BlockSpec((bb, rows, hd), per_step),
                     pl.BlockSpec((bb, mt, nh, hd), lambda i, j: (i * nb + j, 0, 0, 0)),
                     pl.BlockSpec((bb, mt, nh, hd), lambda i, j: (i * nb + j, 0, 0, 0))]
        out_specs.append(pl.BlockSpec((bb, rows, hd), per_step))
        out_shape.append(jax.ShapeDtypeStruct((nbatch, rows, hd), BF16))
        args += [qm, ck, cv]
    c_in, c_out, c_shape = _cast_specs(casts, steps, lambda i, j: i * nb + j)
    in_specs, out_specs, out_shape = in_specs + c_in, out_specs + c_out, out_shape + c_shape
    args += [src for src, _, _ in casts]
    return pl.pallas_call(
        functools.partial(_merge_body, fox_transposed=fox_transposed, rider_bb=bb, n_cast=len(casts)),
        grid=(m // tm, nb),
        in_specs=in_specs, out_specs=out_specs, out_shape=out_shape,
        compiler_params=_cparams(("arbitrary", "arbitrary")),
        name="branch_merge",
    )(*args)


def _ep_residual_ln(acc, x, g, b, *, alpha):
    return (_layer_norm(alpha * x + acc, g, b),)


def _ffn_body(h_ref, wu_ref, wd_ref, g_ref, b_ref, o_ref, hb_ref, *, alpha):
    k = pl.program_id(1)

    @pl.when(k == 0)
    def _():
        hb_ref[...] = h_ref[...].astype(BF16)
        o_ref[...] = jnp.zeros_like(o_ref)

    a = jnp.square(jnp.maximum(_dot(hb_ref[...], wu_ref[...]), 0.0))
    o_ref[...] += _dot(a.astype(BF16), wd_ref[...])

    @pl.when(k == pl.num_programs(1) - 1)
    def _():
        o_ref[...] = _layer_norm(alpha * h_ref[...] + o_ref[...], g_ref[...], b_ref[...])


def _ffn(h, w_up, w_down, g, b, *, alpha, tm=1024, tf=512):
    m, d = h.shape
    dff = w_up.shape[1]
    tm = min(tm, m)
    return pl.pallas_call(
        functools.partial(_ffn_body, alpha=alpha),
        grid=(m // tm, dff // tf),
        in_specs=[pl.BlockSpec((tm, d), lambda i, k: (i, 0)),
                  pl.BlockSpec((d, tf), lambda i, k: (0, k)),
                  pl.BlockSpec((tf, d), lambda i, k: (k, 0)),
                  pl.BlockSpec((1, d), lambda i, k: (0, 0)),
                  pl.BlockSpec((1, d), lambda i, k: (0, 0))],
        out_specs=pl.BlockSpec((tm, d), lambda i, k: (i, 0)),
        out_shape=jax.ShapeDtypeStruct((m, d), F32),
        scratch_shapes=[pltpu.VMEM((tm, d), BF16)],
        compiler_params=_cparams(("parallel", "arbitrary")),
        name="ffn_ln",
    )(h, w_up, w_down, g, b)


def _mix(x, x_bf, o_fox, o_pool, o_mem, w, *, alpha, fox_transposed, rider=None, merge_casts=(),
         out_casts=()):
    d = x.shape[1]
    merged, *side = _merge(x_bf, o_fox, o_pool, o_mem, w["br_fox"], w["br_pool"], w["br_mem"],
                           w["gates_t"], fox_transposed=fox_transposed, rider=rider, casts=merge_casts)
    row = lambda i, j: (0, 0)
    h, *copies = _mm(merged, w["out"], [(F32, False)], functools.partial(_ep_residual_ln, alpha=alpha),
                     extras=[(x, (min(512, x.shape[0]), d), lambda i, j: (i, 0)),
                             (w["ln1_g"], (1, d), row), (w["ln1_b"], (1, d), row)],
                     casts=out_casts, tm=512, tn=d, name="out_proj_ln")
    return h, side + copies


def kernel(x_prompt, x_sample, mem_prompt, cache_fox_k, cache_fox_v, cache_fox_logf, state_pool,
           cache_mem_k, cache_mem_v, page_table, w_in, b_fgate, w_pool, pool_scale, w_br_fox,
           w_br_pool, w_br_mem, w_mem_kv, w_out, ln1_g, ln1_b, w_up, w_down, ln2_g, ln2_b):
    batch, seq, d_model = x_prompt.shape
    dec_batch, dec_seq, _ = x_sample.shape
    depth = w_in.shape[0]
    n_phys, page_size = cache_fox_k.shape[1], cache_fox_k.shape[2]
    n_pages = page_table.shape[1]
    past = n_pages * page_size
    fox_w = FOX_HEADS * FOX_HD
    pool_w = len(POOL_WINDOWS) * POOL_GW
    mem_w = MEM_HEADS * MEM_HD
    mem_tokens = mem_prompt.shape[1]
    assert batch == 1 and dec_seq == FOX_HEADS == 8
    assert fox_w == pool_w == mem_w
    alpha = (2 * depth) ** 0.25
    nat, tr = False, True

    hp = x_prompt.reshape(seq, d_model)
    hs = x_sample.reshape(dec_batch * dec_seq, d_model)
    seg = POOL_HALO + dec_seq
    pos_p = jnp.arange(seq, dtype=F32)[:, None]
    pos_s = jnp.tile(jnp.concatenate([jnp.full((POOL_HALO,), 1e9, F32),
                                      past + jnp.arange(dec_seq, dtype=F32)]), dec_batch)[:, None]

    outs = {n: [] for n in ("pk", "pv", "pf", "pp", "pmk", "pmv", "sk", "sv", "sf", "sp")}
    for l in range(depth):
        w_t = jnp.swapaxes(w_in[l], 0, 1)
        r_q, r_k, r_v, r_f = 0, fox_w, 2 * fox_w, 3 * fox_w
        r_u = r_f + FOX_HEADS
        r_qm = r_u + pool_w
        r_g = r_qm + mem_w
        b_f = jnp.pad(b_fgate[l], (0, 128 - FOX_HEADS))[None, :]
        w = dict(ln1_g=ln1_g[l][None, :], ln1_b=ln1_b[l][None, :],
                 ln2_g=ln2_g[l][None, :], ln2_b=ln2_b[l][None, :])
        whole = lambda a: (a, 0, a.shape[0])
        wp_bf = w_pool[l].astype(BF16)
        psc = pool_scale[l][None, :]
        logf_of = functools.partial(_mm, w=w_t, outs=[(F32, nat)], epilogue=_ep_log_sigmoid,
                                    extras=[(b_f, (1, 128), lambda i, j: (0, 0))], tn=128, row0=r_f,
                                    ncols=128, name="proj_logf")

        xs_bf, q_s = _mm(hs, w_t, [(BF16, nat)], functools.partial(_ep_scaled, scale=FOX_HD ** -0.5 * LOG2E),
                         row0=r_q, ncols=fox_w, tm=512, emit_x=True, name="proj_q")
        k_s, = _mm(xs_bf, w_t, [(F32, nat)], _ep_identity, row0=r_k, ncols=fox_w, name="proj_k")
        v_s, = _mm(xs_bf, w_t, [(F32, nat)], _ep_identity, row0=r_v, ncols=fox_w, name="proj_v")
        u_s, = _mm(xs_bf, w_t, [(F32, nat)], _ep_identity, row0=r_u, ncols=pool_w, name="proj_u")
        qm_s, = _mm(xs_bf, w_t, [(BF16, nat)], _ep_identity, row0=r_qm, ncols=mem_w, name="proj_qm")
        logf_s = logf_of(xs_bf)[0][:, :FOX_HEADS]

        x_bf, q2T = _mm(hp, w_t, [(BF16, tr)], functools.partial(_ep_scaled_t, scale=FOX_HD ** -0.5 * LOG2E),
                        row0=r_q, ncols=fox_w, tm=512, emit_x=True, name="proj_q")
        k, w["gates_t"] = _mm(x_bf, w_t, [(F32, nat)], _ep_identity, row0=r_k, ncols=fox_w,
                              casts=[(w_t, r_g, 3 * d_model)], name="proj_k")
        v, vT = _mm(x_bf, w_t, [(F32, nat), (BF16, tr)], _ep_with_t, row0=r_v, ncols=fox_w, name="proj_v")
        u, w["br_fox"], w["br_pool"], w["br_mem"] = _mm(
            x_bf, w_t, [(F32, nat)], _ep_identity, row0=r_u, ncols=pool_w,
            casts=[whole(w_br_fox[l]), whole(w_br_pool[l]), whole(w_br_mem[l])], name="proj_u")
        qm, w["out"] = _mm(x_bf, w_t, [(BF16, nat)], _ep_identity, row0=r_qm, ncols=mem_w,
                           casts=[whole(w_out[l])], name="proj_qm")
        logf = logf_of(x_bf)[0][:, :FOX_HEADS]
        c2_rows = _cumsum_lanes(logf.T, scale=LOG2E)
        o_foxT = _fox_prompt(q2T, k, vT, c2_rows)
        o_pool = _pool(u, pos_p, wp_bf, psc, tm=1024, zero_first=True)
        mkv, mkv_bf = _mm(mem_prompt.reshape(mem_tokens, d_model), w_mem_kv[l], [(F32, nat), (BF16, nat)],
                          _ep_dup, name="mem_kv")
        o_mem = _mem_prompt(qm, mkv_bf[:, :mem_w], mkv_bf[:, mem_w:])
        outs["pk"].append(k.reshape(batch, seq, FOX_HEADS, FOX_HD))
        outs["pv"].append(v.reshape(batch, seq, FOX_HEADS, FOX_HD))
        outs["pf"].append(logf.reshape(batch, seq, FOX_HEADS))
        outs["pp"].append(u[-POOL_STATE:].reshape(batch, POOL_STATE, pool_w))
        outs["pmk"].append(mkv[:, :mem_w].reshape(batch, mem_tokens, MEM_HEADS, MEM_HD))
        outs["pmv"].append(mkv[:, mem_w:].reshape(batch, mem_tokens, MEM_HEADS, MEM_HD))
        h_p, (o_mem_s, w["down"], w["up"]) = _mix(
            hp, x_bf, o_foxT, o_pool, o_mem, w, alpha=alpha, fox_transposed=True,
            rider=(qm_s.reshape(dec_batch, dec_seq * MEM_HEADS, MEM_HD), cache_mem_k[l], cache_mem_v[l]),
            merge_casts=[whole(w_down[l])], out_casts=[whole(w_up[l])])
        rows = dec_seq * FOX_HEADS
        hp, o_fox_s = _ffn_fox(
            h_p, w["up"], w["down"], w["ln2_g"], w["ln2_b"],
            q_s.reshape(dec_batch, rows, FOX_HD), k_s.reshape(dec_batch, rows, FOX_HD),
            v_s.reshape(dec_batch, rows, FOX_HD), logf_s.reshape(dec_batch, 1, rows),
            cache_fox_k[l].reshape(n_phys, page_size * FOX_HEADS, FOX_HD),
            cache_fox_v[l].reshape(n_phys, page_size * FOX_HEADS, FOX_HD),
            _page_logf_sums(cache_fox_logf[l]), page_table, alpha=alpha)

        u3 = u_s.reshape(dec_batch, dec_seq, pool_w)
        state = state_pool[l].astype(F32)
        u_ext = jnp.concatenate([jnp.zeros((dec_batch, POOL_HALO - POOL_STATE, pool_w), F32), state, u3],
                                axis=1).reshape(dec_batch * seg, pool_w)
        o_pool_s = _pool(u_ext, pos_s, wp_bf, psc, tm=32 * seg, zero_first=False)
        o_pool_s = o_pool_s.reshape(dec_batch, seg, pool_w)[:, POOL_HALO:].reshape(dec_batch * dec_seq, pool_w)
        outs["sk"].append(k_s.reshape(dec_batch, dec_seq, FOX_HEADS, FOX_HD))
        outs["sv"].append(v_s.reshape(dec_batch, dec_seq, FOX_HEADS, FOX_HD))
        outs["sf"].append(logf_s.reshape(dec_batch, dec_seq, FOX_HEADS))
        outs["sp"].append(jnp.concatenate([state, u3], axis=1)[:, -POOL_STATE:])
        h_s, _ = _mix(hs, xs_bf, o_fox_s.reshape(dec_batch * dec_seq, fox_w), o_pool_s,
                      o_mem_s.reshape(dec_batch * dec_seq, mem_w), w, alpha=alpha, fox_transposed=False)
        hs = _ffn(h_s, w["up"], w["down"], w["ln2_g"], w["ln2_b"], alpha=alpha)

    st = lambda n: jnp.stack(outs[n])
    return (hp.reshape(batch, seq, d_model), hs.reshape(dec_batch, dec_seq, d_model),
            st("pk"), st("pv"), st("pf"), st("pp"), st("pmk"), st("pmv"),
            st("sk"), st("sv"), st("sf"), st("sp"))
```
